```python
import math
import jax
import jax.numpy as jnp
from jax import lax
import numpy as np

D_MODEL = 2048
BATCH = 16
SEQ = 256
DEPTH = 4
DEC_BATCH = 4
DEC_SEQ = 1024
PAST_LEN = 256

GRID_W = 64
N_MIXERS = 4
Q_BLOCK = 128
EPS = 1e-6
ROPE_BASE = 10000.0
NEG = -1e30

N_A = (DEPTH + 3) // N_MIXERS
N_B = (DEPTH + 2) // N_MIXERS
N_C = (DEPTH + 1) // N_MIXERS
N_D = DEPTH // N_MIXERS

A_HEADS = 16
A_DH = 64
B_HEADS = 16
B_DK = 128
B_DV = D_MODEL // B_HEADS
B_CHUNK = 32
C_HEADS = 16
C_NOPE = 128
C_ROPE = 64
C_VD = 128
C_QLORA = 512
C_KVLORA = 256
D_HEADS = 32
D_KV_HEADS = 4
D_DH = 64
D_WINDOW = 128
D_FF = -(-8 * D_MODEL // (3 * 256)) * 256

kernel_name = 'hybrid_diffusion_prefix_trunk_step'


def rms_norm(x, g):
    xf = x.astype(jnp.float32)
    y = xf * lax.rsqrt(jnp.mean(xf * xf, axis=-1, keepdims=True) + EPS)
    return (y * g.astype(jnp.float32)).astype(x.dtype)


def modulate(h, shift, scale):
    return h * (1 + scale) + shift


def grid_positions(n_tok):
    rows = n_tok // GRID_W
    row = jnp.repeat(jnp.arange(rows), GRID_W)
    col = jnp.tile(jnp.arange(GRID_W), rows)
    return row, col


def rope_2d(x):
    n_tok, d = x.shape[1], x.shape[-1]
    row, col = grid_positions(n_tok)
    half = d // 2
    nf = half // 2
    inv_freq = ROPE_BASE ** (-jnp.arange(nf, dtype=jnp.float32) / nf)
    bshape = (n_tok,) + (1,) * (x.ndim - 3) + (nf,)

    def rot(xa, pos):
        ang = (pos.astype(jnp.float32)[:, None] * inv_freq[None, :]).reshape(bshape)
        cos, sin = jnp.cos(ang), jnp.sin(ang)
        x1 = xa[..., :nf].astype(jnp.float32)
        x2 = xa[..., nf:].astype(jnp.float32)
        return jnp.concatenate([x1 * cos - x2 * sin, x2 * cos + x1 * sin], axis=-1)

    y = jnp.concatenate([rot(x[..., :half], row), rot(x[..., half:], col)], axis=-1)
    return y.astype(x.dtype)


def map_query_blocks(body, q):
    b, s = q.shape[:2]
    nb = s // Q_BLOCK
    qb = jnp.moveaxis(q.reshape((b, nb, Q_BLOCK) + q.shape[2:]), 1, 0)
    out = lax.map(body, (qb, jnp.arange(nb)))
    return jnp.moveaxis(out, 0, 1).reshape((b, s) + out.shape[3:])


def softmax_with_sink(s, sink):
    m = jnp.maximum(jnp.max(s, axis=-1, keepdims=True), sink)
    e = jnp.exp(s - m)
    return e / (jnp.sum(e, axis=-1, keepdims=True) + jnp.exp(sink - m))


def gqa_core(q, k, v, sink=None):
    h, dh = q.shape[2], q.shape[3]
    g = k.shape[2]
    r = h // g
    scale = dh ** -0.5

    def body(args):
        qb, _ = args
        b, nq = qb.shape[:2]
        qg = qb.reshape(b, nq, g, r, dh)
        s = jnp.einsum('bqgrd,bkgd->bgrqk', qg, k).astype(jnp.float32) * scale
        if sink is None:
            p = jax.nn.softmax(s, axis=-1)
        else:
            p = softmax_with_sink(s, sink.astype(jnp.float32).reshape(g, r, 1, 1))
        o = jnp.einsum('bgrqk,bkge->bqgre', p.astype(v.dtype), v)
        return o.reshape(b, nq, h, o.shape[-1])

    return map_query_blocks(body, q)


def diff_lambda(lam_p, layer_idx):
    lam_init = 0.8 - 0.6 * math.exp(-0.3 * layer_idx)
    lp = lam_p.astype(jnp.float32)
    lam = jnp.exp(jnp.sum(lp[0] * lp[1])) - jnp.exp(jnp.sum(lp[2] * lp[3])) + lam_init
    return lam, lam_init


def diff_attn_project(h, wq, wk, wv):
    b, t, _ = h.shape
    q = (h @ wq).reshape(b, t, A_HEADS, 2, A_DH)
    k = (h @ wk).reshape(b, t, A_HEADS, 2, A_DH)
    v = (h @ wv).reshape(b, t, A_HEADS, 2 * A_DH)
    return q, k, v


def diff_attn_core(q, k, v, lam):
    scale = A_DH ** -0.5

    def body(args):
        qb, _ = args
        s = jnp.einsum('bqhcd,bkhcd->bhcqk', qb, k).astype(jnp.float32) * scale
        pr = jax.nn.softmax(s, axis=-1)
        w = pr[:, :, 0] - lam * pr[:, :, 1]
        return jnp.einsum('bhqk,bkhe->bqhe', w.astype(v.dtype), v)

    return map_query_blocks(body, q)


def diff_attn_output(o, subln, lam_init, wo):
    o = rms_norm(o, subln) * (1.0 - lam_init)
    return o.reshape(o.shape[:2] + (-1,)) @ wo


def diff_attn_context(h, wq, wk, wv, wo, lam_p, subln, layer_idx):
    b, t, _ = h.shape
    q, k, v = diff_attn_project(h, wq, wk, wv)
    lam, lam_init = diff_lambda(lam_p, layer_idx)
    y = diff_attn_output(diff_attn_core(q, k, v, lam), subln, lam_init, wo)
    return y, k.reshape(b, t, A_HEADS, 2 * A_DH), v


def diff_attn_latent(h, k_ctx, v_ctx, wq, wk, wv, wo, lam_p, subln, layer_idx):
    q, k, v = diff_attn_project(h, wq, wk, wv)
    q, k = rope_2d(q), rope_2d(k)
    b, n = k_ctx.shape[:2]
    k_all = jnp.concatenate([k_ctx.reshape(b, n, A_HEADS, 2, A_DH), k], axis=1)
    v_all = jnp.concatenate([v_ctx, v], axis=1)
    lam, lam_init = diff_lambda(lam_p, layer_idx)
    return diff_attn_output(diff_attn_core(q, k_all, v_all, lam), subln, lam_init, wo)


def hgrn2_lower_bound(lower, layer_idx):
    p = jax.nn.softmax(lower.astype(jnp.float32), axis=0)
    return (jnp.cumsum(p, axis=0) - p[0])[layer_idx]


def gla_chunked(q, k, v, logf, s0):
    b, t, h, _ = q.shape
    n = t // B_CHUNK

    def chunks(a):
        return jnp.moveaxis(a.reshape((b, n, B_CHUNK) + a.shape[2:]), 1, 0).astype(jnp.float32)

    qc, kc, vc, gc = chunks(q), chunks(k), chunks(v), chunks(logf)
    causal = jnp.tril(jnp.ones((B_CHUNK, B_CHUNK), dtype=bool))

    def step(s, inp):
        qi, ki, vi, gi = inp
        bcum = jnp.cumsum(gi, axis=1)
        q_dec = qi * jnp.exp(bcum)
        k_inv = ki * jnp.exp(-bcum)
        a = jnp.where(causal, jnp.einsum('bthd,bshd->bhts', q_dec, k_inv), 0.0)
        o = jnp.einsum('bhts,bshv->bthv', a, vi) + jnp.einsum('bthd,bhdv->bthv', q_dec, s)
        b_last = bcum[:, -1]
        k_end = ki * jnp.exp(b_last[:, None] - bcum)
        s_new = jnp.exp(b_last)[..., None] * s + jnp.einsum('bshd,bshv->bhdv', k_end, vi)
        return s_new, o

    s_fin, o = lax.scan(step, s0.astype(jnp.float32), (qc, kc, vc, gc))
    o = jnp.moveaxis(o, 0, 1).reshape(b, t, h, -1)
    return o, s_fin


def hgrn2_mixer(h, s0_fwd, s0_bwd, wq, wi, wf, wg, gnorm, wo, lb):
    b, t, _ = h.shape
    q = jax.nn.silu(h @ wq).reshape(b, t, B_HEADS, B_DK)
    inp = (h @ wi).reshape(b, t, B_HEADS, B_DV)
    outs, states = [], []
    for d, s0 in enumerate((s0_fwd, s0_bwd)):
        z = (h @ wf[d]).astype(jnp.float32)
        f = lb[d] + (1.0 - lb[d]) * jax.nn.sigmoid(z)
        logf = jnp.log(f).reshape(b, t, B_HEADS, B_DK)
        kk = (1.0 - f).reshape(b, t, B_HEADS, B_DK)
        qd, vd = q, inp
        if d == 1:
            qd, kk, vd, logf = (jnp.flip(a, axis=1) for a in (qd, kk, vd, logf))
        o, s = gla_chunked(qd, kk, vd, logf, s0)
        if d == 1:
            o = jnp.flip(o, axis=1)
        outs.append(o)
        states.append(s.astype(h.dtype))
    g = (h @ wg).reshape(b, t, B_HEADS, B_DV).astype(jnp.float32)
    o = rms_norm(outs[0] + outs[1], gnorm) * jax.nn.silu(g)
    y = o.reshape(b, t, B_HEADS * B_DV).astype(h.dtype) @ wo
    return y, states[0], states[1]


def mla_queries(h, wdq, qnorm, wuq):
    b, t, _ = h.shape
    q = (rms_norm(h @ wdq, qnorm) @ wuq).reshape(b, t, C_HEADS, C_NOPE + C_ROPE)
    return q[..., :C_NOPE], q[..., C_NOPE:]


def mla_compress(h, wdkv, kvnorm):
    ckv = h @ wdkv
    return rms_norm(ckv[..., :C_KVLORA], kvnorm), ckv[..., C_KVLORA:]


def mla_expand(ckv, kpe, wuk, wuv):
    b, n, _ = ckv.shape
    k_nope = (ckv @ wuk).reshape(b, n, C_HEADS, C_NOPE)
    v = (ckv @ wuv).reshape(b, n, C_HEADS, C_VD)
    k_pe = jnp.broadcast_to(kpe[:, :, None, :], (b, n, C_HEADS, C_ROPE))
    return jnp.concatenate([k_nope, k_pe], axis=-1), v


def mla_out(o, wo):
    return o.reshape(o.shape[:2] + (-1,)) @ wo


def mla_context(h, wdq, qnorm, wuq, wdkv, kvnorm, wuk, wuv, wo):
    q_nope, q_pe = mla_queries(h, wdq, qnorm, wuq)
    ckv, kpe = mla_compress(h, wdkv, kvnorm)
    k, v = mla_expand(ckv, kpe, wuk, wuv)
    o = gqa_core(jnp.concatenate([q_nope, q_pe], axis=-1), k, v)
    return mla_out(o, wo), ckv, kpe


def mla_latent(h, ckv_ctx, kpe_ctx, wdq, qnorm, wuq, wdkv, kvnorm, wuk, wuv, wo):
    q_nope, q_pe = mla_queries(h, wdq, qnorm, wuq)
    ckv, kpe = mla_compress(h, wdkv, kvnorm)
    q = jnp.concatenate([q_nope, rope_2d(q_pe)], axis=-1)
    ckv_all = jnp.concatenate([ckv_ctx, ckv], axis=1)
    kpe_all = jnp.concatenate([kpe_ctx, rope_2d(kpe)], axis=1)
    k, v = mla_expand(ckv_all, kpe_all, wuk, wuv)
    return mla_out(gqa_core(q, k, v), wo)


def swa_project(h, wq, wk, wv):
    b, t, _ = h.shape
    q = (h @ wq).reshape(b, t, D_HEADS, D_DH)
    k = (h @ wk).reshape(b, t, D_KV_HEADS, D_DH)
    v = (h @ wv).reshape(b, t, D_KV_HEADS, D_DH)
    return q, k, v


def swa_latent_attention(q, k, v, k_ctx, v_ctx, sink):
    b, t, h, dh = q.shape
    g = k.shape[2]
    r = h // g
    band = Q_BLOCK + 2 * D_WINDOW
    pad = ((0, 0), (D_WINDOW, D_WINDOW), (0, 0), (0, 0))
    k_pad = jnp.pad(k, pad)
    v_pad = jnp.pad(v, pad)
    sink_b = sink.astype(jnp.float32).reshape(g, r, 1, 1)
    scale = dh ** -0.5

    def body(args):
        qb, n = args
        start = n * Q_BLOCK
        kb = lax.dynamic_slice_in_dim(k_pad, start, band, axis=1)
        vb = lax.dynamic_slice_in_dim(v_pad, start, band, axis=1)
        qpos = start + jnp.arange(Q_BLOCK)
        kpos = start - D_WINDOW + jnp.arange(band)
        valid = ((jnp.abs(qpos[:, None] - kpos[None, :]) <= D_WINDOW)
                 & (kpos >= 0)[None, :] & (kpos < t)[None, :])
        qg = qb.reshape(b, Q_BLOCK, g, r, dh)
        s_ctx = jnp.einsum('bqgrd,bkgd->bgrqk', qg, k_ctx).astype(jnp.float32)
        s_lat = jnp.einsum('bqgrd,bkgd->bgrqk', qg, kb).astype(jnp.float32)
        s_lat = jnp.where(valid, s_lat, NEG)
        s = jnp.concatenate([s_ctx, s_lat], axis=-1) * scale
        p = softmax_with_sink(s, sink_b)
        vals = jnp.concatenate([v_ctx, vb], axis=1)
        o = jnp.einsum('bgrqk,bkge->bqgre', p.astype(v.dtype), vals)
        return o.reshape(b, Q_BLOCK, h, dh)

    return map_query_blocks(body, q)


def swa_context(h, wq, wk, wv, sink, wo):
    q, k, v = swa_project(h, wq, wk, wv)
    o = gqa_core(q, k, v, sink)
    return o.reshape(o.shape[:2] + (-1,)) @ wo, k, v


def swa_latent(h, k_ctx, v_ctx, wq, wk, wv, sink, wo):
    q, k, v = swa_project(h, wq, wk, wv)
    o = swa_latent_attention(rope_2d(q), rope_2d(k), v, k_ctx, v_ctx, sink)
    return o.reshape(o.shape[:2] + (-1,)) @ wo


def swiglu(h, wg, wu, wd):
    return (jax.nn.silu(h @ wg) * (h @ wu)) @ wd


def setup_inputs(seed: int = 0) -> dict:
    key = jax.random.key(seed)
    ks = iter(jax.random.split(key, 64))
    D = D_MODEL

    def nrm(shape, scale=1.0):
        return jax.random.normal(next(ks), shape, jnp.float32) * scale

    def lin(shape, mult=1.0):
        return nrm(shape, mult * shape[-2] ** -0.5)

    def gain(shape):
        return 1.0 + nrm(shape, 0.1)

    return {
        'x_prompt': nrm((BATCH, SEQ, D)),
        'x_sample': nrm((DEC_BATCH, DEC_SEQ, D)),
        'cache_a_k': nrm((DEC_BATCH, N_A, PAST_LEN, A_HEADS, 2 * A_DH)),
        'cache_a_v': nrm((DEC_BATCH, N_A, PAST_LEN, A_HEADS, 2 * A_DH)),
        'state_b_fwd': nrm((DEC_BATCH, N_B, B_HEADS, B_DK, B_DV), 0.5),
        'state_b_bwd': nrm((DEC_BATCH, N_B, B_HEADS, B_DK, B_DV), 0.5),
        'cache_c_ckv': nrm((DEC_BATCH, N_C, PAST_LEN, C_KVLORA)),
        'cache_c_kpe': nrm((DEC_BATCH, N_C, PAST_LEN, C_ROPE)),
        'cache_d_k': nrm((DEC_BATCH, N_D, PAST_LEN, D_KV_HEADS, D_DH)),
        'cache_d_v': nrm((DEC_BATCH, N_D, PAST_LEN, D_KV_HEADS, D_DH)),
        'c': nrm((DEC_BATCH, D)),
        'c_ctx': nrm((D,)),
        'ada_w': lin((DEPTH, D, 6 * D), 0.5),
        'ada_b': nrm((DEPTH, 6 * D), 0.02),
        'norm_mix_pre': gain((DEPTH, D)),
        'norm_mix_post': gain((DEPTH, D)),
        'norm_ffn_pre': gain((DEPTH, D)),
        'norm_ffn_post': gain((DEPTH, D)),
        'a_wq': lin((N_A, D, A_HEADS * 2 * A_DH)),
        'a_wk': lin((N_A, D, A_HEADS * 2 * A_DH)),
        'a_wv': lin((N_A, D, A_HEADS * 2 * A_DH)),
        'a_wo': lin((N_A, A_HEADS * 2 * A_DH, D)),
        'a_lambda': nrm((N_A, 4, A_DH), 0.1),
        'a_subln': gain((N_A, 2 * A_DH)),
        'b_wq': lin((N_B, D, B_HEADS * B_DK)),
        'b_wi': lin((N_B, D, B_HEADS * B_DV)),
        'b_wf': lin((N_B, 2, D, B_HEADS * B_DK)),
        'b_lower': gain((DEPTH, 2, B_HEADS * B_DK)),
        'b_wg': lin((N_B, D, B_HEADS * B_DV)),
        'b_gnorm': gain((N_B, B_DV)),
        'b_wo': lin((N_B, B_HEADS * B_DV, D)),
        'c_wdq': lin((N_C, D, C_QLORA)),
        'c_qnorm': gain((N_C, C_QLORA)),
        'c_wuq': lin((N_C, C_QLORA, C_HEADS * (C_NOPE + C_ROPE))),
        'c_wdkv': lin((N_C, D, C_KVLORA + C_ROPE)),
        'c_kvnorm': gain((N_C, C_KVLORA)),
        'c_wuk': lin((N_C, C_KVLORA, C_HEADS * C_NOPE)),
        'c_wuv': lin((N_C, C_KVLORA, C_HEADS * C_VD)),
        'c_wo': lin((N_C, C_HEADS * C_VD, D)),
        'd_wq': lin((N_D, D, D_HEADS * D_DH)),
        'd_wk': lin((N_D, D, D_KV_HEADS * D_DH)),
        'd_wv': lin((N_D, D, D_KV_HEADS * D_DH)),
        'd_sink': nrm((N_D, D_HEADS), 0.5),
        'd_wo': lin((N_D, D_HEADS * D_DH, D)),
        'ffn_wg': lin((DEPTH, D, D_FF)),
        'ffn_wu': lin((DEPTH, D, D_FF)),
        'ffn_wd': lin((DEPTH, D_FF, D)),
    }


def reference(x_prompt, x_sample, cache_a_k, cache_a_v, state_b_fwd, state_b_bwd,
              cache_c_ckv, cache_c_kpe, cache_d_k, cache_d_v, c, c_ctx,
              ada_w, ada_b, norm_mix_pre, norm_mix_post, norm_ffn_pre, norm_ffn_post,
              a_wq, a_wk, a_wv, a_wo, a_lambda, a_subln,
              b_wq, b_wi, b_wf, b_lower, b_wg, b_gnorm, b_wo,
              c_wdq, c_qnorm, c_wuq, c_wdkv, c_kvnorm, c_wuk, c_wuv, c_wo,
              d_wq, d_wk, d_wv, d_sink, d_wo,
              ffn_wg, ffn_wu, ffn_wd):
    ctx = x_prompt
    lat = x_sample
    new_a_k, new_a_v = [], []
    new_b_fwd, new_b_bwd = [], []
    new_c_ckv, new_c_kpe = [], []
    new_d_k, new_d_v = [], []
    for i in range(DEPTH):
        m, j = i % N_MIXERS, i // N_MIXERS
        mod_c = jnp.split(jax.nn.silu(c_ctx) @ ada_w[i] + ada_b[i], 6, axis=-1)
        mod_l = [u[:, None, :] for u in jnp.split(jax.nn.silu(c) @ ada_w[i] + ada_b[i], 6, axis=-1)]

        hc = modulate(rms_norm(ctx, norm_mix_pre[i]), mod_c[0], mod_c[1])
        hl = modulate(rms_norm(lat, norm_mix_pre[i]), mod_l[0], mod_l[1])
        if m == 0:
            oc, kc, vc = diff_attn_context(hc, a_wq[j], a_wk[j], a_wv[j], a_wo[j], a_lambda[j], a_subln[j], i)
            ol = diff_attn_latent(hl, cache_a_k[:, j], cache_a_v[:, j], a_wq[j], a_wk[j], a_wv[j],
                                  a_wo[j], a_lambda[j], a_subln[j], i)
            new_a_k.append(kc)
            new_a_v.append(vc)
        elif m == 1:
            lb = hgrn2_lower_bound(b_lower, i)
            zeros = jnp.zeros((ctx.shape[0], B_HEADS, B_DK, B_DV), ctx.dtype)
            oc, sf, sb = hgrn2_mixer(hc, zeros, zeros, b_wq[j], b_wi[j], b_wf[j], b_wg[j],
                                     b_gnorm[j], b_wo[j], lb)
            ol, _, _ = hgrn2_mixer(hl, state_b_fwd[:, j], state_b_bwd[:, j], b_wq[j], b_wi[j], b_wf[j],
                                   b_wg[j], b_gnorm[j], b_wo[j], lb)
            new_b_fwd.append(sf)
            new_b_bwd.append(sb)
        elif m == 2:
            oc, ckv, kpe = mla_context(hc, c_wdq[j], c_qnorm[j], c_wuq[j], c_wdkv[j], c_kvnorm[j],
                                       c_wuk[j], c_wuv[j], c_wo[j])
            ol = mla_latent(hl, cache_c_ckv[:, j], cache_c_kpe[:, j], c_wdq[j], c_qnorm[j], c_wuq[j],
                            c_wdkv[j], c_kvnorm[j], c_wuk[j], c_wuv[j], c_wo[j])
            new_c_ckv.append(ckv)
            new_c_kpe.append(kpe)
        else:
            oc, kc, vc = swa_context(hc, d_wq[j], d_wk[j], d_wv[j], d_sink[j], d_wo[j])
            ol = swa_latent(hl, cache_d_k[:, j], cache_d_v[:, j], d_wq[j], d_wk[j], d_wv[j],
                            d_sink[j], d_wo[j])
            new_d_k.append(kc)
            new_d_v.append(vc)
        ctx = ctx + mod_c[2] * rms_norm(oc, norm_mix_post[i])
        lat = lat + mod_l[2] * rms_norm(ol, norm_mix_post[i])

        hc = modulate(rms_norm(ctx, norm_ffn_pre[i]), mod_c[3], mod_c[4])
        hl = modulate(rms_norm(lat, norm_ffn_pre[i]), mod_l[3], mod_l[4])
        ctx = ctx + mod_c[5] * rms_norm(swiglu(hc, ffn_wg[i], ffn_wu[i], ffn_wd[i]), norm_ffn_post[i])
        lat = lat + mod_l[5] * rms_norm(swiglu(hl, ffn_wg[i], ffn_wu[i], ffn_wd[i]), norm_ffn_post[i])

    return (ctx, lat,
            jnp.stack(new_a_k, axis=1), jnp.stack(new_a_v, axis=1),
            jnp.stack(new_b_fwd, axis=1), jnp.stack(new_b_bwd, axis=1),
            jnp.stack(new_c_ckv, axis=1), jnp.stack(new_c_kpe, axis=1),
            jnp.stack(new_d_k, axis=1), jnp.stack(new_d_v, axis=1))
```

```python
import functools
import math

import numpy as np
import jax
import jax.numpy as jnp
from jax import lax
from jax.experimental import pallas as pl
from jax.experimental.pallas import tpu as pltpu

EPS = 1e-6
ROPE_BASE = 10000.0
GRID_W = 64
NEG = -1e30
N_MIXERS = 4
A_HEADS, A_DH = 16, 64
B_HEADS, B_DK, B_CHUNK = 16, 128, 32
C_HEADS, C_NOPE, C_ROPE, C_VD = 16, 128, 64, 128
D_HEADS, D_KV_HEADS, D_DH, D_WINDOW, Q_BLOCK = 32, 4, 64, 128, 128

LANES = 128
SUBLANES = 8
VMEM_CAP_MIB = 56

MXU = jnp.bfloat16
F32 = jnp.float32


def _cparams(sem, vmem_mib):
    return pltpu.CompilerParams(dimension_semantics=sem,
                                vmem_limit_bytes=min(vmem_mib, VMEM_CAP_MIB) << 20)


def _dot(a, b):
    return jnp.dot(a.astype(MXU), b.astype(MXU), preferred_element_type=F32)


def _dot_nt(a, b):
    return lax.dot_general(a.astype(MXU), b.astype(MXU), (((1,), (1,)), ((), ())),
                           preferred_element_type=F32)


def _dot_tn(a, b):
    return lax.dot_general(a.astype(MXU), b.astype(MXU), (((0,), (0,)), ((), ())),
                           preferred_element_type=F32)


def _sigmoid(x):
    return 1.0 / (1.0 + jnp.exp(-x))


def _silu(x):
    return x * _sigmoid(x)


def _rms(x, g):
    ms = jnp.mean(x * x, axis=-1, keepdims=True)
    return x * lax.rsqrt(ms + EPS) * g


def _largest_tile(n, pref):
    t = min(n, pref)
    while n % t:
        t -= SUBLANES
    assert t > 0 and n % t == 0
    return t


class _Rows:
    def __init__(self, b, s, db, ds, tm):
        self.b, self.s, self.db, self.ds = b, s, db, ds
        self.n_ctx, self.n_lat = b * s, db * ds
        self.mt = self.n_ctx + self.n_lat
        self.tm = tm
        assert self.n_ctx % tm == 0 and ds % tm == 0 and self.n_ctx % ds == 0
        self.nct = self.n_ctx // tm
        self.ntiles = self.mt // tm
        self.crows = -(-(db + 1) // SUBLANES) * SUBLANES

    def cond_row(self, i):
        lat = ((i - self.nct) * self.tm) // self.ds
        return jnp.where(i < self.nct, self.db, lat)

    def pos_block(self, i):
        return (jnp.maximum(i - self.nct, 0) * self.tm % self.ds) // self.tm


def _rope_tables(n_tok):
    nf = 16
    inv = (ROPE_BASE ** (-np.arange(nf, dtype=np.float32) / nf)).astype(np.float32)
    t = np.arange(n_tok)
    row, col = (t // GRID_W).astype(np.float32), (t % GRID_W).astype(np.float32)
    lane = np.arange(64)
    pos = np.where(lane[None, :] < 32, row[:, None], col[:, None]).astype(np.float32)
    ang = (pos * inv[lane % nf][None, :]).astype(np.float32)
    cos, sin = np.cos(ang).astype(np.float32), np.sin(ang).astype(np.float32)
    first = (lane % 32) < 16
    sa = np.where(first[None, :], -sin, 0.0).astype(np.float32)
    sb = np.where(first[None, :], 0.0, sin).astype(np.float32)
    tile2 = lambda a: np.concatenate([a, a], axis=1)
    return jnp.asarray(tile2(cos)), jnp.asarray(tile2(sa)), jnp.asarray(tile2(sb))


def _rope128(y, cos, sa, sb):
    return (y * cos + pltpu.roll(y, LANES - 16, 1) * sa + pltpu.roll(y, 16, 1) * sb)


def _ada_kernel(c_ref, w_ref, b_ref, o_ref):
    o_ref[...] = _dot(_silu(c_ref[...]), w_ref[...]) + b_ref[...]


def _ada(cond, ada_w, ada_b):
    depth, d, n = ada_w.shape
    cr = cond.shape[0]
    tn = _largest_tile(n, 1024)
    return pl.pallas_call(
        _ada_kernel,
        out_shape=jax.ShapeDtypeStruct((depth, cr, n), F32),
        grid=(depth, n // tn),
        in_specs=[pl.BlockSpec((cr, d), lambda l, j: (0, 0)),
                  pl.BlockSpec((None, d, tn), lambda l, j: (l, 0, j)),
                  pl.BlockSpec((None, 1, tn), lambda l, j: (l, 0, j))],
        out_specs=pl.BlockSpec((None, cr, tn), lambda l, j: (l, 0, j)),
        compiler_params=_cparams(("parallel", "parallel"), 40),
        name="ada_mod",
    )(cond, ada_w, ada_b.reshape(depth, 1, n))


def _prep_kernel(xp_ref, xs_ref, g_ref, mod_ref, x_ref, h_ref, *, rows, d):
    i = pl.program_id(0)
    x = jnp.where(i < rows.nct, xp_ref[...], xs_ref[...])
    cr = rows.cond_row(i)
    shift = mod_ref[pl.ds(cr, 1), pl.ds(0, d)]
    scale = mod_ref[pl.ds(cr, 1), pl.ds(d, d)]
    x_ref[...] = x
    h_ref[...] = (_rms(x, g_ref[...]) * (1.0 + scale) + shift).astype(h_ref.dtype)


def _prep(xp, xs, g, mod, layer, rows):
    d = xp.shape[1]
    rows = _Rows(rows.b, rows.s, rows.db, rows.ds, _largest_tile(rows.tm, 256))
    tm, nct = rows.tm, rows.nct
    return pl.pallas_call(
        functools.partial(_prep_kernel, rows=rows, d=d),
        out_shape=(jax.ShapeDtypeStruct((rows.mt, d), F32),
                   jax.ShapeDtypeStruct((rows.mt, d), MXU)),
        grid=(rows.ntiles,),
        in_specs=[pl.BlockSpec((tm, d), lambda i: (jnp.minimum(i, nct - 1), 0)),
                  pl.BlockSpec((tm, d), lambda i: (jnp.maximum(i - nct, 0), 0)),
                  pl.BlockSpec((None, 1, d), lambda i: (layer, 0, 0)),
                  pl.BlockSpec((None, rows.crows, mod.shape[2]), lambda i: (layer, 0, 0))],
        out_specs=(pl.BlockSpec((tm, d), lambda i: (i, 0)),
                   pl.BlockSpec((tm, d), lambda i: (i, 0))),
        compiler_params=_cparams(("parallel",), 40 * tm * d // (1 << 20) + 8),
        name="prep",
    )(xp, xs, g, mod)


def _proj_kernel(*refs, n_w, n_extra, n_out, epi):
    h_ref = refs[0]
    w_refs = refs[1:1 + n_w]
    extra = refs[1 + n_w:1 + n_w + n_extra]
    outs = refs[1 + n_w + n_extra:1 + n_w + n_extra + n_out]
    wb_refs = refs[1 + n_w + n_extra + n_out:]
    m = pl.program_id(1)

    @pl.when(m == 0)
    def _():
        for w_ref, wb_ref in zip(w_refs, wb_refs):
            wb_ref[...] = w_ref[...].astype(wb_ref.dtype)

    h = h_ref[...]
    ys = [jnp.dot(h, wb_ref[...], preferred_element_type=F32) for wb_ref in wb_refs]
    epi(ys, m, extra, outs)


def _proj(h, ws, epi, outs, rows, tn, extra=(), name="proj"):
    mt, k = h.shape
    n = ws[0][0].shape[2]
    assert n % tn == 0
    tm = rows.tm
    nj = n // tn
    in_specs = [pl.BlockSpec((tm, k), lambda j, m: (m, 0))]
    in_specs += [pl.BlockSpec((None, k, tn), lambda j, m, li=li: (li, 0, j)) for _, li in ws]
    in_specs += [s for _, s in extra]
    out_shape = tuple(jax.ShapeDtypeStruct((mt, c * nj), dt) for c, dt in outs)
    out_specs = tuple(pl.BlockSpec((tm, c), lambda j, m: (m, j)) for c, _ in outs)
    out_bytes = sum(tm * c * jnp.dtype(dt).itemsize for c, dt in outs)
    vmem = (2 * tm * k * 2 + len(ws) * (2 * k * tn * 4 + k * tn * 2) + 2 * out_bytes
            + len(ws) * tm * tn * 8) // (1 << 20) + 6
    res = pl.pallas_call(
        functools.partial(_proj_kernel, n_w=len(ws), n_extra=len(extra), n_out=len(outs), epi=epi),
        out_shape=out_shape,
        grid=(nj, rows.ntiles),
        in_specs=in_specs,
        out_specs=out_specs,
        scratch_shapes=[pltpu.VMEM((k, tn), MXU) for _ in ws],
        compiler_params=_cparams(("parallel", "arbitrary"), vmem),
        name=name,
    )(h, *[w for w, _ in ws], *[a for a, _ in extra])
    return res


def _epi_store(ys, m, extra, outs):
    outs[0][...] = ys[0].astype(outs[0].dtype)


def _epi_silu(ys, m, extra, outs):
    outs[0][...] = _silu(ys[0]).astype(outs[0].dtype)


def _epi_swiglu(ys, m, extra, outs):
    outs[0][...] = (_silu(ys[0]) * ys[1]).astype(outs[0].dtype)


def _make_epi_rope(rows, tn):
    def epi(ys, m, extra, outs):
        cos_ref, sa_ref, sb_ref = extra
        y, o_ref = ys[0], outs[0]

        @pl.when(m < rows.nct)
        def _():
            o_ref[...] = y.astype(o_ref.dtype)

        @pl.when(m >= rows.nct)
        def _():
            cos, sa, sb = cos_ref[...], sa_ref[...], sb_ref[...]
            for c in range(tn // LANES):
                sl = slice(c * LANES, (c + 1) * LANES)
                o_ref[:, sl] = _rope128(y[:, sl], cos, sa, sb).astype(o_ref.dtype)
    return epi


def _rope_extra(tabs, rows):
    tm = rows.tm
    spec = pl.BlockSpec((tm, LANES), lambda j, m: (rows.pos_block(m), 0))
    return [(t, spec) for t in tabs]


def _make_epi_forget(layer_idx, depth):
    def epi(ys, m, extra, outs):
        bl = extra[0][...]
        e = jnp.exp(bl - jnp.max(bl, axis=0, keepdims=True))
        p = e / jnp.sum(e, axis=0, keepdims=True)
        lb = jnp.sum(p[0:layer_idx + 1], axis=0, keepdims=True) - p[0:1]
        outs[0][...] = lb + (1.0 - lb) * _sigmoid(ys[0])
    return epi


def _make_epi_rmsnorm():
    def epi(ys, m, extra, outs):
        outs[0][...] = _rms(ys[0], extra[0][...]).astype(outs[0].dtype)
    return epi


def _make_epi_ckv(rows):
    def epi(ys, m, extra, outs):
        g_ref, cos_ref, sa_ref, sb_ref = extra
        y = ys[0]
        outs[0][...] = _rms(y[:, :C_KVLORA], g_ref[...])
        kpe = y[:, C_KVLORA:C_KVLORA + C_ROPE]

        @pl.when(m < rows.nct)
        def _():
            outs[1][...] = kpe

        @pl.when(m >= rows.nct)
        def _():
            k2 = jnp.concatenate([kpe, kpe], axis=1)
            r = _rope128(k2, cos_ref[...], sa_ref[...], sb_ref[...])
            outs[1][...] = r[:, :C_ROPE]
    return epi


C_KVLORA = 256


def _cast_kernel(w_ref, o_ref):
    o_ref[...] = w_ref[...].astype(o_ref.dtype)


def _cast_mxu(w):
    l, k, n = w.shape
    tr = _largest_tile(k, 512)
    return pl.pallas_call(
        _cast_kernel,
        out_shape=jax.ShapeDtypeStruct(w.shape, MXU),
        grid=(l, k // tr),
        in_specs=[pl.BlockSpec((None, tr, n), lambda i, r: (i, r, 0))],
        out_specs=pl.BlockSpec((None, tr, n), lambda i, r: (i, r, 0)),
        compiler_params=_cparams(("parallel", "parallel"), 12 * tr * n // (1 << 20) + 8),
        name="cast_w",
    )(w)


def _resid_kernel(*refs, rows, d, nk, gate_chunk, next_chunk):
    a_ref, w_ref, x_ref, modc_ref, gpost_ref = refs[:5]
    pos = 5
    if next_chunk is not None:
        modn_ref, gpre_ref = refs[pos:pos + 2]
        pos += 2
    xo_ref = refs[pos]
    pos += 1
    if next_chunk is not None:
        ho_ref = refs[pos]
        pos += 1
    scratch = refs[pos:]
    i = pl.program_id(0)
    k = pl.program_id(1)

    def finish(y):
        cr = rows.cond_row(i)
        gate = modc_ref[pl.ds(cr, 1), pl.ds(gate_chunk * d, d)]
        xn = x_ref[...] + gate * _rms(y, gpost_ref[...])
        xo_ref[...] = xn
        if next_chunk is not None:
            shift = modn_ref[pl.ds(cr, 1), pl.ds(next_chunk * d, d)]
            scale = modn_ref[pl.ds(cr, 1), pl.ds((next_chunk + 1) * d, d)]
            ho_ref[...] = (_rms(xn, gpre_ref[...]) * (1.0 + scale) + shift).astype(ho_ref.dtype)

    part = jnp.dot(a_ref[...], w_ref[...], preferred_element_type=F32)
    if nk == 1:
        finish(part)
    else:
        acc_ref = scratch[0]

        @pl.when(k == 0)
        def _():
            acc_ref[...] = part

        @pl.when(k > 0)
        def _():
            acc_ref[...] += part

        @pl.when(k == nk - 1)
        def _():
            finish(acc_ref[...])


def _resid(a, w, li, x, mod, layer, gate_chunk, gpost, rows, nxt=None, name="resid"):
    mt, kdim = a.shape
    d = w.shape[2]
    tm = _largest_tile(rows.tm, 512)
    sub = _Rows(rows.b, rows.s, rows.db, rows.ds, tm)
    resident = kdim * d * 2 <= (8 << 20)
    tk = kdim if resident else _largest_tile(kdim, 512)
    nk = kdim // tk
    mspec = lambda l: pl.BlockSpec((None, rows.crows, mod.shape[2]), lambda i, k: (l, 0, 0))
    gspec = lambda l: pl.BlockSpec((None, 1, d), lambda i, k: (l, 0, 0))
    w_spec = pl.BlockSpec((None, tk, d), lambda i, k: (li, k, 0),
                          pipeline_mode=pl.Buffered(1 if resident else 2))
    in_specs = [pl.BlockSpec((tm, tk), lambda i, k: (i, k)), w_spec,
                pl.BlockSpec((tm, d), lambda i, k: (i, 0)), mspec(layer), gspec(layer)]
    args = [a, w, x, mod, gpost]
    out_shape = [jax.ShapeDtypeStruct((mt, d), F32)]
    out_specs = [pl.BlockSpec((tm, d), lambda i, k: (i, 0))]
    if nxt is not None:
        nl, nchunk, gpre = nxt
        in_specs += [mspec(nl), gspec(nl)]
        args += [mod, gpre]
        out_shape.append(jax.ShapeDtypeStruct((mt, d), MXU))
        out_specs.append(pl.BlockSpec((tm, d), lambda i, k: (i, 0)))
    else:
        nchunk = None
    scratch = [] if nk == 1 else [pltpu.VMEM((tm, d), F32)]
    vmem = (2 * tk * d * 2 + 2 * tm * tk * 2 + tm * d * (8 + 8 + 4 + 4 + 12)) // (1 << 20) + 6
    res = pl.pallas_call(
        functools.partial(_resid_kernel, rows=sub, d=d, nk=nk, gate_chunk=gate_chunk,
                          next_chunk=nchunk),
        out_shape=tuple(out_shape),
        grid=(mt // tm, nk),
        in_specs=in_specs,
        out_specs=tuple(out_specs),
        scratch_shapes=scratch,
        compiler_params=_cparams(("arbitrary", "arbitrary"), vmem),
        name=name,
    )(*args)
    return res if nxt is not None else (res[0], None)


def _softmax_parts(parts):
    m = functools.reduce(jnp.maximum, [jnp.max(s, axis=-1, keepdims=True) for s in parts])
    es = [jnp.exp(s - m) for s in parts]
    den = functools.reduce(lambda a, b: a + b, [jnp.sum(e, axis=-1, keepdims=True) for e in es])
    inv = 1.0 / den
    return [e * inv for e in es]


def _attn_a_kernel(*refs, hb, qc, lam_init, has_cache):
    q_ref, k_ref, v_ref = refs[:3]
    pos = 3
    if has_cache:
        kc_ref, vc_ref = refs[pos:pos + 2]
        pos += 2
    lam_ref, sub_ref, o_ref = refs[pos:pos + 3]
    sq = q_ref.shape[0]
    scale = A_DH ** -0.5
    lp = lam_ref[...]
    lam = (jnp.exp(jnp.sum(lp[0:1] * lp[1:2], axis=1, keepdims=True))
           - jnp.exp(jnp.sum(lp[2:3] * lp[3:4], axis=1, keepdims=True)) + lam_init)
    lo = lax.broadcasted_iota(jnp.int32, (1, 2 * A_DH), 1) < A_DH
    sub = sub_ref[...]
    for h in range(hb):
        cols = slice(h * 2 * A_DH, (h + 1) * 2 * A_DH)
        keys = [k_ref[:, cols].astype(MXU)]
        vals = [v_ref[:, cols].astype(MXU)]
        if has_cache:
            keys.append(kc_ref[:, cols].astype(MXU))
            vals.append(vc_ref[:, cols].astype(MXU))

        def body(c, carry, cols=cols, keys=keys, vals=vals):
            r = pl.ds(pl.multiple_of(c * qc, qc), qc)
            q = q_ref[r, cols]
            zero = jnp.zeros_like(q)
            p1 = _softmax_parts([_dot_nt(jnp.where(lo, q, zero), kk) * scale for kk in keys])
            p2 = _softmax_parts([_dot_nt(jnp.where(lo, zero, q), kk) * scale for kk in keys])
            o = functools.reduce(lambda a, b: a + b,
                                 [_dot(a - lam * b, vv) for a, b, vv in zip(p1, p2, vals)])
            o_ref[r, cols] = (_rms(o, sub) * (1.0 - lam_init)).astype(o_ref.dtype)
            return carry

        lax.fori_loop(0, sq // qc, body, 0)


def _attn_a(q, k, v, cache_k, cache_v, lam_p, subln, layer_idx, j, rows, prev=None):
    d = q.shape[1]
    lam_init = 0.8 - 0.6 * math.exp(-0.3 * layer_idx)
    has_cache = cache_k is not None
    if has_cache:
        t, nb, roff = rows.ds, rows.db, rows.n_ctx // rows.ds
        hb = 4
    else:
        t, nb, roff = rows.s, rows.b, 0
        hb = A_HEADS
    cw = hb * 2 * A_DH
    qc = _largest_tile(t, 256)
    tok = lambda b, g: (b + roff, g)
    in_specs = [pl.BlockSpec((t, cw), tok)] * 3
    args = [q, k, v]
    if has_cache:
        p = cache_k.shape[2]
        cspec = pl.BlockSpec((None, None, p, cw), lambda b, g: (b, j, 0, g))
        in_specs += [cspec, cspec]
        args += [cache_k, cache_v]
    in_specs += [pl.BlockSpec((None, 4, A_DH), lambda b, g: (j, 0, 0)),
                 pl.BlockSpec((None, 1, 2 * A_DH), lambda b, g: (j, 0, 0))]
    args += [lam_p, subln]
    aliases = {}
    if prev is not None:
        in_specs.append(pl.BlockSpec(memory_space=pl.ANY))
        args.append(prev)
        aliases = {len(args) - 1: 0}
    kern = functools.partial(_attn_a_kernel, hb=hb, qc=qc, lam_init=lam_init, has_cache=has_cache)
    if prev is not None:
        kern = _drop_last_input(kern, len(args))
    return pl.pallas_call(
        kern,
        out_shape=jax.ShapeDtypeStruct((rows.mt, d), MXU),
        grid=(nb, d // cw),
        in_specs=in_specs,
        out_specs=pl.BlockSpec((t, cw), tok),
        input_output_aliases=aliases,
        compiler_params=_cparams(("parallel", "parallel"), 40),
        name="attn_a_lat" if has_cache else "attn_a_ctx",
    )(*args)


def _drop_last_input(kern, n_in):
    def wrapped(*refs):
        return kern(*refs[:n_in - 1], *refs[n_in:])
    return wrapped


def _gla_kernel(*refs, t, has_s0, has_sout):
    q_ref, v_ref, f0_ref, f1_ref, g_ref, gn_ref = refs[:6]
    pos = 6
    if has_s0:
        s0f_ref, s0b_ref = refs[pos:pos + 2]
        pos += 2
    o_ref = refs[pos]
    pos += 1
    if has_sout:
        sf_ref, sb_ref = refs[pos:pos + 2]
        pos += 2
    qdf, kif, kef, qdb, kib, keb, vb, bcf, bsb, of, ob = refs[pos:]
    c = B_CHUNK
    n = t // c
    r32 = lax.broadcasted_iota(jnp.int32, (t, 1), 0) % c

    def prefix(x):
        s = 1
        while s < c:
            x = x + jnp.where(r32 >= s, pltpu.roll(x, s, 0), 0.0)
            s *= 2
        return x

    def suffix(x):
        s = 1
        while s < c:
            x = x + jnp.where(r32 < c - s, pltpu.roll(x, t - s, 0), 0.0)
            s *= 2
        return x

    q = q_ref[...]
    vb[...] = v_ref[...].astype(vb.dtype)
    f = f0_ref[...]
    lg = jnp.log(f)
    kk = 1.0 - f
    bc = prefix(lg)
    tot = bc + suffix(lg) - lg
    bcf[...] = bc
    qdf[...] = (q * jnp.exp(bc)).astype(qdf.dtype)
    kif[...] = (kk * jnp.exp(-bc)).astype(kif.dtype)
    kef[...] = (kk * jnp.exp(tot - bc)).astype(kef.dtype)
    f = f1_ref[...]
    lg = jnp.log(f)
    kk = 1.0 - f
    bs = suffix(lg)
    tot = bs + prefix(lg) - lg
    bsb[...] = bs
    qdb[...] = (q * jnp.exp(bs)).astype(qdb.dtype)
    kib[...] = (kk * jnp.exp(-bs)).astype(kib.dtype)
    keb[...] = (kk * jnp.exp(tot - bs)).astype(keb.dtype)

    ti = lax.broadcasted_iota(jnp.int32, (c, c), 0)
    si = lax.broadcasted_iota(jnp.int32, (c, c), 1)
    lower, upper = ti >= si, ti <= si

    def step(i, carry):
        sf, sb = carry
        rf = pl.ds(pl.multiple_of(i * c, c), c)
        a = jnp.where(lower, _dot_nt(qdf[rf, :], kif[rf, :]), 0.0)
        of[rf, :] = _dot(a, vb[rf, :]) + _dot_nt(qdf[rf, :], sf)
        dec = jnp.exp(bcf[pl.ds(i * c + c - 1, 1), :])
        sf = sf * dec + _dot_tn(vb[rf, :], kef[rf, :])
        ib = n - 1 - i
        rb = pl.ds(pl.multiple_of(ib * c, c), c)
        a = jnp.where(upper, _dot_nt(qdb[rb, :], kib[rb, :]), 0.0)
        ob[rb, :] = _dot(a, vb[rb, :]) + _dot_nt(qdb[rb, :], sb)
        dec = jnp.exp(bsb[pl.ds(ib * c, 1), :])
        sb = sb * dec + _dot_tn(vb[rb, :], keb[rb, :])
        return sf, sb

    if has_s0:
        init = (s0f_ref[...].T, s0b_ref[...].T)
    else:
        init = (jnp.zeros((B_DK, B_DK), F32), jnp.zeros((B_DK, B_DK), F32))
    sf, sb = lax.fori_loop(0, n, step, init)
    if has_sout:
        sf_ref[...] = sf.T
        sb_ref[...] = sb.T
    o = of[...] + ob[...]
    o_ref[...] = (_rms(o, gn_ref[...]) * _silu(g_ref[...])).astype(o_ref.dtype)


def _gla(q, v, f0, f1, g, gnorm, s0f, s0b, j, rows, prev=None):
    d = q.shape[1]
    has_s0 = s0f is not None
    if has_s0:
        t, nb, roff = rows.ds, rows.db, rows.n_ctx // rows.ds
    else:
        t, nb, roff = rows.s, rows.b, 0
    dv = d // B_HEADS
    assert dv == LANES and B_DK == LANES
    tok = pl.BlockSpec((t, LANES), lambda b, h: (b + roff, h))
    in_specs = [tok] * 5 + [pl.BlockSpec((None, 1, dv), lambda b, h: (j, 0, 0))]
    args = [q, v, f0, f1, g, gnorm]
    if has_s0:
        sspec = pl.BlockSpec((None, None, None, B_DK, dv), lambda b, h: (b, j, h, 0, 0))
        in_specs += [sspec, sspec]
        args += [s0f, s0b]
    out_shape = [jax.ShapeDtypeStruct((rows.mt, d), MXU)]
    out_specs = [tok]
    if not has_s0:
        ospec = pl.BlockSpec((None, None, B_DK, dv), lambda b, h: (b, h, 0, 0))
        out_shape += [jax.ShapeDtypeStruct((nb, B_HEADS, B_DK, dv), F32)] * 2
        out_specs += [ospec, ospec]
    aliases = {}
    kern = functools.partial(_gla_kernel, t=t, has_s0=has_s0, has_sout=not has_s0)
    if prev is not None:
        in_specs.append(pl.BlockSpec(memory_space=pl.ANY))
        args.append(prev)
        aliases = {len(args) - 1: 0}
        kern = _drop_last_input(kern, len(args))
    scratch = ([pltpu.VMEM((t, LANES), MXU) for _ in range(7)]
               + [pltpu.VMEM((t, LANES), F32) for _ in range(4)])
    return pl.pallas_call(
        kern,
        out_shape=tuple(out_shape),
        grid=(nb, B_HEADS),
        in_specs=in_specs,
        out_specs=tuple(out_specs),
        scratch_shapes=scratch,
        input_output_aliases=aliases,
        compiler_params=_cparams(("parallel", "parallel"), 32),
        name="gla_lat" if has_s0 else "gla_ctx",
    )(*args)


def _attn_c_kernel(*refs, qc, has_cache):
    qn_ref, qp_ref, ckv_ref, kpe_ref = refs[:4]
    pos = 4
    if has_cache:
        ckvc_ref, kpec_ref = refs[pos:pos + 2]
        pos += 2
    wuk_ref, wuv_ref, o_ref = refs[pos:pos + 3]
    scr = refs[pos + 3:]
    sq = qn_ref.shape[0]
    scale = (C_NOPE + C_ROPE) ** -0.5
    wuk = wuk_ref[...].astype(MXU)
    wuv = wuv_ref[...].astype(MXU)

    def expand(ckv_r, kpe_r, kn_r, vn_r, kp_r):
        ck = ckv_r[...].astype(MXU)
        kn_r[...] = jnp.dot(ck, wuk, preferred_element_type=F32).astype(kn_r.dtype)
        vn_r[...] = jnp.dot(ck, wuv, preferred_element_type=F32).astype(vn_r.dtype)
        kp = kpe_r[...]
        kp_r[...] = jnp.concatenate([kp, kp], axis=1).astype(kp_r.dtype)

    groups = [scr[0:3]]
    expand(ckv_ref, kpe_ref, *scr[0:3])
    if has_cache:
        groups.append(scr[3:6])
        expand(ckvc_ref, kpec_ref, *scr[3:6])
    lo = lax.broadcasted_iota(jnp.int32, (1, LANES), 1) < C_ROPE

    def body(c, carry):
        r = pl.ds(pl.multiple_of(c * qc, qc), qc)
        qp = qp_ref[r, :]
        zero = jnp.zeros_like(qp)
        for hh in range(2):
            cols = slice(hh * C_NOPE, (hh + 1) * C_NOPE)
            qn = qn_ref[r, cols]
            qph = jnp.where(lo, qp, zero) if hh == 0 else jnp.where(lo, zero, qp)
            parts = [(_dot_nt(qn, kn_r[:, cols]) + _dot_nt(qph, kp_r[...])) * scale
                     for kn_r, _, kp_r in groups]
            ps = _softmax_parts(parts)
            o = functools.reduce(lambda a, b: a + b,
                                 [_dot(p, vn_r[:, cols]) for p, (_, vn_r, _) in zip(ps, groups)])
            o_ref[r, cols] = o.astype(o_ref.dtype)
        return carry

    lax.fori_loop(0, sq // qc, body, 0)


def _attn_c(qn, qp, ckv, kpe, cache_ckv, cache_kpe, wuk, wuv, j, rows, prev=None):
    has_cache = cache_ckv is not None
    if has_cache:
        t, nb, roff = rows.ds, rows.db, rows.n_ctx // rows.ds
    else:
        t, nb, roff = rows.s, rows.b, 0
    hw = 2 * C_NOPE
    qc = _largest_tile(t, 256)
    in_specs = [pl.BlockSpec((t, hw), lambda b, g: (b + roff, g)),
                pl.BlockSpec((t, 2 * C_ROPE), lambda b, g: (b + roff, g)),
                pl.BlockSpec((t, C_KVLORA), lambda b, g: (b + roff, 0)),
                pl.BlockSpec((t, C_ROPE), lambda b, g: (b + roff, 0))]
    args = [qn, qp, ckv, kpe]
    scratch = [pltpu.VMEM((t, hw), MXU), pltpu.VMEM((t, hw), MXU), pltpu.VMEM((t, LANES), MXU)]
    if has_cache:
        p = cache_ckv.shape[2]
        in_specs += [pl.BlockSpec((None, None, p, C_KVLORA), lambda b, g: (b, j, 0, 0)),
                     pl.BlockSpec((None, None, p, C_ROPE), lambda b, g: (b, j, 0, 0))]
        args += [cache_ckv, cache_kpe]
        scratch += [pltpu.VMEM((p, hw), MXU), pltpu.VMEM((p, hw), MXU), pltpu.VMEM((p, LANES), MXU)]
    in_specs += [pl.BlockSpec((None, C_KVLORA, hw), lambda b, g: (j, 0, g))] * 2
    args += [wuk, wuv]
    aliases = {}
    kern = functools.partial(_attn_c_kernel, qc=qc, has_cache=has_cache)
    if prev is not None:
        in_specs.append(pl.BlockSpec(memory_space=pl.ANY))
        args.append(prev)
        aliases = {len(args) - 1: 0}
        kern = _drop_last_input(kern, len(args))
    return pl.pallas_call(
        kern,
        out_shape=jax.ShapeDtypeStruct((rows.mt, C_HEADS * C_VD), MXU),
        grid=(nb, C_HEADS // 2),
        in_specs=in_specs,
        out_specs=pl.BlockSpec((t, hw), lambda b, g: (b + roff, g)),
        scratch_shapes=scratch,
        input_output_aliases=aliases,
        compiler_params=_cparams(("parallel", "parallel"), 32),
        name="attn_c_lat" if has_cache else "attn_c_ctx",
    )(*args)


def _attn_d_kernel(*refs, has_cache):
    q_ref, k_ref, v_ref = refs[:3]
    pos = 3
    if has_cache:
        kc_ref, vc_ref = refs[pos:pos + 2]
        pos += 2
    sink_ref, o_ref = refs[pos:pos + 2]
    k2, vlo, vhi = refs[pos + 2:pos + 5]
    if has_cache:
        k2c, vloc, vhic = refs[pos + 5:pos + 8]
    t = q_ref.shape[0]
    qb = Q_BLOCK
    scale = D_DH ** -0.5
    r = D_HEADS // D_KV_HEADS
    win = min(t, qb + 2 * D_WINDOW)
    lo = lax.broadcasted_iota(jnp.int32, (1, LANES), 1) < D_DH

    def stage(k_r, v_r, k2_r, vlo_r, vhi_r, g):
        gs = slice(g * D_DH, (g + 1) * D_DH)
        kg, vg = k_r[:, gs], v_r[:, gs]
        zero = jnp.zeros_like(vg)
        k2_r[...] = jnp.concatenate([kg, kg], axis=1).astype(k2_r.dtype)
        vlo_r[...] = jnp.concatenate([vg, zero], axis=1).astype(vlo_r.dtype)
        vhi_r[...] = jnp.concatenate([zero, vg], axis=1).astype(vhi_r.dtype)

    for g in range(D_KV_HEADS):
        stage(k_ref, v_ref, k2, vlo, vhi, g)
        if has_cache:
            stage(kc_ref, vc_ref, k2c, vloc, vhic, g)
        for pr in range(r // 2):
            h0 = g * r + 2 * pr
            cols = slice(h0 * D_DH, (h0 + 2) * D_DH)
            sinks = [sink_ref[0:1, h0:h0 + 1], sink_ref[0:1, h0 + 1:h0 + 2]]

            def body(c, carry, cols=cols, sinks=sinks):
                rq = pl.ds(pl.multiple_of(c * qb, qb), qb)
                q = q_ref[rq, cols]
                zero = jnp.zeros_like(q)
                if has_cache:
                    ws = pl.multiple_of(jnp.clip(c * qb - D_WINDOW, 0, t - win), qb)
                    rk = pl.ds(ws, win)
                    qpos = c * qb + lax.broadcasted_iota(jnp.int32, (qb, 1), 0)
                    kpos = ws + lax.broadcasted_iota(jnp.int32, (1, win), 1)
                    valid = jnp.abs(qpos - kpos) <= D_WINDOW
                    kw, vl, vh = k2[rk, :], vlo[rk, :], vhi[rk, :]
                else:
                    kw, vl, vh = k2[...], vlo[...], vhi[...]
                o = None
                for e, (qm, vv, vvc) in enumerate(((jnp.where(lo, q, zero), vl, vloc if has_cache else None),
                                                   (jnp.where(lo, zero, q), vh, vhic if has_cache else None))):
                    s = _dot_nt(qm, kw)
                    if has_cache:
                        s = jnp.where(valid, s, NEG)
                        parts = [_dot_nt(qm, k2c[...]) * scale, s * scale]
                        vs = [vvc[...], vv]
                    else:
                        parts = [s * scale]
                        vs = [vv]
                    sink = sinks[e]
                    m = functools.reduce(jnp.maximum,
                                         [jnp.max(x, axis=-1, keepdims=True) for x in parts])
                    m = jnp.maximum(m, sink)
                    es = [jnp.exp(x - m) for x in parts]
                    den = functools.reduce(lambda a, b: a + b,
                                           [jnp.sum(x, axis=-1, keepdims=True) for x in es])
                    inv = 1.0 / (den + jnp.exp(sink - m))
                    for x, vx in zip(es, vs):
                        term = _dot(x * inv, vx)
                        o = term if o is None else o + term
                o_ref[rq, cols] = o.astype(o_ref.dtype)
                return carry

            lax.fori_loop(0, t // qb, body, 0)


def _attn_d(q, k, v, cache_k, cache_v, sink, j, rows, prev=None):
    d = q.shape[1]
    kvw = D_KV_HEADS * D_DH
    has_cache = cache_k is not None
    if has_cache:
        t, nb, roff = rows.ds, rows.db, rows.n_ctx // rows.ds
    else:
        t, nb, roff = rows.s, rows.b, 0
    assert t % Q_BLOCK == 0
    in_specs = [pl.BlockSpec((t, d), lambda b: (b + roff, 0)),
                pl.BlockSpec((t, kvw), lambda b: (b + roff, 0)),
                pl.BlockSpec((t, kvw), lambda b: (b + roff, 0))]
    args = [q, k, v]
    scratch = [pltpu.VMEM((t, LANES), MXU) for _ in range(3)]
    if has_cache:
        p = cache_k.shape[2]
        cspec = pl.BlockSpec((None, None, p, kvw), lambda b: (b, j, 0, 0))
        in_specs += [cspec, cspec]
        args += [cache_k, cache_v]
        scratch += [pltpu.VMEM((p, LANES), MXU) for _ in range(3)]
    in_specs.append(pl.BlockSpec((None, 1, D_HEADS), lambda b: (j, 0, 0)))
    args.append(sink)
    aliases = {}
    kern = functools.partial(_attn_d_kernel, has_cache=has_cache)
    if prev is not None:
        in_specs.append(pl.BlockSpec(memory_space=pl.ANY))
        args.append(prev)
        aliases = {len(args) - 1: 0}
        kern = _drop_last_input(kern, len(args))
    return pl.pallas_call(
        kern,
        out_shape=jax.ShapeDtypeStruct((rows.mt, d), MXU),
        grid=(nb,),
        in_specs=in_specs,
        out_specs=pl.BlockSpec((t, d), lambda b: (b + roff, 0)),
        scratch_shapes=scratch,
        input_output_aliases=aliases,
        compiler_params=_cparams(("parallel",), 32),
        name="attn_d_lat" if has_cache else "attn_d_ctx",
    )(*args)


def kernel(x_prompt, x_sample, cache_a_k, cache_a_v, state_b_fwd, state_b_bwd, cache_c_ckv,
           cache_c_kpe, cache_d_k, cache_d_v, c, c_ctx, ada_w, ada_b, norm_mix_pre,
           norm_mix_post, norm_ffn_pre, norm_ffn_post, a_wq, a_wk, a_wv, a_wo, a_lambda,
           a_subln, b_wq, b_wi, b_wf, b_lower, b_wg, b_gnorm, b_wo, c_wdq, c_qnorm, c_wuq,
           c_wdkv, c_kvnorm, c_wuk, c_wuv, c_wo, d_wq, d_wk, d_wv, d_sink, d_wo, ffn_wg,
           ffn_wu, ffn_wd):
    b, s, d = x_prompt.shape
    db, ds, _ = x_sample.shape
    depth = ada_w.shape[0]
    rows = _Rows(b, s, db, ds, _largest_tile(math.gcd(b * s, ds), 1024))
    n_ctx = rows.n_ctx

    cond = jnp.zeros((rows.crows, d), F32).at[:db].set(c).at[db].set(c_ctx)
    mod = _ada(cond, ada_w, ada_b)
    g3 = lambda a: a.reshape(a.shape[0], 1, a.shape[1])
    n_mix_pre, n_mix_post = g3(norm_mix_pre), g3(norm_mix_post)
    n_ffn_pre, n_ffn_post = g3(norm_ffn_pre), g3(norm_ffn_post)
    tabs = _rope_tables(ds)
    rope_epi = lambda tn: (_make_epi_rope(rows, tn), _rope_extra(tabs, rows))

    x, h = _prep(x_prompt.reshape(n_ctx, d), x_sample.reshape(rows.n_lat, d),
                 n_mix_pre, mod, 0, rows)

    flat4 = lambda a: a.reshape(a.shape[:3] + (-1,))
    wo_b = {0: _cast_mxu(a_wo), 1: _cast_mxu(b_wo), 2: _cast_mxu(c_wo), 3: _cast_mxu(d_wo)}
    wd_b = _cast_mxu(ffn_wd)
    wf = b_wf.reshape((-1,) + b_wf.shape[2:])
    outs = {k_: [] for k_ in ("a_k", "a_v", "b_f", "b_b", "c_ckv", "c_kpe", "d_k", "d_v")}
    for i in range(depth):
        m, j = i % N_MIXERS, i // N_MIXERS
        if m == 0:
            tn = _largest_tile(a_wq.shape[2], 1024)
            epi, ext = rope_epi(tn)
            q, = _proj(h, [(a_wq, j)], epi, [(tn, MXU)], rows, tn, ext, name="a_q")
            k, = _proj(h, [(a_wk, j)], epi, [(tn, F32)], rows, tn, ext, name="a_k")
            v, = _proj(h, [(a_wv, j)], _epi_store, [(tn, F32)], rows, tn, name="a_v")
            o = _attn_a(q, k, v, None, None, a_lambda, g3(a_subln), i, j, rows)
            o = _attn_a(q, k, v, flat4(cache_a_k), flat4(cache_a_v), a_lambda, g3(a_subln),
                        i, j, rows, prev=o)
            outs["a_k"].append(k[:n_ctx].reshape(b, s, A_HEADS, 2 * A_DH))
            outs["a_v"].append(v[:n_ctx].reshape(b, s, A_HEADS, 2 * A_DH))
        elif m == 1:
            tn = _largest_tile(b_wq.shape[2], 1024)
            q, = _proj(h, [(b_wq, j)], _epi_silu, [(tn, F32)], rows, tn, name="b_q")
            vi, = _proj(h, [(b_wi, j)], _epi_store, [(tn, F32)], rows, tn, name="b_i")
            g, = _proj(h, [(b_wg, j)], _epi_store, [(tn, F32)], rows, tn, name="b_g")
            fs = []
            for dr in range(2):
                bl_spec = pl.BlockSpec((None, depth, tn), lambda jn, mm, dr=dr: (dr, 0, jn))
                f, = _proj(h, [(wf, 2 * j + dr)], _make_epi_forget(i, depth), [(tn, F32)], rows,
                           tn, [(jnp.swapaxes(b_lower, 0, 1), bl_spec)], name="b_f%d" % dr)
                fs.append(f)
            o, sf, sb = _gla(q, vi, fs[0], fs[1], g, g3(b_gnorm), None, None, j, rows)
            o, = _gla(q, vi, fs[0], fs[1], g, g3(b_gnorm), state_b_fwd, state_b_bwd, j, rows,
                      prev=o)
            outs["b_f"].append(sf)
            outs["b_b"].append(sb)
        elif m == 2:
            nq = c_wdq.shape[2]
            cq, = _proj(h, [(c_wdq, j)], _make_epi_rmsnorm(), [(nq, MXU)], rows, nq,
                        [(g3(c_qnorm), pl.BlockSpec((None, 1, nq), lambda jn, mm: (j, 0, 0)))],
                        name="c_dq")
            nkv = c_wdkv.shape[2]
            ext = [(g3(c_kvnorm), pl.BlockSpec((None, 1, C_KVLORA), lambda jn, mm: (j, 0, 0)))]
            ext += _rope_extra(tabs, rows)
            ckv, kpe = _proj(h, [(c_wdkv, j)], _make_epi_ckv(rows),
                             [(C_KVLORA, F32), (C_ROPE, F32)], rows, nkv, ext, name="c_dkv")
            wuq = c_wuq.reshape(c_wuq.shape[0], nq, C_HEADS, C_NOPE + C_ROPE)
            wuq_n = wuq[:, :, :, :C_NOPE].reshape(-1, nq, C_HEADS * C_NOPE)
            wuq_p = wuq[:, :, :, C_NOPE:].reshape(-1, nq, C_HEADS * C_ROPE)
            tn = _largest_tile(wuq_n.shape[2], 1024)
            qn, = _proj(cq, [(wuq_n, j)], _epi_store, [(tn, MXU)], rows, tn, name="c_qn")
            tnp = wuq_p.shape[2]
            epi, ext = rope_epi(tnp)
            qp, = _proj(cq, [(wuq_p, j)], epi, [(tnp, MXU)], rows, tnp, ext, name="c_qp")
            o = _attn_c(qn, qp, ckv, kpe, None, None, c_wuk, c_wuv, j, rows)
            o = _attn_c(qn, qp, ckv, kpe, cache_c_ckv, cache_c_kpe, c_wuk, c_wuv, j, rows, prev=o)
            outs["c_ckv"].append(ckv[:n_ctx].reshape(b, s, C_KVLORA))
            outs["c_kpe"].append(kpe[:n_ctx].reshape(b, s, C_ROPE))
        else:
            tn = _largest_tile(d_wq.shape[2], 1024)
            epi, ext = rope_epi(tn)
            q, = _proj(h, [(d_wq, j)], epi, [(tn, MXU)], rows, tn, ext, name="d_q")
            kvw = d_wk.shape[2]
            epi, ext = rope_epi(kvw)
            k, = _proj(h, [(d_wk, j)], epi, [(kvw, F32)], rows, kvw, ext, name="d_k")
            v, = _proj(h, [(d_wv, j)], _epi_store, [(kvw, F32)], rows, kvw, name="d_v")
            o = _attn_d(q, k, v, None, None, g3(d_sink), j, rows)
            o = _attn_d(q, k, v, flat4(cache_d_k), flat4(cache_d_v), g3(d_sink), j, rows, prev=o)
            outs["d_k"].append(k[:n_ctx].reshape(b, s, D_KV_HEADS, D_DH))
            outs["d_v"].append(v[:n_ctx].reshape(b, s, D_KV_HEADS, D_DH))

        x, h = _resid(o, wo_b[m], j, x, mod, i, 2, n_mix_post, rows, nxt=(i, 3, n_ffn_pre),
                      name="mix_out")
        tf = _largest_tile(ffn_wg.shape[2], 512)
        a, = _proj(h, [(ffn_wg, i), (ffn_wu, i)], _epi_swiglu, [(tf, MXU)], rows, tf,
                   name="ffn_gu")
        nxt = (i + 1, 0, n_mix_pre) if i + 1 < depth else None
        x, h = _resid(a, wd_b, i, x, mod, i, 5, n_ffn_post, rows, nxt=nxt, name="ffn_down")

    stack = lambda lst: jnp.stack(lst, axis=1)
    return (x[:n_ctx].reshape(b, s, d), x[n_ctx:].reshape(db, ds, d),
            stack(outs["a_k"]), stack(outs["a_v"]), stack(outs["b_f"]), stack(outs["b_b"]),
            stack(outs["c_ckv"]), stack(outs["c_kpe"]), stack(outs["d_k"]), stack(outs["d_v"]))
```

```python
import functools
import math

import numpy as np
import jax
import jax.numpy as jnp
from jax import lax
from jax.experimental import pallas as pl
from jax.experimental.pallas import tpu as pltpu

EPS = 1e-6
ROPE_BASE = 10000.0
GRID_W = 64
NEG = -1e30
N_MIXERS = 4
A_HEADS, A_DH = 16, 64
B_HEADS, B_DK, B_CHUNK = 16, 128, 32
C_HEADS, C_NOPE, C_ROPE, C_VD = 16, 128, 64, 128
D_HEADS, D_KV_HEADS, D_DH, D_WINDOW, Q_BLOCK = 32, 4, 64, 128, 128

LANES = 128
SUBLANES = 8
VMEM_CAP_MIB = 56

MXU = jnp.bfloat16
F32 = jnp.float32


def _cparams(sem, vmem_mib):
    return pltpu.CompilerParams(dimension_semantics=sem,
                                vmem_limit_bytes=min(vmem_mib, VMEM_CAP_MIB) << 20)


def _dot(a, b):
    return jnp.dot(a.astype(MXU), b.astype(MXU), preferred_element_type=F32)


def _dot_nt(a, b):
    return lax.dot_general(a.astype(MXU), b.astype(MXU), (((1,), (1,)), ((), ())),
                           preferred_element_type=F32)


def _dot_tn(a, b):
    return lax.dot_general(a.astype(MXU), b.astype(MXU), (((0,), (0,)), ((), ())),
                           preferred_element_type=F32)


def _sigmoid(x):
    return 1.0 / (1.0 + jnp.exp(-x))


def _silu(x):
    return x * _sigmoid(x)


def _rms(x, g):
    ms = jnp.mean(x * x, axis=-1, keepdims=True)
    return x * lax.rsqrt(ms + EPS) * g


def _largest_tile(n, pref):
    t = min(n, pref)
    while n % t:
        t -= SUBLANES
    assert t > 0 and n % t == 0
    return t


class _Rows:
    def __init__(self, b, s, db, ds, tm):
        self.b, self.s, self.db, self.ds = b, s, db, ds
        self.n_ctx, self.n_lat = b * s, db * ds
        self.mt = self.n_ctx + self.n_lat
        self.tm = tm
        assert self.n_ctx % tm == 0 and ds % tm == 0 and self.n_ctx % ds == 0
        self.nct = self.n_ctx // tm
        self.ntiles = self.mt // tm
        self.crows = -(-(db + 1) // SUBLANES) * SUBLANES

    def cond_row(self, i):
        lat = ((i - self.nct) * self.tm) // self.ds
        return jnp.where(i < self.nct, self.db, lat)

    def pos_block(self, i):
        return (jnp.maximum(i - self.nct, 0) * self.tm % self.ds) // self.tm


def _rope_tables(n_tok):
    nf = 16
    inv = (ROPE_BASE ** (-np.arange(nf, dtype=np.float32) / nf)).astype(np.float32)
    t = np.arange(n_tok)
    row, col = (t // GRID_W).astype(np.float32), (t % GRID_W).astype(np.float32)
    lane = np.arange(64)
    pos = np.where(lane[None, :] < 32, row[:, None], col[:, None]).astype(np.float32)
    ang = (pos * inv[lane % nf][None, :]).astype(np.float32)
    cos, sin = np.cos(ang).astype(np.float32), np.sin(ang).astype(np.float32)
    first = (lane % 32) < 16
    sa = np.where(first[None, :], -sin, 0.0).astype(np.float32)
    sb = np.where(first[None, :], 0.0, sin).astype(np.float32)
    tile2 = lambda a: np.concatenate([a, a], axis=1)
    return jnp.asarray(tile2(cos)), jnp.asarray(tile2(sa)), jnp.asarray(tile2(sb))


def _rope128(y, cos, sa, sb):
    return (y * cos + pltpu.roll(y, LANES - 16, 1) * sa + pltpu.roll(y, 16, 1) * sb)


def _ada_kernel(c_ref, w_ref, b_ref, o_ref):
    o_ref[...] = _dot(_silu(c_ref[...]), w_ref[...]) + b_ref[...]


def _ada(cond, ada_w, ada_b):
    depth, d, n = ada_w.shape
    cr = cond.shape[0]
    tn = _largest_tile(n, 1024)
    return pl.pallas_call(
        _ada_kernel,
        out_shape=jax.ShapeDtypeStruct((depth, cr, n), F32),
        grid=(depth, n // tn),
        in_specs=[pl.BlockSpec((cr, d), lambda l, j: (0, 0)),
                  pl.BlockSpec((None, d, tn), lambda l, j: (l, 0, j)),
                  pl.BlockSpec((None, 1, tn), lambda l, j: (l, 0, j))],
        out_specs=pl.BlockSpec((None, cr, tn), lambda l, j: (l, 0, j)),
        compiler_params=_cparams(("parallel", "parallel"), 40),
        name="ada_mod",
    )(cond, ada_w, ada_b.reshape(depth, 1, n))


def _prep_kernel(xp_ref, xs_ref, g_ref, mod_ref, x_ref, h_ref, *, rows, d):
    i = pl.program_id(0)
    x = jnp.where(i < rows.nct, xp_ref[...], xs_ref[...])
    cr = rows.cond_row(i)
    shift = mod_ref[pl.ds(cr, 1), pl.ds(0, d)]
    scale = mod_ref[pl.ds(cr, 1), pl.ds(d, d)]
    x_ref[...] = x
    h_ref[...] = (_rms(x, g_ref[...]) * (1.0 + scale) + shift).astype(h_ref.dtype)


def _prep(xp, xs, g, mod, layer, rows):
    d = xp.shape[1]
    rows = _Rows(rows.b, rows.s, rows.db, rows.ds, _largest_tile(rows.tm, 256))
    tm, nct = rows.tm, rows.nct
    return pl.pallas_call(
        functools.partial(_prep_kernel, rows=rows, d=d),
        out_shape=(jax.ShapeDtypeStruct((rows.mt, d), F32),
                   jax.ShapeDtypeStruct((rows.mt, d), MXU)),
        grid=(rows.ntiles,),
        in_specs=[pl.BlockSpec((tm, d), lambda i: (jnp.minimum(i, nct - 1), 0)),
                  pl.BlockSpec((tm, d), lambda i: (jnp.maximum(i - nct, 0), 0)),
                  pl.BlockSpec((None, 1, d), lambda i: (layer, 0, 0)),
                  pl.BlockSpec((None, rows.crows, mod.shape[2]), lambda i: (layer, 0, 0))],
        out_specs=(pl.BlockSpec((tm, d), lambda i: (i, 0)),
                   pl.BlockSpec((tm, d), lambda i: (i, 0))),
        compiler_params=_cparams(("parallel",), 40 * tm * d // (1 << 20) + 8),
        name="prep",
    )(xp, xs, g, mod)


def _proj_kernel(*refs, n_w, n_extra, n_out, epi):
    h_ref = refs[0]
    w_refs = refs[1:1 + n_w]
    extra = refs[1 + n_w:1 + n_w + n_extra]
    outs = refs[1 + n_w + n_extra:1 + n_w + n_extra + n_out]
    wb_refs = refs[1 + n_w + n_extra + n_out:]
    m = pl.program_id(1)

    @pl.when(m == 0)
    def _():
        for w_ref, wb_ref in zip(w_refs, wb_refs):
            wb_ref[...] = w_ref[...].astype(wb_ref.dtype)

    h = h_ref[...]
    ys = [jnp.dot(h, wb_ref[...], preferred_element_type=F32) for wb_ref in wb_refs]
    epi(ys, m, extra, outs)


def _proj(h, ws, epi, outs, rows, tn, extra=(), name="proj"):
    mt, k = h.shape
    n = ws[0][0].shape[2]
    assert n % tn == 0
    tm = rows.tm
    nj = n // tn
    in_specs = [pl.BlockSpec((tm, k), lambda j, m: (m, 0))]
    in_specs += [pl.BlockSpec((None, k, tn), lambda j, m, li=li: (li, 0, j)) for _, li in ws]
    in_specs += [s for _, s in extra]
    out_shape = tuple(jax.ShapeDtypeStruct((mt, c * nj), dt) for c, dt in outs)
    out_specs = tuple(pl.BlockSpec((tm, c), lambda j, m: (m, j)) for c, _ in outs)
    out_bytes = sum(tm * c * jnp.dtype(dt).itemsize for c, dt in outs)
    vmem = (2 * tm * k * 2 + len(ws) * (2 * k * tn * 4 + k * tn * 2) + 2 * out_bytes
            + len(ws) * tm * tn * 8) // (1 << 20) + 6
    res = pl.pallas_call(
        functools.partial(_proj_kernel, n_w=len(ws), n_extra=len(extra), n_out=len(outs), epi=epi),
        out_shape=out_shape,
        grid=(nj, rows.ntiles),
        in_specs=in_specs,
        out_specs=out_specs,
        scratch_shapes=[pltpu.VMEM((k, tn), MXU) for _ in ws],
        compiler_params=_cparams(("parallel", "arbitrary"), vmem),
        name=name,
    )(h, *[w for w, _ in ws], *[a for a, _ in extra])
    return res


def _epi_store(ys, m, extra, outs):
    outs[0][...] = ys[0].astype(outs[0].dtype)


def _epi_silu(ys, m, extra, outs):
    outs[0][...] = _silu(ys[0]).astype(outs[0].dtype)


def _epi_swiglu(ys, m, extra, outs):
    outs[0][...] = (_silu(ys[0]) * ys[1]).astype(outs[0].dtype)


def _make_epi_rope(rows, tn):
    def epi(ys, m, extra, outs):
        cos_ref, sa_ref, sb_ref = extra
        y, o_ref = ys[0], outs[0]

        @pl.when(m < rows.nct)
        def _():
            o_ref[...] = y.astype(o_ref.dtype)

        @pl.when(m >= rows.nct)
        def _():
            cos, sa, sb = cos_ref[...], sa_ref[...], sb_ref[...]
            for c in range(tn // LANES):
                sl = slice(c * LANES, (c + 1) * LANES)
                o_ref[:, sl] = _rope128(y[:, sl], cos, sa, sb).astype(o_ref.dtype)
    return epi


def _rope_extra(tabs, rows):
    tm = rows.tm
    spec = pl.BlockSpec((tm, LANES), lambda j, m: (rows.pos_block(m), 0))
    return [(t, spec) for t in tabs]


def _make_epi_forget(layer_idx, depth):
    def epi(ys, m, extra, outs):
        bl = extra[0][...]
        e = jnp.exp(bl - jnp.max(bl, axis=0, keepdims=True))
        p = e / jnp.sum(e, axis=0, keepdims=True)
        lb = jnp.sum(p[0:layer_idx + 1], axis=0, keepdims=True) - p[0:1]
        outs[0][...] = lb + (1.0 - lb) * _sigmoid(ys[0])
    return epi


def _make_epi_rmsnorm():
    def epi(ys, m, extra, outs):
        outs[0][...] = _rms(ys[0], extra[0][...]).astype(outs[0].dtype)
    return epi


def _make_epi_ckv(rows):
    def epi(ys, m, extra, outs):
        g_ref, cos_ref, sa_ref, sb_ref = extra
        y = ys[0]
        outs[0][...] = _rms(y[:, :C_KVLORA], g_ref[...])
        kpe = y[:, C_KVLORA:C_KVLORA + C_ROPE]

        @pl.when(m < rows.nct)
        def _():
            outs[1][...] = kpe

        @pl.when(m >= rows.nct)
        def _():
            k2 = jnp.concatenate([kpe, kpe], axis=1)
            r = _rope128(k2, cos_ref[...], sa_ref[...], sb_ref[...])
            outs[1][...] = r[:, :C_ROPE]
    return epi


C_KVLORA = 256


def _cast_kernel(w_ref, o_ref):
    o_ref[...] = w_ref[...].astype(o_ref.dtype)


def _cast_mxu(w):
    l, k, n = w.shape
    tr = _largest_tile(k, 512)
    return pl.pallas_call(
        _cast_kernel,
        out_shape=jax.ShapeDtypeStruct(w.shape, MXU),
        grid=(l, k // tr),
        in_specs=[pl.BlockSpec((None, tr, n), lambda i, r: (i, r, 0))],
        out_specs=pl.BlockSpec((None, tr, n), lambda i, r: (i, r, 0)),
        compiler_params=_cparams(("parallel", "parallel"), 12 * tr * n // (1 << 20) + 8),
        name="cast_w",
    )(w)


def _resid_kernel(*refs, rows, d, nsub, gate_chunk, next_chunk):
    a_ref, w_ref, x_ref, modc_ref, gpost_ref = refs[:5]
    pos = 5
    if next_chunk is not None:
        modn_ref, gpre_ref = refs[pos:pos + 2]
        pos += 2
    xo_ref = refs[pos]
    pos += 1
    if next_chunk is not None:
        ho_ref = refs[pos]
        pos += 1
    i = pl.program_id(0)
    cr = rows.cond_row(i)
    gate = modc_ref[pl.ds(cr, 1), pl.ds(gate_chunk * d, d)] * gpost_ref[...]
    if next_chunk is not None:
        shift = modn_ref[pl.ds(cr, 1), pl.ds(next_chunk * d, d)]
        scale = (1.0 + modn_ref[pl.ds(cr, 1), pl.ds((next_chunk + 1) * d, d)]) * gpre_ref[...]
    w = w_ref[...]
    tm = a_ref.shape[0]
    rs = tm // nsub
    ys = [jnp.dot(a_ref[s * rs:(s + 1) * rs, :], w, preferred_element_type=F32)
          for s in range(nsub)]
    for s, y in enumerate(ys):
        r = slice(s * rs, (s + 1) * rs)
        yn = y * lax.rsqrt(jnp.mean(y * y, axis=-1, keepdims=True) + EPS)
        xn = x_ref[r, :] + yn * gate
        xo_ref[r, :] = xn
        if next_chunk is not None:
            hn = xn * lax.rsqrt(jnp.mean(xn * xn, axis=-1, keepdims=True) + EPS)
            ho_ref[r, :] = (hn * scale + shift).astype(ho_ref.dtype)


def _resid(a, w, li, x, mod, layer, gate_chunk, gpost, rows, nxt=None, nsub=1, name="resid"):
    mt, kdim = a.shape
    d = w.shape[2]
    row_bytes = 2 * kdim * 2 + d * (8 + 8 + 4 + 12)
    budget = (VMEM_CAP_MIB - 10 << 20) - kdim * d * 2
    tm = _largest_tile(rows.tm, max(SUBLANES, 1 << int(math.log2(budget // row_bytes))))
    sub = _Rows(rows.b, rows.s, rows.db, rows.ds, tm)
    mspec = lambda l: pl.BlockSpec((None, rows.crows, mod.shape[2]), lambda i: (l, 0, 0))
    gspec = lambda l: pl.BlockSpec((None, 1, d), lambda i: (l, 0, 0))
    in_specs = [pl.BlockSpec((tm, kdim), lambda i: (i, 0)),
                pl.BlockSpec((None, kdim, d), lambda i: (li, 0, 0), pipeline_mode=pl.Buffered(1)),
                pl.BlockSpec((tm, d), lambda i: (i, 0)), mspec(layer), gspec(layer)]
    args = [a, w, x, mod, gpost]
    out_shape = [jax.ShapeDtypeStruct((mt, d), F32)]
    out_specs = [pl.BlockSpec((tm, d), lambda i: (i, 0))]
    if nxt is not None:
        nl, nchunk, gpre = nxt
        in_specs += [mspec(nl), gspec(nl)]
        args += [mod, gpre]
        out_shape.append(jax.ShapeDtypeStruct((mt, d), MXU))
        out_specs.append(pl.BlockSpec((tm, d), lambda i: (i, 0)))
    else:
        nchunk = None
    vmem = (kdim * d * 2 + tm * row_bytes) // (1 << 20) + 8
    res = pl.pallas_call(
        functools.partial(_resid_kernel, rows=sub, d=d, nsub=nsub, gate_chunk=gate_chunk,
                          next_chunk=nchunk),
        out_shape=tuple(out_shape),
        grid=(mt // tm,),
        in_specs=in_specs,
        out_specs=tuple(out_specs),
        compiler_params=_cparams(("arbitrary",), vmem),
        name=name,
    )(*args)
    return res if nxt is not None else (res[0], None)


def _softmax_parts(parts):
    m = functools.reduce(jnp.maximum, [jnp.max(s, axis=-1, keepdims=True) for s in parts])
    es = [jnp.exp(s - m) for s in parts]
    den = functools.reduce(lambda a, b: a + b, [jnp.sum(e, axis=-1, keepdims=True) for e in es])
    inv = 1.0 / den
    return [e * inv for e in es]


def _attn_a_kernel(*refs, hb, qc, lam_init, has_cache):
    q_ref, k_ref, v_ref = refs[:3]
    pos = 3
    if has_cache:
        kc_ref, vc_ref = refs[pos:pos + 2]
        pos += 2
    lam_ref, sub_ref, o_ref = refs[pos:pos + 3]
    sq = q_ref.shape[0]
    scale = A_DH ** -0.5
    lp = lam_ref[...]
    lam = (jnp.exp(jnp.sum(lp[0:1] * lp[1:2], axis=1, keepdims=True))
           - jnp.exp(jnp.sum(lp[2:3] * lp[3:4], axis=1, keepdims=True)) + lam_init)
    lo = lax.broadcasted_iota(jnp.int32, (1, 2 * A_DH), 1) < A_DH
    sub = sub_ref[...]
    for h in range(hb):
        cols = slice(h * 2 * A_DH, (h + 1) * 2 * A_DH)
        keys = [k_ref[:, cols].astype(MXU)]
        vals = [v_ref[:, cols].astype(MXU)]
        if has_cache:
            keys.append(kc_ref[:, cols].astype(MXU))
            vals.append(vc_ref[:, cols].astype(MXU))

        def body(c, carry, cols=cols, keys=keys, vals=vals):
            r = pl.ds(pl.multiple_of(c * qc, qc), qc)
            q = q_ref[r, cols]
            zero = jnp.zeros_like(q)
            p1 = _softmax_parts([_dot_nt(jnp.where(lo, q, zero), kk) * scale for kk in keys])
            p2 = _softmax_parts([_dot_nt(jnp.where(lo, zero, q), kk) * scale for kk in keys])
            o = functools.reduce(lambda a, b: a + b,
                                 [_dot(a - lam * b, vv) for a, b, vv in zip(p1, p2, vals)])
            o_ref[r, cols] = (_rms(o, sub) * (1.0 - lam_init)).astype(o_ref.dtype)
            return carry

        lax.fori_loop(0, sq // qc, body, 0)


def _attn_a(q, k, v, cache_k, cache_v, lam_p, subln, layer_idx, j, rows, prev=None):
    d = q.shape[1]
    lam_init = 0.8 - 0.6 * math.exp(-0.3 * layer_idx)
    has_cache = cache_k is not None
    if has_cache:
        t, nb, roff = rows.ds, rows.db, rows.n_ctx // rows.ds
        hb = 4
    else:
        t, nb, roff = rows.s, rows.b, 0
        hb = A_HEADS
    cw = hb * 2 * A_DH
    qc = _largest_tile(t, 256)
    tok = lambda b, g: (b + roff, g)
    in_specs = [pl.BlockSpec((t, cw), tok)] * 3
    args = [q, k, v]
    if has_cache:
        p = cache_k.shape[2]
        cspec = pl.BlockSpec((None, None, p, cw), lambda b, g: (b, j, 0, g))
        in_specs += [cspec, cspec]
        args += [cache_k, cache_v]
    in_specs += [pl.BlockSpec((None, 4, A_DH), lambda b, g: (j, 0, 0)),
                 pl.BlockSpec((None, 1, 2 * A_DH), lambda b, g: (j, 0, 0))]
    args += [lam_p, subln]
    aliases = {}
    if prev is not None:
        in_specs.append(pl.BlockSpec(memory_space=pl.ANY))
        args.append(prev)
        aliases = {len(args) - 1: 0}
    kern = functools.partial(_attn_a_kernel, hb=hb, qc=qc, lam_init=lam_init, has_cache=has_cache)
    if prev is not None:
        kern = _drop_last_input(kern, len(args))
    return pl.pallas_call(
        kern,
        out_shape=jax.ShapeDtypeStruct((rows.mt, d), MXU),
        grid=(nb, d // cw),
        in_specs=in_specs,
        out_specs=pl.BlockSpec((t, cw), tok),
        input_output_aliases=aliases,
        compiler_params=_cparams(("parallel", "parallel"), 40),
        name="attn_a_lat" if has_cache else "attn_a_ctx",
    )(*args)


def _drop_last_input(kern, n_in):
    def wrapped(*refs):
        return kern(*refs[:n_in - 1], *refs[n_in:])
    return wrapped


def _gla_kernel(*refs, t, has_s0, has_sout):
    q_ref, v_ref, f0_ref, f1_ref, g_ref, gn_ref = refs[:6]
    pos = 6
    if has_s0:
        s0f_ref, s0b_ref = refs[pos:pos + 2]
        pos += 2
    o_ref = refs[pos]
    pos += 1
    if has_sout:
        sf_ref, sb_ref = refs[pos:pos + 2]
        pos += 2
    qdf, kif, qdb, kib, vb, bcf, bsb, of, ob, uf, ub = refs[pos:]
    c = B_CHUNK
    n = t // c
    r32 = lax.broadcasted_iota(jnp.int32, (t, 1), 0) % c

    def prefix(x):
        s = 1
        while s < c:
            x = x + jnp.where(r32 >= s, pltpu.roll(x, s, 0), 0.0)
            s *= 2
        return x

    def suffix(x):
        s = 1
        while s < c:
            x = x + jnp.where(r32 < c - s, pltpu.roll(x, t - s, 0), 0.0)
            s *= 2
        return x

    q = q_ref[...]
    vb[...] = v_ref[...].astype(vb.dtype)
    f = f0_ref[...]
    bc = prefix(jnp.log(f))
    bcf[...] = bc
    qdf[...] = (q * jnp.exp(bc)).astype(qdf.dtype)
    kif[...] = ((1.0 - f) * jnp.exp(-bc)).astype(kif.dtype)
    f = f1_ref[...]
    bs = suffix(jnp.log(f))
    bsb[...] = bs
    qdb[...] = (q * jnp.exp(bs)).astype(qdb.dtype)
    kib[...] = ((1.0 - f) * jnp.exp(-bs)).astype(kib.dtype)

    blk = _largest_tile(t, 256)
    ti = lax.broadcasted_iota(jnp.int32, (blk, blk), 0)
    si = lax.broadcasted_iota(jnp.int32, (blk, blk), 1)
    same = (ti // c) == (si // c)
    lower, upper = same & (ti >= si), same & (ti <= si)
    for bi in range(t // blk):
        rb = slice(bi * blk, (bi + 1) * blk)
        vv = vb[rb, :]
        of[rb, :] = _dot(jnp.where(lower, _dot_nt(qdf[rb, :], kif[rb, :]), 0.0), vv)
        ob[rb, :] = _dot(jnp.where(upper, _dot_nt(qdb[rb, :], kib[rb, :]), 0.0), vv)

    def incr(i, carry):
        r = pl.ds(pl.multiple_of(i * c, c), c)
        vv = vb[r, :]
        bc_c, bs_c = bcf[r, :], bsb[r, :]
        uf[i] = _dot_tn(vv, (1.0 - f0_ref[r, :]) * jnp.exp(bc_c[c - 1:c, :] - bc_c))
        ub[i] = _dot_tn(vv, (1.0 - f1_ref[r, :]) * jnp.exp(bs_c[0:1, :] - bs_c))
        return carry

    lax.fori_loop(0, n, incr, 0, unroll=min(4, n))

    def step(i, carry):
        sf, sb = carry
        rf = pl.ds(pl.multiple_of(i * c, c), c)
        of[rf, :] += _dot_nt(qdf[rf, :], sf)
        sf = sf * jnp.exp(bcf[pl.ds(i * c + c - 1, 1), :]) + uf[i]
        ib = n - 1 - i
        rb = pl.ds(pl.multiple_of(ib * c, c), c)
        ob[rb, :] += _dot_nt(qdb[rb, :], sb)
        sb = sb * jnp.exp(bsb[pl.ds(ib * c, 1), :]) + ub[ib]
        return sf, sb

    if has_s0:
        init = (s0f_ref[...].T, s0b_ref[...].T)
    else:
        init = (jnp.zeros((B_DK, B_DK), F32), jnp.zeros((B_DK, B_DK), F32))
    sf, sb = lax.fori_loop(0, n, step, init, unroll=min(4, n))
    if has_sout:
        sf_ref[...] = sf.T
        sb_ref[...] = sb.T
    o = of[...] + ob[...]
    o_ref[...] = (_rms(o, gn_ref[...]) * _silu(g_ref[...])).astype(o_ref.dtype)


def _gla(q, v, f0, f1, g, gnorm, s0f, s0b, j, rows, prev=None):
    d = q.shape[1]
    has_s0 = s0f is not None
    if has_s0:
        t, nb, roff = rows.ds, rows.db, rows.n_ctx // rows.ds
    else:
        t, nb, roff = rows.s, rows.b, 0
    dv = d // B_HEADS
    assert dv == LANES and B_DK == LANES
    tok = pl.BlockSpec((t, LANES), lambda b, h: (b + roff, h))
    in_specs = [tok] * 5 + [pl.BlockSpec((None, 1, dv), lambda b, h: (j, 0, 0))]
    args = [q, v, f0, f1, g, gnorm]
    if has_s0:
        sspec = pl.BlockSpec((None, None, None, B_DK, dv), lambda b, h: (b, j, h, 0, 0))
        in_specs += [sspec, sspec]
        args += [s0f, s0b]
    out_shape = [jax.ShapeDtypeStruct((rows.mt, d), MXU)]
    out_specs = [tok]
    if not has_s0:
        ospec = pl.BlockSpec((None, None, B_DK, dv), lambda b, h: (b, h, 0, 0))
        out_shape += [jax.ShapeDtypeStruct((nb, B_HEADS, B_DK, dv), F32)] * 2
        out_specs += [ospec, ospec]
    aliases = {}
    kern = functools.partial(_gla_kernel, t=t, has_s0=has_s0, has_sout=not has_s0)
    if prev is not None:
        in_specs.append(pl.BlockSpec(memory_space=pl.ANY))
        args.append(prev)
        aliases = {len(args) - 1: 0}
        kern = _drop_last_input(kern, len(args))
    scratch = ([pltpu.VMEM((t, LANES), MXU) for _ in range(5)]
               + [pltpu.VMEM((t, LANES), F32) for _ in range(4)]
               + [pltpu.VMEM((t // B_CHUNK, dv, B_DK), F32) for _ in range(2)])
    return pl.pallas_call(
        kern,
        out_shape=tuple(out_shape),
        grid=(nb, B_HEADS),
        in_specs=in_specs,
        out_specs=tuple(out_specs),
        scratch_shapes=scratch,
        input_output_aliases=aliases,
        compiler_params=_cparams(("parallel", "parallel"), 32),
        name="gla_lat" if has_s0 else "gla_ctx",
    )(*args)


def _attn_c_kernel(*refs, qc, has_cache):
    qn_ref, qp_ref, ckv_ref, kpe_ref = refs[:4]
    pos = 4
    if has_cache:
        ckvc_ref, kpec_ref = refs[pos:pos + 2]
        pos += 2
    wuk_ref, wuv_ref, o_ref = refs[pos:pos + 3]
    scr = refs[pos + 3:]
    sq = qn_ref.shape[0]
    scale = (C_NOPE + C_ROPE) ** -0.5
    wuk = wuk_ref[...].astype(MXU)
    wuv = wuv_ref[...].astype(MXU)

    def expand(ckv_r, kpe_r, kn_r, vn_r, kp_r):
        ck = ckv_r[...].astype(MXU)
        kn_r[...] = jnp.dot(ck, wuk, preferred_element_type=F32).astype(kn_r.dtype)
        vn_r[...] = jnp.dot(ck, wuv, preferred_element_type=F32).astype(vn_r.dtype)
        kp = kpe_r[...]
        kp_r[...] = jnp.concatenate([kp, kp], axis=1).astype(kp_r.dtype)

    groups = [scr[0:3]]
    expand(ckv_ref, kpe_ref, *scr[0:3])
    if has_cache:
        groups.append(scr[3:6])
        expand(ckvc_ref, kpec_ref, *scr[3:6])
    lo = lax.broadcasted_iota(jnp.int32, (1, LANES), 1) < C_ROPE

    def body(c, carry):
        r = pl.ds(pl.multiple_of(c * qc, qc), qc)
        qp = qp_ref[r, :]
        zero = jnp.zeros_like(qp)
        for hh in range(2):
            cols = slice(hh * C_NOPE, (hh + 1) * C_NOPE)
            qn = qn_ref[r, cols]
            qph = jnp.where(lo, qp, zero) if hh == 0 else jnp.where(lo, zero, qp)
            parts = [(_dot_nt(qn, kn_r[:, cols]) + _dot_nt(qph, kp_r[...])) * scale
                     for kn_r, _, kp_r in groups]
            ps = _softmax_parts(parts)
            o = functools.reduce(lambda a, b: a + b,
                                 [_dot(p, vn_r[:, cols]) for p, (_, vn_r, _) in zip(ps, groups)])
            o_ref[r, cols] = o.astype(o_ref.dtype)
        return carry

    lax.fori_loop(0, sq // qc, body, 0)


def _attn_c(qn, qp, ckv, kpe, cache_ckv, cache_kpe, wuk, wuv, j, rows, prev=None):
    has_cache = cache_ckv is not None
    if has_cache:
        t, nb, roff = rows.ds, rows.db, rows.n_ctx // rows.ds
    else:
        t, nb, roff = rows.s, rows.b, 0
    hw = 2 * C_NOPE
    qc = _largest_tile(t, 256)
    in_specs = [pl.BlockSpec((t, hw), lambda b, g: (b + roff, g)),
                pl.BlockSpec((t, 2 * C_ROPE), lambda b, g: (b + roff, g)),
                pl.BlockSpec((t, C_KVLORA), lambda b, g: (b + roff, 0)),
                pl.BlockSpec((t, C_ROPE), lambda b, g: (b + roff, 0))]
    args = [qn, qp, ckv, kpe]
    scratch = [pltpu.VMEM((t, hw), MXU), pltpu.VMEM((t, hw), MXU), pltpu.VMEM((t, LANES), MXU)]
    if has_cache:
        p = cache_ckv.shape[2]
        in_specs += [pl.BlockSpec((None, None, p, C_KVLORA), lambda b, g: (b, j, 0, 0)),
                     pl.BlockSpec((None, None, p, C_ROPE), lambda b, g: (b, j, 0, 0))]
        args += [cache_ckv, cache_kpe]
        scratch += [pltpu.VMEM((p, hw), MXU), pltpu.VMEM((p, hw), MXU), pltpu.VMEM((p, LANES), MXU)]
    in_specs += [pl.BlockSpec((None, C_KVLORA, hw), lambda b, g: (j, 0, g))] * 2
    args += [wuk, wuv]
    aliases = {}
    kern = functools.partial(_attn_c_kernel, qc=qc, has_cache=has_cache)
    if prev is not None:
        in_specs.append(pl.BlockSpec(memory_space=pl.ANY))
        args.append(prev)
        aliases = {len(args) - 1: 0}
        kern = _drop_last_input(kern, len(args))
    return pl.pallas_call(
        kern,
        out_shape=jax.ShapeDtypeStruct((rows.mt, C_HEADS * C_VD), MXU),
        grid=(nb, C_HEADS // 2),
        in_specs=in_specs,
        out_specs=pl.BlockSpec((t, hw), lambda b, g: (b + roff, g)),
        scratch_shapes=scratch,
        input_output_aliases=aliases,
        compiler_params=_cparams(("parallel", "parallel"), 32),
        name="attn_c_lat" if has_cache else "attn_c_ctx",
    )(*args)


def _attn_d_kernel(*refs, has_cache):
    q_ref, k_ref, v_ref = refs[:3]
    pos = 3
    if has_cache:
        kc_ref, vc_ref = refs[pos:pos + 2]
        pos += 2
    sink_ref, o_ref = refs[pos:pos + 2]
    k2, v2 = refs[pos + 2:pos + 4]
    if has_cache:
        k2c, v2c = refs[pos + 4:pos + 6]
    t = q_ref.shape[0]
    qb = Q_BLOCK
    scale = D_DH ** -0.5
    r = D_HEADS // D_KV_HEADS
    win = min(t, qb + 2 * D_WINDOW)
    lo = lax.broadcasted_iota(jnp.int32, (1, LANES), 1) < D_DH
    rowblk = lax.broadcasted_iota(jnp.int32, (r * qb, 1), 0) // qb

    def stage(k_r, v_r, k2_r, v2_r, g):
        gs = slice(g * D_DH, (g + 1) * D_DH)
        kg, vg = k_r[:, gs], v_r[:, gs]
        k2_r[...] = jnp.concatenate([kg, kg], axis=1).astype(k2_r.dtype)
        v2_r[...] = jnp.concatenate([vg, vg], axis=1).astype(v2_r.dtype)

    for g in range(D_KV_HEADS):
        stage(k_ref, v_ref, k2, v2, g)
        if has_cache:
            stage(kc_ref, vc_ref, k2c, v2c, g)
        sink = jnp.zeros((r * qb, 1), F32)
        for e in range(r):
            h = g * r + e
            sink = jnp.where(rowblk == e, sink_ref[0:1, h:h + 1], sink)

        def body(c, carry, g=g, sink=sink):
            rq = pl.ds(pl.multiple_of(c * qb, qb), qb)
            tiles = []
            for e in range(r):
                cols = slice((g * r + e - e % 2) * D_DH, (g * r + e - e % 2 + 2) * D_DH)
                qp = q_ref[rq, cols]
                zero = jnp.zeros_like(qp)
                tiles.append(jnp.where(lo, qp, zero) if e % 2 == 0 else jnp.where(lo, zero, qp))
            qs = jnp.concatenate(tiles, axis=0)
            if has_cache:
                ws = pl.multiple_of(jnp.clip(c * qb - D_WINDOW, 0, t - win), qb)
                rk = pl.ds(ws, win)
                qpos = c * qb + lax.broadcasted_iota(jnp.int32, (r * qb, 1), 0) % qb
                kpos = ws + lax.broadcasted_iota(jnp.int32, (1, win), 1)
                valid = jnp.abs(qpos - kpos) <= D_WINDOW
                parts = [_dot_nt(qs, k2c[...]) * scale,
                         jnp.where(valid, _dot_nt(qs, k2[rk, :]), NEG) * scale]
                vs = [v2c[...], v2[rk, :]]
            else:
                parts = [_dot_nt(qs, k2[...]) * scale]
                vs = [v2[...]]
            m = functools.reduce(jnp.maximum, [jnp.max(x, axis=-1, keepdims=True) for x in parts])
            m = jnp.maximum(m, sink)
            es = [jnp.exp(x - m) for x in parts]
            den = functools.reduce(lambda a, b: a + b,
                                   [jnp.sum(x, axis=-1, keepdims=True) for x in es])
            inv = 1.0 / (den + jnp.exp(sink - m))
            o = functools.reduce(lambda a, b: a + b,
                                 [_dot(x * inv, vx) for x, vx in zip(es, vs)])
            for pr in range(r // 2):
                cols = slice((g * r + 2 * pr) * D_DH, (g * r + 2 * pr + 2) * D_DH)
                pair = jnp.where(lo, o[2 * pr * qb:(2 * pr + 1) * qb, :],
                                 o[(2 * pr + 1) * qb:(2 * pr + 2) * qb, :])
                o_ref[rq, cols] = pair.astype(o_ref.dtype)
            return carry

        lax.fori_loop(0, t // qb, body, 0)


def _attn_d(q, k, v, cache_k, cache_v, sink, j, rows, prev=None):
    d = q.shape[1]
    kvw = D_KV_HEADS * D_DH
    has_cache = cache_k is not None
    if has_cache:
        t, nb, roff = rows.ds, rows.db, rows.n_ctx // rows.ds
    else:
        t, nb, roff = rows.s, rows.b, 0
    assert t % Q_BLOCK == 0
    in_specs = [pl.BlockSpec((t, d), lambda b: (b + roff, 0)),
                pl.BlockSpec((t, kvw), lambda b: (b + roff, 0)),
                pl.BlockSpec((t, kvw), lambda b: (b + roff, 0))]
    args = [q, k, v]
    scratch = [pltpu.VMEM((t, LANES), MXU) for _ in range(2)]
    if has_cache:
        p = cache_k.shape[2]
        cspec = pl.BlockSpec((None, None, p, kvw), lambda b: (b, j, 0, 0))
        in_specs += [cspec, cspec]
        args += [cache_k, cache_v]
        scratch += [pltpu.VMEM((p, LANES), MXU) for _ in range(2)]
    in_specs.append(pl.BlockSpec((None, 1, D_HEADS), lambda b: (j, 0, 0)))
    args.append(sink)
    aliases = {}
    kern = functools.partial(_attn_d_kernel, has_cache=has_cache)
    if prev is not None:
        in_specs.append(pl.BlockSpec(memory_space=pl.ANY))
        args.append(prev)
        aliases = {len(args) - 1: 0}
        kern = _drop_last_input(kern, len(args))
    return pl.pallas_call(
        kern,
        out_shape=jax.ShapeDtypeStruct((rows.mt, d), MXU),
        grid=(nb,),
        in_specs=in_specs,
        out_specs=pl.BlockSpec((t, d), lambda b: (b + roff, 0)),
        scratch_shapes=scratch,
        input_output_aliases=aliases,
        compiler_params=_cparams(("parallel",), 32),
        name="attn_d_lat" if has_cache else "attn_d_ctx",
    )(*args)


def kernel(x_prompt, x_sample, cache_a_k, cache_a_v, state_b_fwd, state_b_bwd, cache_c_ckv,
           cache_c_kpe, cache_d_k, cache_d_v, c, c_ctx, ada_w, ada_b, norm_mix_pre,
           norm_mix_post, norm_ffn_pre, norm_ffn_post, a_wq, a_wk, a_wv, a_wo, a_lambda,
           a_subln, b_wq, b_wi, b_wf, b_lower, b_wg, b_gnorm, b_wo, c_wdq, c_qnorm, c_wuq,
           c_wdkv, c_kvnorm, c_wuk, c_wuv, c_wo, d_wq, d_wk, d_wv, d_sink, d_wo, ffn_wg,
           ffn_wu, ffn_wd):
    b, s, d = x_prompt.shape
    db, ds, _ = x_sample.shape
    depth = ada_w.shape[0]
    rows = _Rows(b, s, db, ds, _largest_tile(math.gcd(b * s, ds), 1024))
    n_ctx = rows.n_ctx

    cond = jnp.zeros((rows.crows, d), F32).at[:db].set(c).at[db].set(c_ctx)
    mod = _ada(cond, ada_w, ada_b)
    g3 = lambda a: a.reshape(a.shape[0], 1, a.shape[1])
    n_mix_pre, n_mix_post = g3(norm_mix_pre), g3(norm_mix_post)
    n_ffn_pre, n_ffn_post = g3(norm_ffn_pre), g3(norm_ffn_post)
    tabs = _rope_tables(ds)
    rope_epi = lambda tn: (_make_epi_rope(rows, tn), _rope_extra(tabs, rows))

    x, h = _prep(x_prompt.reshape(n_ctx, d), x_sample.reshape(rows.n_lat, d),
                 n_mix_pre, mod, 0, rows)

    flat4 = lambda a: a.reshape(a.shape[:3] + (-1,))
    wo_b = {0: _cast_mxu(a_wo), 1: _cast_mxu(b_wo), 2: _cast_mxu(c_wo), 3: _cast_mxu(d_wo)}
    wd_b = _cast_mxu(ffn_wd)
    wf = b_wf.reshape((-1,) + b_wf.shape[2:])
    outs = {k_: [] for k_ in ("a_k", "a_v", "b_f", "b_b", "c_ckv", "c_kpe", "d_k", "d_v")}
    for i in range(depth):
        m, j = i % N_MIXERS, i // N_MIXERS
        if m == 0:
            tn = _largest_tile(a_wq.shape[2], 1024)
            epi, ext = rope_epi(tn)
            q, = _proj(h, [(a_wq, j)], epi, [(tn, MXU)], rows, tn, ext, name="a_q")
            k, = _proj(h, [(a_wk, j)], epi, [(tn, F32)], rows, tn, ext, name="a_k")
            v, = _proj(h, [(a_wv, j)], _epi_store, [(tn, F32)], rows, tn, name="a_v")
            o = _attn_a(q, k, v, None, None, a_lambda, g3(a_subln), i, j, rows)
            o = _attn_a(q, k, v, flat4(cache_a_k), flat4(cache_a_v), a_lambda, g3(a_subln),
                        i, j, rows, prev=o)
            outs["a_k"].append(k[:n_ctx].reshape(b, s, A_HEADS, 2 * A_DH))
            outs["a_v"].append(v[:n_ctx].reshape(b, s, A_HEADS, 2 * A_DH))
        elif m == 1:
            tn = _largest_tile(b_wq.shape[2], 1024)
            q, = _proj(h, [(b_wq, j)], _epi_silu, [(tn, F32)], rows, tn, name="b_q")
            vi, = _proj(h, [(b_wi, j)], _epi_store, [(tn, F32)], rows, tn, name="b_i")
            g, = _proj(h, [(b_wg, j)], _epi_store, [(tn, F32)], rows, tn, name="b_g")
            fs = []
            for dr in range(2):
                bl_spec = pl.BlockSpec((None, depth, tn), lambda jn, mm, dr=dr: (dr, 0, jn))
                f, = _proj(h, [(wf, 2 * j + dr)], _make_epi_forget(i, depth), [(tn, F32)], rows,
                           tn, [(jnp.swapaxes(b_lower, 0, 1), bl_spec)], name="b_f%d" % dr)
                fs.append(f)
            o, sf, sb = _gla(q, vi, fs[0], fs[1], g, g3(b_gnorm), None, None, j, rows)
            o, = _gla(q, vi, fs[0], fs[1], g, g3(b_gnorm), state_b_fwd, state_b_bwd, j, rows,
                      prev=o)
            outs["b_f"].append(sf)
            outs["b_b"].append(sb)
        elif m == 2:
            nq = c_wdq.shape[2]
            cq, = _proj(h, [(c_wdq, j)], _make_epi_rmsnorm(), [(nq, MXU)], rows, nq,
                        [(g3(c_qnorm), pl.BlockSpec((None, 1, nq), lambda jn, mm: (j, 0, 0)))],
                        name="c_dq")
            nkv = c_wdkv.shape[2]
            ext = [(g3(c_kvnorm), pl.BlockSpec((None, 1, C_KVLORA), lambda jn, mm: (j, 0, 0)))]
            ext += _rope_extra(tabs, rows)
            ckv, kpe = _proj(h, [(c_wdkv, j)], _make_epi_ckv(rows),
                             [(C_KVLORA, F32), (C_ROPE, F32)], rows, nkv, ext, name="c_dkv")
            wuq = c_wuq.reshape(c_wuq.shape[0], nq, C_HEADS, C_NOPE + C_ROPE)
            wuq_n = wuq[:, :, :, :C_NOPE].reshape(-1, nq, C_HEADS * C_NOPE)
            wuq_p = wuq[:, :, :, C_NOPE:].reshape(-1, nq, C_HEADS * C_ROPE)
            tn = _largest_tile(wuq_n.shape[2], 1024)
            qn, = _proj(cq, [(wuq_n, j)], _epi_store, [(tn, MXU)], rows, tn, name="c_qn")
            tnp = wuq_p.shape[2]
            epi, ext = rope_epi(tnp)
            qp, = _proj(cq, [(wuq_p, j)], epi, [(tnp, MXU)], rows, tnp, ext, name="c_qp")
            o = _attn_c(qn, qp, ckv, kpe, None, None, c_wuk, c_wuv, j, rows)
            o = _attn_c(qn, qp, ckv, kpe, cache_c_ckv, cache_c_kpe, c_wuk, c_wuv, j, rows, prev=o)
            outs["c_ckv"].append(ckv[:n_ctx].reshape(b, s, C_KVLORA))
            outs["c_kpe"].append(kpe[:n_ctx].reshape(b, s, C_ROPE))
        else:
            tn = _largest_tile(d_wq.shape[2], 1024)
            epi, ext = rope_epi(tn)
            q, = _proj(h, [(d_wq, j)], epi, [(tn, MXU)], rows, tn, ext, name="d_q")
            kvw = d_wk.shape[2]
            epi, ext = rope_epi(kvw)
            k, = _proj(h, [(d_wk, j)], epi, [(kvw, F32)], rows, kvw, ext, name="d_k")
            v, = _proj(h, [(d_wv, j)], _epi_store, [(kvw, F32)], rows, kvw, name="d_v")
            o = _attn_d(q, k, v, None, None, g3(d_sink), j, rows)
            o = _attn_d(q, k, v, flat4(cache_d_k), flat4(cache_d_v), g3(d_sink), j, rows, prev=o)
            outs["d_k"].append(k[:n_ctx].reshape(b, s, D_KV_HEADS, D_DH))
            outs["d_v"].append(v[:n_ctx].reshape(b, s, D_KV_HEADS, D_DH))

        x, h = _resid(o, wo_b[m], j, x, mod, i, 2, n_mix_post, rows, nxt=(i, 3, n_ffn_pre),
                      nsub=2, name="mix_out")
        tf = _largest_tile(ffn_wg.shape[2], 512)
        a, = _proj(h, [(ffn_wg, i), (ffn_wu, i)], _epi_swiglu, [(tf, MXU)], rows, tf,
                   name="ffn_gu")
        nxt = (i + 1, 0, n_mix_pre) if i + 1 < depth else None
        x, h = _resid(a, wd_b, i, x, mod, i, 5, n_ffn_post, rows, nxt=nxt, name="ffn_down")

    stack = lambda lst: jnp.stack(lst, axis=1)
    return (x[:n_ctx].reshape(b, s, d), x[n_ctx:].reshape(db, ds, d),
            stack(outs["a_k"]), stack(outs["a_v"]), stack(outs["b_f"]), stack(outs["b_b"]),
            stack(outs["c_ckv"]), stack(outs["c_kpe"]), stack(outs["d_k"]), stack(outs["d_v"]))
```

```python
import functools
import math

import numpy as np
import jax
import jax.numpy as jnp
from jax import lax
from jax.experimental import pallas as pl
from jax.experimental.pallas import tpu as pltpu

EPS = 1e-6
ROPE_BASE = 10000.0
GRID_W = 64
NEG = -1e30
N_MIXERS = 4
A_HEADS, A_DH = 16, 64
B_HEADS, B_DK, B_CHUNK = 16, 128, 32
C_HEADS, C_NOPE, C_ROPE, C_VD = 16, 128, 64, 128
D_HEADS, D_KV_HEADS, D_DH, D_WINDOW, Q_BLOCK = 32, 4, 64, 128, 128

LANES = 128
SUBLANES = 8
VMEM_CAP_MIB = 56

MXU = jnp.bfloat16
F32 = jnp.float32


def _cparams(sem, vmem_mib):
    return pltpu.CompilerParams(dimension_semantics=sem,
                                vmem_limit_bytes=min(vmem_mib, VMEM_CAP_MIB) << 20)


def _dot(a, b):
    return jnp.dot(a.astype(MXU), b.astype(MXU), preferred_element_type=F32)


def _dot_nt(a, b):
    return lax.dot_general(a.astype(MXU), b.astype(MXU), (((1,), (1,)), ((), ())),
                           preferred_element_type=F32)


def _dot_tn(a, b):
    return lax.dot_general(a.astype(MXU), b.astype(MXU), (((0,), (0,)), ((), ())),
                           preferred_element_type=F32)


def _sigmoid(x):
    return 1.0 / (1.0 + jnp.exp(-x))


def _silu(x):
    return x * _sigmoid(x)


def _rms(x, g):
    ms = jnp.mean(x * x, axis=-1, keepdims=True)
    return x * lax.rsqrt(ms + EPS) * g


def _largest_tile(n, pref):
    t = min(n, pref)
    while n % t:
        t -= SUBLANES
    assert t > 0 and n % t == 0
    return t


class _Rows:
    def __init__(self, b, s, db, ds, tm):
        self.b, self.s, self.db, self.ds = b, s, db, ds
        self.n_ctx, self.n_lat = b * s, db * ds
        self.mt = self.n_ctx + self.n_lat
        self.tm = tm
        assert self.n_ctx % tm == 0 and ds % tm == 0 and self.n_ctx % ds == 0
        self.nct = self.n_ctx // tm
        self.ntiles = self.mt // tm
        self.crows = -(-(db + 1) // SUBLANES) * SUBLANES

    def cond_row(self, i):
        lat = ((i - self.nct) * self.tm) // self.ds
        return jnp.where(i < self.nct, self.db, lat)

    def pos_block(self, i):
        return (jnp.maximum(i - self.nct, 0) * self.tm % self.ds) // self.tm


def _rope_tables(n_tok):
    nf = 16
    inv = (ROPE_BASE ** (-np.arange(nf, dtype=np.float32) / nf)).astype(np.float32)
    t = np.arange(n_tok)
    row, col = (t // GRID_W).astype(np.float32), (t % GRID_W).astype(np.float32)
    lane = np.arange(64)
    pos = np.where(lane[None, :] < 32, row[:, None], col[:, None]).astype(np.float32)
    ang = (pos * inv[lane % nf][None, :]).astype(np.float32)
    cos, sin = np.cos(ang).astype(np.float32), np.sin(ang).astype(np.float32)
    first = (lane % 32) < 16
    sa = np.where(first[None, :], -sin, 0.0).astype(np.float32)
    sb = np.where(first[None, :], 0.0, sin).astype(np.float32)
    tile2 = lambda a: np.concatenate([a, a], axis=1)
    return jnp.asarray(tile2(cos)), jnp.asarray(tile2(sa)), jnp.asarray(tile2(sb))


def _rope128(y, cos, sa, sb):
    return (y * cos + pltpu.roll(y, LANES - 16, 1) * sa + pltpu.roll(y, 16, 1) * sb)


def _ada_kernel(c_ref, w_ref, b_ref, o_ref):
    o_ref[...] = _dot(_silu(c_ref[...]), w_ref[...]) + b_ref[...]


def _ada(cond, ada_w, ada_b):
    depth, d, n = ada_w.shape
    cr = cond.shape[0]
    tn = _largest_tile(n, 1024)
    return pl.pallas_call(
        _ada_kernel,
        out_shape=jax.ShapeDtypeStruct((depth, cr, n), F32),
        grid=(depth, n // tn),
        in_specs=[pl.BlockSpec((cr, d), lambda l, j: (0, 0)),
                  pl.BlockSpec((None, d, tn), lambda l, j: (l, 0, j)),
                  pl.BlockSpec((None, 1, tn), lambda l, j: (l, 0, j))],
        out_specs=pl.BlockSpec((None, cr, tn), lambda l, j: (l, 0, j)),
        compiler_params=_cparams(("parallel", "parallel"), 40),
        name="ada_mod",
    )(cond, ada_w, ada_b.reshape(depth, 1, n))


def _prep_kernel(xp_ref, xs_ref, g_ref, mod_ref, x_ref, h_ref, *, rows, d):
    i = pl.program_id(0)
    x = jnp.where(i < rows.nct, xp_ref[...], xs_ref[...])
    cr = rows.cond_row(i)
    shift = mod_ref[pl.ds(cr, 1), pl.ds(0, d)]
    scale = mod_ref[pl.ds(cr, 1), pl.ds(d, d)]
    x_ref[...] = x
    h_ref[...] = (_rms(x, g_ref[...]) * (1.0 + scale) + shift).astype(h_ref.dtype)


def _prep(xp, xs, g, mod, layer, rows):
    d = xp.shape[1]
    rows = _Rows(rows.b, rows.s, rows.db, rows.ds, _largest_tile(rows.tm, 256))
    tm, nct = rows.tm, rows.nct
    return pl.pallas_call(
        functools.partial(_prep_kernel, rows=rows, d=d),
        out_shape=(jax.ShapeDtypeStruct((rows.mt, d), F32),
                   jax.ShapeDtypeStruct((rows.mt, d), MXU)),
        grid=(rows.ntiles,),
        in_specs=[pl.BlockSpec((tm, d), lambda i: (jnp.minimum(i, nct - 1), 0)),
                  pl.BlockSpec((tm, d), lambda i: (jnp.maximum(i - nct, 0), 0)),
                  pl.BlockSpec((None, 1, d), lambda i: (layer, 0, 0)),
                  pl.BlockSpec((None, rows.crows, mod.shape[2]), lambda i: (layer, 0, 0))],
        out_specs=(pl.BlockSpec((tm, d), lambda i: (i, 0)),
                   pl.BlockSpec((tm, d), lambda i: (i, 0))),
        compiler_params=_cparams(("parallel",), 40 * tm * d // (1 << 20) + 8),
        name="prep",
    )(xp, xs, g, mod)


def _proj_kernel(*refs, n_w, n_extra, n_out, epi):
    h_ref = refs[0]
    w_refs = refs[1:1 + n_w]
    extra = refs[1 + n_w:1 + n_w + n_extra]
    outs = refs[1 + n_w + n_extra:1 + n_w + n_extra + n_out]
    wb_refs = refs[1 + n_w + n_extra + n_out:]
    m = pl.program_id(1)

    @pl.when(m == 0)
    def _():
        for w_ref, wb_ref in zip(w_refs, wb_refs):
            wb_ref[...] = w_ref[...].astype(wb_ref.dtype)

    h = h_ref[...]
    ys = [jnp.dot(h, wb_ref[...], preferred_element_type=F32) for wb_ref in wb_refs]
    epi(ys, m, extra, outs)


def _proj(h, ws, epi, outs, rows, tn, extra=(), name="proj"):
    mt, k = h.shape
    n = ws[0][0].shape[2]
    assert n % tn == 0
    tm = rows.tm
    nj = n // tn
    in_specs = [pl.BlockSpec((tm, k), lambda j, m: (m, 0))]
    in_specs += [pl.BlockSpec((None, k, tn), lambda j, m, li=li: (li, 0, j)) for _, li in ws]
    in_specs += [s for _, s in extra]
    out_shape = tuple(jax.ShapeDtypeStruct((mt, c * nj), dt) for c, dt in outs)
    out_specs = tuple(pl.BlockSpec((tm, c), lambda j, m: (m, j)) for c, _ in outs)
    out_bytes = sum(tm * c * jnp.dtype(dt).itemsize for c, dt in outs)
    vmem = (2 * tm * k * 2 + len(ws) * (2 * k * tn * 4 + k * tn * 2) + 2 * out_bytes
            + len(ws) * tm * tn * 8) // (1 << 20) + 6
    res = pl.pallas_call(
        functools.partial(_proj_kernel, n_w=len(ws), n_extra=len(extra), n_out=len(outs), epi=epi),
        out_shape=out_shape,
        grid=(nj, rows.ntiles),
        in_specs=in_specs,
        out_specs=out_specs,
        scratch_shapes=[pltpu.VMEM((k, tn), MXU) for _ in ws],
        compiler_params=_cparams(("parallel", "arbitrary"), vmem),
        name=name,
    )(h, *[w for w, _ in ws], *[a for a, _ in extra])
    return res


def _epi_store(ys, m, extra, outs):
    outs[0][...] = ys[0].astype(outs[0].dtype)


def _epi_silu(ys, m, extra, outs):
    outs[0][...] = _silu(ys[0]).astype(outs[0].dtype)


def _epi_swiglu(ys, m, extra, outs):
    outs[0][...] = (_silu(ys[0]) * ys[1]).astype(outs[0].dtype)


def _make_epi_rope(rows, tn):
    def epi(ys, m, extra, outs):
        cos_ref, sa_ref, sb_ref = extra
        y, o_ref = ys[0], outs[0]

        @pl.when(m < rows.nct)
        def _():
            o_ref[...] = y.astype(o_ref.dtype)

        @pl.when(m >= rows.nct)
        def _():
            cos, sa, sb = cos_ref[...], sa_ref[...], sb_ref[...]
            for c in range(tn // LANES):
                sl = slice(c * LANES, (c + 1) * LANES)
                o_ref[:, sl] = _rope128(y[:, sl], cos, sa, sb).astype(o_ref.dtype)
    return epi


def _rope_extra(tabs, rows):
    tm = rows.tm
    spec = pl.BlockSpec((tm, LANES), lambda j, m: (rows.pos_block(m), 0))
    return [(t, spec) for t in tabs]


def _make_epi_forget(layer_idx, depth):
    def epi(ys, m, extra, outs):
        bl = extra[0][...]
        e = jnp.exp(bl - jnp.max(bl, axis=0, keepdims=True))
        p = e / jnp.sum(e, axis=0, keepdims=True)
        lb = jnp.sum(p[0:layer_idx + 1], axis=0, keepdims=True) - p[0:1]
        outs[0][...] = lb + (1.0 - lb) * _sigmoid(ys[0])
    return epi


def _make_epi_rmsnorm():
    def epi(ys, m, extra, outs):
        outs[0][...] = _rms(ys[0], extra[0][...]).astype(outs[0].dtype)
    return epi


def _make_epi_ckv(rows):
    def epi(ys, m, extra, outs):
        g_ref, cos_ref, sa_ref, sb_ref = extra
        y = ys[0]
        outs[0][...] = _rms(y[:, :C_KVLORA], g_ref[...])
        kpe = y[:, C_KVLORA:C_KVLORA + C_ROPE]

        @pl.when(m < rows.nct)
        def _():
            outs[1][...] = kpe

        @pl.when(m >= rows.nct)
        def _():
            k2 = jnp.concatenate([kpe, kpe], axis=1)
            r = _rope128(k2, cos_ref[...], sa_ref[...], sb_ref[...])
            outs[1][...] = r[:, :C_ROPE]
    return epi


C_KVLORA = 256


def _cast_kernel(w_ref, o_ref):
    o_ref[...] = w_ref[...].astype(o_ref.dtype)


def _cast_mxu(w):
    l, k, n = w.shape
    tr = _largest_tile(k, 512)
    return pl.pallas_call(
        _cast_kernel,
        out_shape=jax.ShapeDtypeStruct(w.shape, MXU),
        grid=(l, k // tr),
        in_specs=[pl.BlockSpec((None, tr, n), lambda i, r: (i, r, 0))],
        out_specs=pl.BlockSpec((None, tr, n), lambda i, r: (i, r, 0)),
        compiler_params=_cparams(("parallel", "parallel"), 12 * tr * n // (1 << 20) + 8),
        name="cast_w",
    )(w)


def _resid_kernel(*refs, rows, d, nsub, gate_chunk, next_chunk, split_a):
    a_ref = refs[0]
    if split_a:
        al_ref = refs[1]
        refs = refs[1:]
    w_ref, x_ref, modc_ref, gpost_ref = refs[1:5]
    pos = 5
    if next_chunk is not None:
        modn_ref, gpre_ref = refs[pos:pos + 2]
        pos += 2
    xo_ref = refs[pos]
    pos += 1
    if next_chunk is not None:
        ho_ref = refs[pos]
        pos += 1
    i = pl.program_id(0)
    cr = rows.cond_row(i)
    gate = modc_ref[pl.ds(cr, 1), pl.ds(gate_chunk * d, d)] * gpost_ref[...]
    if next_chunk is not None:
        shift = modn_ref[pl.ds(cr, 1), pl.ds(next_chunk * d, d)]
        scale = (1.0 + modn_ref[pl.ds(cr, 1), pl.ds((next_chunk + 1) * d, d)]) * gpre_ref[...]
    w = w_ref[...]
    tm = a_ref.shape[0]
    rs = tm // nsub
    def a_rows(s):
        r = slice(s * rs, (s + 1) * rs)
        return jnp.where(i < rows.nct, a_ref[r, :], al_ref[r, :]) if split_a else a_ref[r, :]

    ys = [jnp.dot(a_rows(s), w, preferred_element_type=F32) for s in range(nsub)]
    for s, y in enumerate(ys):
        r = slice(s * rs, (s + 1) * rs)
        yn = y * lax.rsqrt(jnp.mean(y * y, axis=-1, keepdims=True) + EPS)
        xn = x_ref[r, :] + yn * gate
        xo_ref[r, :] = xn
        if next_chunk is not None:
            hn = xn * lax.rsqrt(jnp.mean(xn * xn, axis=-1, keepdims=True) + EPS)
            ho_ref[r, :] = (hn * scale + shift).astype(ho_ref.dtype)


def _resid(a, w, li, x, mod, layer, gate_chunk, gpost, rows, nxt=None, nsub=1, name="resid"):
    split_a = isinstance(a, (tuple, list))
    a_parts = list(a) if split_a else [a]
    mt, kdim = rows.mt, a_parts[0].shape[1]
    d = w.shape[2]
    row_bytes = len(a_parts) * 2 * kdim * 2 + d * (8 + 8 + 4 + 12)
    budget = (VMEM_CAP_MIB - 6 << 20) - kdim * d * 2
    tm = _largest_tile(rows.tm, max(SUBLANES, 1 << int(math.log2(budget // row_bytes))))
    sub = _Rows(rows.b, rows.s, rows.db, rows.ds, tm)
    nct = sub.nct
    mspec = lambda l: pl.BlockSpec((None, rows.crows, mod.shape[2]), lambda i: (l, 0, 0))
    gspec = lambda l: pl.BlockSpec((None, 1, d), lambda i: (l, 0, 0))
    if split_a:
        a_specs = [pl.BlockSpec((tm, kdim), lambda i: (jnp.minimum(i, nct - 1), 0)),
                   pl.BlockSpec((tm, kdim), lambda i: (jnp.maximum(i - nct, 0), 0))]
    else:
        a_specs = [pl.BlockSpec((tm, kdim), lambda i: (i, 0))]
    in_specs = a_specs + [
        pl.BlockSpec((None, kdim, d), lambda i: (li, 0, 0), pipeline_mode=pl.Buffered(1)),
        pl.BlockSpec((tm, d), lambda i: (i, 0)), mspec(layer), gspec(layer)]
    args = a_parts + [w, x, mod, gpost]
    out_shape = [jax.ShapeDtypeStruct((mt, d), F32)]
    out_specs = [pl.BlockSpec((tm, d), lambda i: (i, 0))]
    if nxt is not None:
        nl, nchunk, gpre = nxt
        in_specs += [mspec(nl), gspec(nl)]
        args += [mod, gpre]
        out_shape.append(jax.ShapeDtypeStruct((mt, d), MXU))
        out_specs.append(pl.BlockSpec((tm, d), lambda i: (i, 0)))
    else:
        nchunk = None
    vmem = (kdim * d * 2 + tm * row_bytes) // (1 << 20) + 8
    res = pl.pallas_call(
        functools.partial(_resid_kernel, rows=sub, d=d, nsub=nsub, gate_chunk=gate_chunk,
                          next_chunk=nchunk, split_a=split_a),
        out_shape=tuple(out_shape),
        grid=(mt // tm,),
        in_specs=in_specs,
        out_specs=tuple(out_specs),
        compiler_params=_cparams(("arbitrary",), vmem),
        name=name,
    )(*args)
    return res if nxt is not None else (res[0], None)


def _with_ones(v):
    return jnp.concatenate([v, jnp.ones((v.shape[0], LANES), v.dtype)], axis=1)


def _attend(q, keys, vaugs, scale=None, masks=None):
    parts = [_dot_nt(q, kk) for kk in keys]
    if masks is not None:
        parts = [s if mk is None else jnp.where(mk, s, NEG) for s, mk in zip(parts, masks)]
    if scale is not None:
        parts = [s * scale for s in parts]
    m = functools.reduce(jnp.maximum, [jnp.max(s, axis=-1, keepdims=True) for s in parts])
    acc = functools.reduce(lambda a, b: a + b,
                           [_dot(jnp.exp(s - m), va) for s, va in zip(parts, vaugs)])
    return acc, m


def _attn_a_kernel(*refs, hb, qc, lam_init, has_cache):
    q_ref, k_ref, v_ref = refs[:3]
    pos = 3
    if has_cache:
        kc_ref, vc_ref = refs[pos:pos + 2]
        pos += 2
    lam_ref, sub_ref, o_ref = refs[pos:pos + 3]
    sq = q_ref.shape[0]
    dv = 2 * A_DH
    scale = A_DH ** -0.5
    lp = lam_ref[...]
    lam = (jnp.exp(jnp.sum(lp[0:1] * lp[1:2], axis=1, keepdims=True))
           - jnp.exp(jnp.sum(lp[2:3] * lp[3:4], axis=1, keepdims=True)) + lam_init)
    lo = lax.broadcasted_iota(jnp.int32, (1, dv), 1) < A_DH
    sub = sub_ref[...] * (1.0 - lam_init)
    for h in range(hb):
        cols = slice(h * dv, (h + 1) * dv)
        keys = [k_ref[:, cols].astype(MXU)]
        vals = [_with_ones(v_ref[:, cols].astype(MXU))]
        if has_cache:
            keys.append(kc_ref[:, cols].astype(MXU))
            vals.append(_with_ones(vc_ref[:, cols].astype(MXU)))

        def body(c, carry, cols=cols, keys=keys, vals=vals):
            r = pl.ds(pl.multiple_of(c * qc, qc), qc)
            q = q_ref[r, cols] * scale
            zero = jnp.zeros_like(q)
            a1, _ = _attend(jnp.where(lo, q, zero), keys, vals)
            a2, _ = _attend(jnp.where(lo, zero, q), keys, vals)
            o = a1[:, :dv] / a1[:, dv:] - lam * (a2[:, :dv] / a2[:, dv:])
            ms = jnp.mean(o * o, axis=-1, keepdims=True)
            o_ref[r, cols] = (o * lax.rsqrt(ms + EPS) * sub).astype(o_ref.dtype)
            return carry

        lax.fori_loop(0, sq // qc, body, 0, unroll=True)


def _attn_a(q, k, v, cache_k, cache_v, lam_p, subln, layer_idx, j, rows):
    d = q.shape[1]
    lam_init = 0.8 - 0.6 * math.exp(-0.3 * layer_idx)
    has_cache = cache_k is not None
    if has_cache:
        t, nb, roff = rows.ds, rows.db, rows.n_ctx // rows.ds
        hb = 4
    else:
        t, nb, roff = rows.s, rows.b, 0
        hb = A_HEADS
    cw = hb * 2 * A_DH
    qc = _largest_tile(t, 256)
    tok = lambda b, g: (b + roff, g)
    in_specs = [pl.BlockSpec((t, cw), tok)] * 3
    args = [q, k, v]
    if has_cache:
        p = cache_k.shape[2]
        cspec = pl.BlockSpec((None, None, p, cw), lambda b, g: (b, j, 0, g))
        in_specs += [cspec, cspec]
        args += [cache_k, cache_v]
    in_specs += [pl.BlockSpec((None, 4, A_DH), lambda b, g: (j, 0, 0)),
                 pl.BlockSpec((None, 1, 2 * A_DH), lambda b, g: (j, 0, 0))]
    args += [lam_p, subln]
    return pl.pallas_call(
        functools.partial(_attn_a_kernel, hb=hb, qc=qc, lam_init=lam_init, has_cache=has_cache),
        out_shape=jax.ShapeDtypeStruct((nb * t, d), MXU),
        grid=(nb, d // cw),
        in_specs=in_specs,
        out_specs=pl.BlockSpec((t, cw), lambda b, g: (b, g)),
        compiler_params=_cparams(("parallel", "parallel"), 40),
        name="attn_a_lat" if has_cache else "attn_a_ctx",
    )(*args)


def _gla_kernel(*refs, t, has_s0, has_sout):
    q_ref, v_ref, f0_ref, f1_ref, g_ref, gn_ref = refs[:6]
    pos = 6
    if has_s0:
        s0f_ref, s0b_ref = refs[pos:pos + 2]
        pos += 2
    o_ref = refs[pos]
    pos += 1
    if has_sout:
        sf_ref, sb_ref = refs[pos:pos + 2]
        pos += 2
    qdf, kif, qdb, kib, vb, bcf, bsb, of, ob, uf, ub = refs[pos:]
    c = B_CHUNK
    n = t // c
    r32 = lax.broadcasted_iota(jnp.int32, (t, 1), 0) % c

    def prefix(x):
        s = 1
        while s < c:
            x = x + jnp.where(r32 >= s, pltpu.roll(x, s, 0), 0.0)
            s *= 2
        return x

    def suffix(x):
        s = 1
        while s < c:
            x = x + jnp.where(r32 < c - s, pltpu.roll(x, t - s, 0), 0.0)
            s *= 2
        return x

    q = q_ref[...]
    vb[...] = v_ref[...].astype(vb.dtype)
    f = f0_ref[...]
    bc = prefix(jnp.log(f))
    bcf[...] = bc
    qdf[...] = (q * jnp.exp(bc)).astype(qdf.dtype)
    kif[...] = ((1.0 - f) * jnp.exp(-bc)).astype(kif.dtype)
    f = f1_ref[...]
    bs = suffix(jnp.log(f))
    bsb[...] = bs
    qdb[...] = (q * jnp.exp(bs)).astype(qdb.dtype)
    kib[...] = ((1.0 - f) * jnp.exp(-bs)).astype(kib.dtype)

    blk = _largest_tile(t, 256)
    ti = lax.broadcasted_iota(jnp.int32, (blk, blk), 0)
    si = lax.broadcasted_iota(jnp.int32, (blk, blk), 1)
    same = (ti // c) == (si // c)
    lower, upper = same & (ti >= si), same & (ti <= si)
    for bi in range(t // blk):
        rb = slice(bi * blk, (bi + 1) * blk)
        vv = vb[rb, :]
        of[rb, :] = _dot(jnp.where(lower, _dot_nt(qdf[rb, :], kif[rb, :]), 0.0), vv)
        ob[rb, :] = _dot(jnp.where(upper, _dot_nt(qdb[rb, :], kib[rb, :]), 0.0), vv)

    def incr(i, carry):
        r = pl.ds(pl.multiple_of(i * c, c), c)
        vv = vb[r, :]
        bc_c, bs_c = bcf[r, :], bsb[r, :]
        uf[i] = _dot_tn(vv, (1.0 - f0_ref[r, :]) * jnp.exp(bc_c[c - 1:c, :] - bc_c))
        ub[i] = _dot_tn(vv, (1.0 - f1_ref[r, :]) * jnp.exp(bs_c[0:1, :] - bs_c))
        return carry

    lax.fori_loop(0, n, incr, 0, unroll=min(8, n))

    def step(i, carry):
        sf, sb = carry
        rf = pl.ds(pl.multiple_of(i * c, c), c)
        of[rf, :] += _dot_nt(qdf[rf, :], sf)
        sf = sf * jnp.exp(bcf[pl.ds(i * c + c - 1, 1), :]) + uf[i]
        ib = n - 1 - i
        rb = pl.ds(pl.multiple_of(ib * c, c), c)
        ob[rb, :] += _dot_nt(qdb[rb, :], sb)
        sb = sb * jnp.exp(bsb[pl.ds(ib * c, 1), :]) + ub[ib]
        return sf, sb

    if has_s0:
        init = (s0f_ref[...].T, s0b_ref[...].T)
    else:
        init = (jnp.zeros((B_DK, B_DK), F32), jnp.zeros((B_DK, B_DK), F32))
    sf, sb = lax.fori_loop(0, n, step, init, unroll=min(8, n))
    if has_sout:
        sf_ref[...] = sf.T
        sb_ref[...] = sb.T
    o = of[...] + ob[...]
    o_ref[...] = (_rms(o, gn_ref[...]) * _silu(g_ref[...])).astype(o_ref.dtype)


def _gla(q, v, f0, f1, g, gnorm, s0f, s0b, j, rows):
    d = q.shape[1]
    has_s0 = s0f is not None
    if has_s0:
        t, nb, roff = rows.ds, rows.db, rows.n_ctx // rows.ds
    else:
        t, nb, roff = rows.s, rows.b, 0
    dv = d // B_HEADS
    assert dv == LANES and B_DK == LANES
    tok = pl.BlockSpec((t, LANES), lambda b, h: (b + roff, h))
    in_specs = [tok] * 5 + [pl.BlockSpec((None, 1, dv), lambda b, h: (j, 0, 0))]
    args = [q, v, f0, f1, g, gnorm]
    if has_s0:
        sspec = pl.BlockSpec((None, None, None, B_DK, dv), lambda b, h: (b, j, h, 0, 0))
        in_specs += [sspec, sspec]
        args += [s0f, s0b]
    out_shape = [jax.ShapeDtypeStruct((nb * t, d), MXU)]
    out_specs = [pl.BlockSpec((t, LANES), lambda b, h: (b, h))]
    if not has_s0:
        ospec = pl.BlockSpec((None, None, B_DK, dv), lambda b, h: (b, h, 0, 0))
        out_shape += [jax.ShapeDtypeStruct((nb, B_HEADS, B_DK, dv), F32)] * 2
        out_specs += [ospec, ospec]
    scratch = ([pltpu.VMEM((t, LANES), MXU) for _ in range(5)]
               + [pltpu.VMEM((t, LANES), F32) for _ in range(4)]
               + [pltpu.VMEM((t // B_CHUNK, dv, B_DK), F32) for _ in range(2)])
    return pl.pallas_call(
        functools.partial(_gla_kernel, t=t, has_s0=has_s0, has_sout=not has_s0),
        out_shape=tuple(out_shape),
        grid=(nb, B_HEADS),
        in_specs=in_specs,
        out_specs=tuple(out_specs),
        scratch_shapes=scratch,
        compiler_params=_cparams(("parallel", "parallel"), 32),
        name="gla_lat" if has_s0 else "gla_ctx",
    )(*args)


def _attn_c_kernel(*refs, qc, nh, has_cache):
    qn_ref, qp_ref, ckv_ref, kpe_ref = refs[:4]
    pos = 4
    if has_cache:
        ckvc_ref, kpec_ref = refs[pos:pos + 2]
        pos += 2
    wuk_ref, wuv_ref, o_ref = refs[pos:pos + 3]
    scr = refs[pos + 3:]
    sq = qn_ref.shape[0]
    scale = (C_NOPE + C_ROPE) ** -0.5
    wuk = wuk_ref[...].astype(MXU)
    wuv = wuv_ref[...].astype(MXU)

    kw = C_NOPE + 2 * C_ROPE
    vw = C_VD + LANES

    def expand(ckv_r, kpe_r, kcat_r, vcat_r):
        ck = ckv_r[...].astype(MXU)
        kn = jnp.dot(ck, wuk, preferred_element_type=F32).astype(kcat_r.dtype)
        vn = jnp.dot(ck, wuv, preferred_element_type=F32).astype(vcat_r.dtype)
        kp = kpe_r[...].astype(kcat_r.dtype)
        ones = jnp.ones((ck.shape[0], LANES), vcat_r.dtype)
        for hh in range(nh):
            kcat_r[:, hh * kw:(hh + 1) * kw] = jnp.concatenate(
                [kn[:, hh * C_NOPE:(hh + 1) * C_NOPE], kp, kp], axis=1)
            vcat_r[:, hh * vw:(hh + 1) * vw] = jnp.concatenate(
                [vn[:, hh * C_VD:(hh + 1) * C_VD], ones], axis=1)

    groups = [scr[0:2]]
    expand(ckv_ref, kpe_ref, *scr[0:2])
    if has_cache:
        groups.append(scr[2:4])
        expand(ckvc_ref, kpec_ref, *scr[2:4])
    lo = lax.broadcasted_iota(jnp.int32, (1, LANES), 1) < C_ROPE

    def body(c, carry):
        r = pl.ds(pl.multiple_of(c * qc, qc), qc)
        for hh in range(nh):
            cols = slice(hh * C_NOPE, (hh + 1) * C_NOPE)
            qp = qp_ref[r, (hh // 2) * LANES:(hh // 2 + 1) * LANES]
            zero = jnp.zeros_like(qp)
            qph = jnp.where(lo, qp, zero) if hh % 2 == 0 else jnp.where(lo, zero, qp)
            q = jnp.concatenate([qn_ref[r, cols], qph], axis=1)
            acc, _ = _attend(q, [kc[:, hh * kw:(hh + 1) * kw] for kc, _ in groups],
                             [vc[:, hh * vw:(hh + 1) * vw] for _, vc in groups], scale=scale)
            o_ref[r, cols] = (acc[:, :C_VD] / acc[:, C_VD:]).astype(o_ref.dtype)
        return carry

    lax.fori_loop(0, sq // qc, body, 0, unroll=True)


def _attn_c(qn, qp, ckv, kpe, cache_ckv, cache_kpe, wuk, wuv, j, rows):
    has_cache = cache_ckv is not None
    if has_cache:
        t, nb, roff = rows.ds, rows.db, rows.n_ctx // rows.ds
    else:
        t, nb, roff = rows.s, rows.b, 0
    nh = 2 if has_cache else 8
    hw = nh * C_NOPE
    qc = _largest_tile(t, 256)
    in_specs = [pl.BlockSpec((t, hw), lambda b, g: (b + roff, g)),
                pl.BlockSpec((t, nh * C_ROPE), lambda b, g: (b + roff, g)),
                pl.BlockSpec((t, C_KVLORA), lambda b, g: (b + roff, 0)),
                pl.BlockSpec((t, C_ROPE), lambda b, g: (b + roff, 0))]
    args = [qn, qp, ckv, kpe]
    kcw, vcw = nh * (C_NOPE + 2 * C_ROPE), nh * (C_VD + LANES)
    scratch = [pltpu.VMEM((t, kcw), MXU), pltpu.VMEM((t, vcw), MXU)]
    if has_cache:
        p = cache_ckv.shape[2]
        in_specs += [pl.BlockSpec((None, None, p, C_KVLORA), lambda b, g: (b, j, 0, 0)),
                     pl.BlockSpec((None, None, p, C_ROPE), lambda b, g: (b, j, 0, 0))]
        args += [cache_ckv, cache_kpe]
        scratch += [pltpu.VMEM((p, kcw), MXU), pltpu.VMEM((p, vcw), MXU)]
    in_specs += [pl.BlockSpec((None, C_KVLORA, hw), lambda b, g: (j, 0, g))] * 2
    args += [wuk, wuv]
    return pl.pallas_call(
        functools.partial(_attn_c_kernel, qc=qc, nh=nh, has_cache=has_cache),
        out_shape=jax.ShapeDtypeStruct((nb * t, C_HEADS * C_VD), MXU),
        grid=(nb, C_HEADS // nh),
        in_specs=in_specs,
        out_specs=pl.BlockSpec((t, hw), lambda b, g: (b, g)),
        scratch_shapes=scratch,
        compiler_params=_cparams(("parallel", "parallel"), 32),
        name="attn_c_lat" if has_cache else "attn_c_ctx",
    )(*args)


def _attn_d_kernel(*refs, has_cache):
    q_ref, k_ref, v_ref = refs[:3]
    pos = 3
    if has_cache:
        kc_ref, vc_ref = refs[pos:pos + 2]
        pos += 2
    sink_ref, o_ref = refs[pos:pos + 2]
    k2, v2 = refs[pos + 2:pos + 4]
    if has_cache:
        k2c, v2c = refs[pos + 4:pos + 6]
    t = q_ref.shape[0]
    qb = Q_BLOCK
    scale = D_DH ** -0.5
    r = D_HEADS // D_KV_HEADS
    win = min(t, qb + 2 * D_WINDOW)
    lo = lax.broadcasted_iota(jnp.int32, (1, LANES), 1) < D_DH
    rowblk = lax.broadcasted_iota(jnp.int32, (r * qb, 1), 0) // qb

    def stage(k_r, v_r, k2_r, v2_r, g):
        gs = slice(g * D_DH, (g + 1) * D_DH)
        kg, vg = k_r[:, gs].astype(k2_r.dtype), v_r[:, gs].astype(v2_r.dtype)
        k2_r[...] = jnp.concatenate([kg, kg], axis=1)
        v2_r[...] = _with_ones(jnp.concatenate([vg, vg], axis=1))

    for g in range(D_KV_HEADS):
        stage(k_ref, v_ref, k2, v2, g)
        if has_cache:
            stage(kc_ref, vc_ref, k2c, v2c, g)
        sink = jnp.zeros((r * qb, 1), F32)
        for e in range(r):
            h = g * r + e
            sink = jnp.where(rowblk == e, sink_ref[0:1, h:h + 1], sink)

        def body(c, carry, g=g, sink=sink):
            rq = pl.ds(pl.multiple_of(c * qb, qb), qb)
            tiles = []
            for e in range(r):
                cols = slice((g * r + e - e % 2) * D_DH, (g * r + e - e % 2 + 2) * D_DH)
                qp = q_ref[rq, cols] * scale
                zero = jnp.zeros_like(qp)
                tiles.append(jnp.where(lo, qp, zero) if e % 2 == 0 else jnp.where(lo, zero, qp))
            qs = jnp.concatenate(tiles, axis=0)
            if has_cache:
                ws = pl.multiple_of(jnp.clip(c * qb - D_WINDOW, 0, t - win), qb)
                rk = pl.ds(ws, win)
                qpos = c * qb + lax.broadcasted_iota(jnp.int32, (r * qb, 1), 0) % qb
                kpos = ws + lax.broadcasted_iota(jnp.int32, (1, win), 1)
                valid = jnp.abs(qpos - kpos) <= D_WINDOW
                parts = [_dot_nt(qs, k2c[...]), jnp.where(valid, _dot_nt(qs, k2[rk, :]), NEG)]
                vs = [v2c[...], v2[rk, :]]
            else:
                parts = [_dot_nt(qs, k2[...])]
                vs = [v2[...]]
            m = functools.reduce(jnp.maximum, [jnp.max(x, axis=-1, keepdims=True) for x in parts])
            m = jnp.maximum(m, sink)
            acc = functools.reduce(lambda a, b: a + b,
                                   [_dot(jnp.exp(x - m), vx) for x, vx in zip(parts, vs)])
            o = acc[:, :LANES] / (acc[:, LANES:] + jnp.exp(sink - m))
            for pr in range(r // 2):
                cols = slice((g * r + 2 * pr) * D_DH, (g * r + 2 * pr + 2) * D_DH)
                pair = jnp.where(lo, o[2 * pr * qb:(2 * pr + 1) * qb, :],
                                 o[(2 * pr + 1) * qb:(2 * pr + 2) * qb, :])
                o_ref[rq, cols] = pair.astype(o_ref.dtype)
            return carry

        lax.fori_loop(0, t // qb, body, 0, unroll=2)


def _attn_d(q, k, v, cache_k, cache_v, sink, j, rows):
    d = q.shape[1]
    kvw = D_KV_HEADS * D_DH
    has_cache = cache_k is not None
    if has_cache:
        t, nb, roff = rows.ds, rows.db, rows.n_ctx // rows.ds
    else:
        t, nb, roff = rows.s, rows.b, 0
    assert t % Q_BLOCK == 0
    in_specs = [pl.BlockSpec((t, d), lambda b: (b + roff, 0)),
                pl.BlockSpec((t, kvw), lambda b: (b + roff, 0)),
                pl.BlockSpec((t, kvw), lambda b: (b + roff, 0))]
    args = [q, k, v]
    scratch = [pltpu.VMEM((t, LANES), MXU), pltpu.VMEM((t, 2 * LANES), MXU)]
    if has_cache:
        p = cache_k.shape[2]
        cspec = pl.BlockSpec((None, None, p, kvw), lambda b: (b, j, 0, 0))
        in_specs += [cspec, cspec]
        args += [cache_k, cache_v]
        scratch += [pltpu.VMEM((p, LANES), MXU), pltpu.VMEM((p, 2 * LANES), MXU)]
    in_specs.append(pl.BlockSpec((None, 1, D_HEADS), lambda b: (j, 0, 0)))
    args.append(sink)
    return pl.pallas_call(
        functools.partial(_attn_d_kernel, has_cache=has_cache),
        out_shape=jax.ShapeDtypeStruct((nb * t, d), MXU),
        grid=(nb,),
        in_specs=in_specs,
        out_specs=pl.BlockSpec((t, d), lambda b: (b, 0)),
        scratch_shapes=scratch,
        compiler_params=_cparams(("parallel",), 32),
        name="attn_d_lat" if has_cache else "attn_d_ctx",
    )(*args)


def kernel(x_prompt, x_sample, cache_a_k, cache_a_v, state_b_fwd, state_b_bwd, cache_c_ckv,
           cache_c_kpe, cache_d_k, cache_d_v, c, c_ctx, ada_w, ada_b, norm_mix_pre,
           norm_mix_post, norm_ffn_pre, norm_ffn_post, a_wq, a_wk, a_wv, a_wo, a_lambda,
           a_subln, b_wq, b_wi, b_wf, b_lower, b_wg, b_gnorm, b_wo, c_wdq, c_qnorm, c_wuq,
           c_wdkv, c_kvnorm, c_wuk, c_wuv, c_wo, d_wq, d_wk, d_wv, d_sink, d_wo, ffn_wg,
           ffn_wu, ffn_wd):
    b, s, d = x_prompt.shape
    db, ds, _ = x_sample.shape
    depth = ada_w.shape[0]
    rows = _Rows(b, s, db, ds, _largest_tile(math.gcd(b * s, ds), 1024))
    n_ctx = rows.n_ctx

    cond = jnp.zeros((rows.crows, d), F32).at[:db].set(c).at[db].set(c_ctx)
    mod = _ada(cond, ada_w, ada_b)
    g3 = lambda a: a.reshape(a.shape[0], 1, a.shape[1])
    n_mix_pre, n_mix_post = g3(norm_mix_pre), g3(norm_mix_post)
    n_ffn_pre, n_ffn_post = g3(norm_ffn_pre), g3(norm_ffn_post)
    tabs = _rope_tables(ds)
    rope_epi = lambda tn: (_make_epi_rope(rows, tn), _rope_extra(tabs, rows))

    x, h = _prep(x_prompt.reshape(n_ctx, d), x_sample.reshape(rows.n_lat, d),
                 n_mix_pre, mod, 0, rows)

    flat4 = lambda a: a.reshape(a.shape[:3] + (-1,))
    wo_b = {0: _cast_mxu(a_wo), 1: _cast_mxu(b_wo), 2: _cast_mxu(c_wo), 3: _cast_mxu(d_wo)}
    wd_b = _cast_mxu(ffn_wd)
    wf = b_wf.reshape((-1,) + b_wf.shape[2:])
    outs = {k_: [] for k_ in ("a_k", "a_v", "b_f", "b_b", "c_ckv", "c_kpe", "d_k", "d_v")}
    for i in range(depth):
        m, j = i % N_MIXERS, i // N_MIXERS
        if m == 0:
            tn = _largest_tile(a_wq.shape[2], 1024)
            epi, ext = rope_epi(tn)
            q, = _proj(h, [(a_wq, j)], epi, [(tn, MXU)], rows, tn, ext, name="a_q")
            k, = _proj(h, [(a_wk, j)], epi, [(tn, F32)], rows, tn, ext, name="a_k")
            v, = _proj(h, [(a_wv, j)], _epi_store, [(tn, F32)], rows, tn, name="a_v")
            o = (_attn_a(q, k, v, None, None, a_lambda, g3(a_subln), i, j, rows),
                 _attn_a(q, k, v, flat4(cache_a_k), flat4(cache_a_v), a_lambda, g3(a_subln),
                         i, j, rows))
            outs["a_k"].append(k[:n_ctx].reshape(b, s, A_HEADS, 2 * A_DH))
            outs["a_v"].append(v[:n_ctx].reshape(b, s, A_HEADS, 2 * A_DH))
        elif m == 1:
            tn = _largest_tile(b_wq.shape[2], 1024)
            q, = _proj(h, [(b_wq, j)], _epi_silu, [(tn, F32)], rows, tn, name="b_q")
            vi, = _proj(h, [(b_wi, j)], _epi_store, [(tn, MXU)], rows, tn, name="b_i")
            g, = _proj(h, [(b_wg, j)], _epi_store, [(tn, F32)], rows, tn, name="b_g")
            fs = []
            for dr in range(2):
                bl_spec = pl.BlockSpec((None, depth, tn), lambda jn, mm, dr=dr: (dr, 0, jn))
                f, = _proj(h, [(wf, 2 * j + dr)], _make_epi_forget(i, depth), [(tn, F32)], rows,
                           tn, [(jnp.swapaxes(b_lower, 0, 1), bl_spec)], name="b_f%d" % dr)
                fs.append(f)
            oc, sf, sb = _gla(q, vi, fs[0], fs[1], g, g3(b_gnorm), None, None, j, rows)
            ol, = _gla(q, vi, fs[0], fs[1], g, g3(b_gnorm), state_b_fwd, state_b_bwd, j, rows)
            o = (oc, ol)
            outs["b_f"].append(sf)
            outs["b_b"].append(sb)
        elif m == 2:
            nq = c_wdq.shape[2]
            cq, = _proj(h, [(c_wdq, j)], _make_epi_rmsnorm(), [(nq, MXU)], rows, nq,
                        [(g3(c_qnorm), pl.BlockSpec((None, 1, nq), lambda jn, mm: (j, 0, 0)))],
                        name="c_dq")
            nkv = c_wdkv.shape[2]
            ext = [(g3(c_kvnorm), pl.BlockSpec((None, 1, C_KVLORA), lambda jn, mm: (j, 0, 0)))]
            ext += _rope_extra(tabs, rows)
            ckv, kpe = _proj(h, [(c_wdkv, j)], _make_epi_ckv(rows),
                             [(C_KVLORA, F32), (C_ROPE, F32)], rows, nkv, ext, name="c_dkv")
            wuq = c_wuq.reshape(c_wuq.shape[0], nq, C_HEADS, C_NOPE + C_ROPE)
            wuq_n = wuq[:, :, :, :C_NOPE].reshape(-1, nq, C_HEADS * C_NOPE)
            wuq_p = wuq[:, :, :, C_NOPE:].reshape(-1, nq, C_HEADS * C_ROPE)
            tn = _largest_tile(wuq_n.shape[2], 1024)
            qn, = _proj(cq, [(wuq_n, j)], _epi_store, [(tn, MXU)], rows, tn, name="c_qn")
            tnp = wuq_p.shape[2]
            epi, ext = rope_epi(tnp)
            qp, = _proj(cq, [(wuq_p, j)], epi, [(tnp, MXU)], rows, tnp, ext, name="c_qp")
            o = (_attn_c(qn, qp, ckv, kpe, None, None, c_wuk, c_wuv, j, rows),
                 _attn_c(qn, qp, ckv, kpe, cache_c_ckv, cache_c_kpe, c_wuk, c_wuv, j, rows))
            outs["c_ckv"].append(ckv[:n_ctx].reshape(b, s, C_KVLORA))
            outs["c_kpe"].append(kpe[:n_ctx].reshape(b, s, C_ROPE))
        else:
            tn = _largest_tile(d_wq.shape[2], 1024)
            epi, ext = rope_epi(tn)
            q, = _proj(h, [(d_wq, j)], epi, [(tn, MXU)], rows, tn, ext, name="d_q")
            kvw = d_wk.shape[2]
            epi, ext = rope_epi(kvw)
            k, = _proj(h, [(d_wk, j)], epi, [(kvw, F32)], rows, kvw, ext, name="d_k")
            v, = _proj(h, [(d_wv, j)], _epi_store, [(kvw, F32)], rows, kvw, name="d_v")
            o = (_attn_d(q, k, v, None, None, g3(d_sink), j, rows),
                 _attn_d(q, k, v, flat4(cache_d_k), flat4(cache_d_v), g3(d_sink), j, rows))
            outs["d_k"].append(k[:n_ctx].reshape(b, s, D_KV_HEADS, D_DH))
            outs["d_v"].append(v[:n_ctx].reshape(b, s, D_KV_HEADS, D_DH))

        x, h = _resid(o, wo_b[m], j, x, mod, i, 2, n_mix_post, rows, nxt=(i, 3, n_ffn_pre),
                      nsub=2, name="mix_out")
        tf = _largest_tile(ffn_wg.shape[2], 512)
        a, = _proj(h, [(ffn_wg, i), (ffn_wu, i)], _epi_swiglu, [(tf, MXU)], rows, tf,
                   name="ffn_gu")
        nxt = (i + 1, 0, n_mix_pre) if i + 1 < depth else None
        x, h = _resid(a, wd_b, i, x, mod, i, 5, n_ffn_post, rows, nxt=nxt, name="ffn_down")

    stack = lambda lst: jnp.stack(lst, axis=1)
    return (x[:n_ctx].reshape(b, s, d), x[n_ctx:].reshape(db, ds, d),
            stack(outs["a_k"]), stack(outs["a_v"]), stack(outs["b_f"]), stack(outs["b_b"]),
            stack(outs["c_ckv"]), stack(outs["c_kpe"]), stack(outs["d_k"]), stack(outs["d_v"]))
```

```python
import functools
import math

import numpy as np
import jax
import jax.numpy as jnp
from jax import lax
from jax.experimental import pallas as pl
from jax.experimental.pallas import tpu as pltpu

EPS = 1e-6
ROPE_BASE = 10000.0
GRID_W = 64
NEG = -1e30
N_MIXERS = 4
A_HEADS, A_DH = 16, 64
B_HEADS, B_DK, B_CHUNK = 16, 128, 32
C_HEADS, C_NOPE, C_ROPE, C_VD = 16, 128, 64, 128
D_HEADS, D_KV_HEADS, D_DH, D_WINDOW, Q_BLOCK = 32, 4, 64, 128, 128

LANES = 128
SUBLANES = 8
VMEM_CAP_MIB = 56

MXU = jnp.bfloat16
F32 = jnp.float32


def _cparams(sem, vmem_mib):
    return pltpu.CompilerParams(dimension_semantics=sem,
                                vmem_limit_bytes=min(vmem_mib, VMEM_CAP_MIB) << 20)


def _dot(a, b):
    return jnp.dot(a.astype(MXU), b.astype(MXU), preferred_element_type=F32)


def _dot_nt(a, b):
    return lax.dot_general(a.astype(MXU), b.astype(MXU), (((1,), (1,)), ((), ())),
                           preferred_element_type=F32)


def _dot_tn(a, b):
    return lax.dot_general(a.astype(MXU), b.astype(MXU), (((0,), (0,)), ((), ())),
                           preferred_element_type=F32)


def _sigmoid(x):
    return 1.0 / (1.0 + jnp.exp(-x))


def _silu(x):
    return x * _sigmoid(x)


def _rms(x, g):
    ms = jnp.mean(x * x, axis=-1, keepdims=True)
    return x * lax.rsqrt(ms + EPS) * g


def _largest_tile(n, pref):
    t = min(n, pref)
    while n % t:
        t -= SUBLANES
    assert t > 0 and n % t == 0
    return t


class _Rows:
    def __init__(self, b, s, db, ds, tm):
        self.b, self.s, self.db, self.ds = b, s, db, ds
        self.n_ctx, self.n_lat = b * s, db * ds
        self.mt = self.n_ctx + self.n_lat
        self.tm = tm
        assert self.n_ctx % tm == 0 and ds % tm == 0 and self.n_ctx % ds == 0
        self.nct = self.n_ctx // tm
        self.ntiles = self.mt // tm
        self.crows = -(-(db + 1) // SUBLANES) * SUBLANES

    def cond_row(self, i):
        lat = ((i - self.nct) * self.tm) // self.ds
        return jnp.where(i < self.nct, self.db, lat)

    def pos_block(self, i):
        return (jnp.maximum(i - self.nct, 0) * self.tm % self.ds) // self.tm


def _rope_tables(n_tok):
    nf = 16
    inv = (ROPE_BASE ** (-np.arange(nf, dtype=np.float32) / nf)).astype(np.float32)
    t = np.arange(n_tok)
    row, col = (t // GRID_W).astype(np.float32), (t % GRID_W).astype(np.float32)
    lane = np.arange(64)
    pos = np.where(lane[None, :] < 32, row[:, None], col[:, None]).astype(np.float32)
    ang = (pos * inv[lane % nf][None, :]).astype(np.float32)
    cos, sin = np.cos(ang).astype(np.float32), np.sin(ang).astype(np.float32)
    first = (lane % 32) < 16
    sa = np.where(first[None, :], -sin, 0.0).astype(np.float32)
    sb = np.where(first[None, :], 0.0, sin).astype(np.float32)
    tile2 = lambda a: np.concatenate([a, a], axis=1)
    return jnp.asarray(tile2(cos)), jnp.asarray(tile2(sa)), jnp.asarray(tile2(sb))


def _rope128(y, cos, sa, sb):
    return (y * cos + pltpu.roll(y, LANES - 16, 1) * sa + pltpu.roll(y, 16, 1) * sb)


def _ada_kernel(c_ref, w_ref, b_ref, o_ref):
    o_ref[...] = _dot(_silu(c_ref[...]), w_ref[...]) + b_ref[...]


def _ada(cond, ada_w, ada_b):
    depth, d, n = ada_w.shape
    cr = cond.shape[0]
    tn = _largest_tile(n, 1024)
    return pl.pallas_call(
        _ada_kernel,
        out_shape=jax.ShapeDtypeStruct((depth, cr, n), F32),
        grid=(depth, n // tn),
        in_specs=[pl.BlockSpec((cr, d), lambda l, j: (0, 0)),
                  pl.BlockSpec((None, d, tn), lambda l, j: (l, 0, j)),
                  pl.BlockSpec((None, 1, tn), lambda l, j: (l, 0, j))],
        out_specs=pl.BlockSpec((None, cr, tn), lambda l, j: (l, 0, j)),
        compiler_params=_cparams(("parallel", "parallel"), 40),
        name="ada_mod",
    )(cond, ada_w, ada_b.reshape(depth, 1, n))


def _prep_kernel(xp_ref, xs_ref, g_ref, mod_ref, x_ref, h_ref, *, rows, d):
    i = pl.program_id(0)
    x = jnp.where(i < rows.nct, xp_ref[...], xs_ref[...])
    cr = rows.cond_row(i)
    shift = mod_ref[pl.ds(cr, 1), pl.ds(0, d)]
    scale = mod_ref[pl.ds(cr, 1), pl.ds(d, d)]
    x_ref[...] = x
    h_ref[...] = (_rms(x, g_ref[...]) * (1.0 + scale) + shift).astype(h_ref.dtype)


def _prep(xp, xs, g, mod, layer, rows):
    d = xp.shape[1]
    rows = _Rows(rows.b, rows.s, rows.db, rows.ds, _largest_tile(rows.tm, 256))
    tm, nct = rows.tm, rows.nct
    return pl.pallas_call(
        functools.partial(_prep_kernel, rows=rows, d=d),
        out_shape=(jax.ShapeDtypeStruct((rows.mt, d), F32),
                   jax.ShapeDtypeStruct((rows.mt, d), MXU)),
        grid=(rows.ntiles,),
        in_specs=[pl.BlockSpec((tm, d), lambda i: (jnp.minimum(i, nct - 1), 0)),
                  pl.BlockSpec((tm, d), lambda i: (jnp.maximum(i - nct, 0), 0)),
                  pl.BlockSpec((None, 1, d), lambda i: (layer, 0, 0)),
                  pl.BlockSpec((None, rows.crows, mod.shape[2]), lambda i: (layer, 0, 0))],
        out_specs=(pl.BlockSpec((tm, d), lambda i: (i, 0)),
                   pl.BlockSpec((tm, d), lambda i: (i, 0))),
        compiler_params=_cparams(("parallel",), 40 * tm * d // (1 << 20) + 8),
        name="prep",
    )(xp, xs, g, mod)


def _proj_kernel(*refs, n_w, n_extra, n_out, nsub, epi):
    h_ref = refs[0]
    w_refs = refs[1:1 + n_w]
    extra = refs[1 + n_w:1 + n_w + n_extra]
    outs = refs[1 + n_w + n_extra:1 + n_w + n_extra + n_out]
    wb_refs = refs[1 + n_w + n_extra + n_out:]
    m = pl.program_id(1)

    @pl.when(m == 0)
    def _():
        for w_ref, wb_ref in zip(w_refs, wb_refs):
            wb_ref[...] = w_ref[...].astype(wb_ref.dtype)

    rs = h_ref.shape[0] // nsub
    for s in range(nsub):
        r = slice(s * rs, (s + 1) * rs)
        h = h_ref[r, :]
        ys = [jnp.dot(h, wb_ref[...], preferred_element_type=F32) for wb_ref in wb_refs]
        epi(ys, m, extra, outs, r)


def _proj(h, ws, epi, outs, rows, tn, extra=(), tm=None, nsub=1, name="proj"):
    mt, k = h.shape
    n = ws[0][0].shape[2]
    assert n % tn == 0
    tm = rows.tm if tm is None else tm
    assert mt % tm == 0
    nj = n // tn
    in_specs = [pl.BlockSpec((tm, k), lambda j, m: (m, 0))]
    in_specs += [pl.BlockSpec((None, k, tn), lambda j, m, li=li: (li, 0, j)) for _, li in ws]
    in_specs += [s for _, s in extra]
    nct = rows.nct
    ctx_only = [len(o) > 2 and o[2] == "ctx" for o in outs]
    outs = [o[:2] for o in outs]
    out_shape = tuple(jax.ShapeDtypeStruct((rows.n_ctx if co else mt, c * nj), dt)
                      for (c, dt), co in zip(outs, ctx_only))
    out_specs = tuple(pl.BlockSpec((tm, c), (lambda j, m: (jnp.minimum(m, nct - 1), j)) if co
                                   else (lambda j, m: (m, j)))
                      for (c, _), co in zip(outs, ctx_only))
    assert not any(ctx_only) or tm == rows.tm
    out_bytes = sum(tm * c * jnp.dtype(dt).itemsize for c, dt in outs)
    vmem = (2 * tm * k * 2 + len(ws) * (2 * k * tn * 4 + k * tn * 2) + 2 * out_bytes
            + len(ws) * tm * tn * 8) // (1 << 20) + 6
    res = pl.pallas_call(
        functools.partial(_proj_kernel, n_w=len(ws), n_extra=len(extra), n_out=len(outs),
                          nsub=nsub, epi=epi),
        out_shape=out_shape,
        grid=(nj, mt // tm),
        in_specs=in_specs,
        out_specs=out_specs,
        scratch_shapes=[pltpu.VMEM((k, tn), MXU) for _ in ws],
        compiler_params=_cparams(("parallel", "arbitrary"), vmem),
        name=name,
    )(h, *[w for w, _ in ws], *[a for a, _ in extra])
    return res


def _epi_store(ys, m, extra, outs, r):
    outs[0][r, :] = ys[0].astype(outs[0].dtype)


def _epi_silu(ys, m, extra, outs, r):
    outs[0][r, :] = _silu(ys[0]).astype(outs[0].dtype)


def _epi_swiglu(ys, m, extra, outs, r):
    outs[0][r, :] = (_silu(ys[0]) * ys[1]).astype(outs[0].dtype)


def _make_epi_rope(rows, tn):
    def epi(ys, m, extra, outs, r):
        cos_ref, sa_ref, sb_ref = extra
        y, o_ref = ys[0], outs[0]

        @pl.when(m < rows.nct)
        def _():
            o_ref[r, :] = y.astype(o_ref.dtype)

        @pl.when(m >= rows.nct)
        def _():
            cos, sa, sb = cos_ref[r, :], sa_ref[r, :], sb_ref[r, :]
            for c in range(tn // LANES):
                sl = slice(c * LANES, (c + 1) * LANES)
                o_ref[r, sl] = _rope128(y[:, sl], cos, sa, sb).astype(o_ref.dtype)
    return epi


def _with_ctx_copy(epi, rows):
    def wrapped(ys, m, extra, outs, r):
        epi(ys, m, extra, outs[:-1], r)

        @pl.when(m < rows.nct)
        def _():
            outs[-1][r, :] = ys[0]
    return wrapped


def _rope_extra(tabs, rows):
    tm = rows.tm
    spec = pl.BlockSpec((tm, LANES), lambda j, m: (rows.pos_block(m), 0))
    return [(t, spec) for t in tabs]


def _make_epi_forget(layer_idx, depth):
    def epi(ys, m, extra, outs, r):
        bl = extra[0][...]
        e = jnp.exp(bl - jnp.max(bl, axis=0, keepdims=True))
        p = e / jnp.sum(e, axis=0, keepdims=True)
        lb = jnp.sum(p[0:layer_idx + 1], axis=0, keepdims=True) - p[0:1]
        outs[0][r, :] = lb + (1.0 - lb) * _sigmoid(ys[0])
    return epi


def _make_epi_rmsnorm():
    def epi(ys, m, extra, outs, r):
        outs[0][r, :] = _rms(ys[0], extra[0][...]).astype(outs[0].dtype)
    return epi


def _make_epi_ckv(rows):
    def epi(ys, m, extra, outs, r):
        g_ref, cos_ref, sa_ref, sb_ref = extra
        ckv_all, kpe_all, ckv_ctx, kpe_ctx = outs
        y = ys[0]
        ckv = _rms(y[:, :C_KVLORA], g_ref[...])
        ckv_all[r, :] = ckv.astype(ckv_all.dtype)
        kpe = y[:, C_KVLORA:C_KVLORA + C_ROPE]

        @pl.when(m < rows.nct)
        def _():
            kpe_all[r, :] = kpe.astype(kpe_all.dtype)
            ckv_ctx[r, :] = ckv
            kpe_ctx[r, :] = kpe

        @pl.when(m >= rows.nct)
        def _():
            k2 = jnp.concatenate([kpe, kpe], axis=1)
            kr = _rope128(k2, cos_ref[r, :], sa_ref[r, :], sb_ref[r, :])
            kpe_all[r, :] = kr[:, :C_ROPE].astype(kpe_all.dtype)
    return epi


C_KVLORA = 256


def _cast_kernel(w_ref, o_ref):
    o_ref[...] = w_ref[...].astype(o_ref.dtype)


def _cast_mxu(w):
    l, k, n = w.shape
    tr = _largest_tile(k, 512)
    return pl.pallas_call(
        _cast_kernel,
        out_shape=jax.ShapeDtypeStruct(w.shape, MXU),
        grid=(l, k // tr),
        in_specs=[pl.BlockSpec((None, tr, n), lambda i, r: (i, r, 0))],
        out_specs=pl.BlockSpec((None, tr, n), lambda i, r: (i, r, 0)),
        compiler_params=_cparams(("parallel", "parallel"), 12 * tr * n // (1 << 20) + 8),
        name="cast_w",
    )(w)


def _resid_kernel(*refs, rows, d, nsub, gate_chunk, next_chunk, split_a):
    a_ref = refs[0]
    if split_a:
        al_ref = refs[1]
        refs = refs[1:]
    w_ref, x_ref, modc_ref, gpost_ref = refs[1:5]
    pos = 5
    if next_chunk is not None:
        modn_ref, gpre_ref = refs[pos:pos + 2]
        pos += 2
    xo_ref = refs[pos]
    pos += 1
    if next_chunk is not None:
        ho_ref = refs[pos]
    else:
        xl_ref = refs[pos]
    i = pl.program_id(0)
    cr = rows.cond_row(i)
    gate = modc_ref[pl.ds(cr, 1), pl.ds(gate_chunk * d, d)] * gpost_ref[...]
    if next_chunk is not None:
        shift = modn_ref[pl.ds(cr, 1), pl.ds(next_chunk * d, d)]
        scale = (1.0 + modn_ref[pl.ds(cr, 1), pl.ds((next_chunk + 1) * d, d)]) * gpre_ref[...]
    w = w_ref[...]
    tm = a_ref.shape[0]
    rs = tm // nsub
    def a_rows(s):
        r = slice(s * rs, (s + 1) * rs)
        return jnp.where(i < rows.nct, a_ref[r, :], al_ref[r, :]) if split_a else a_ref[r, :]

    ys = [jnp.dot(a_rows(s), w, preferred_element_type=F32) for s in range(nsub)]
    for s, y in enumerate(ys):
        r = slice(s * rs, (s + 1) * rs)
        yn = y * lax.rsqrt(jnp.mean(y * y, axis=-1, keepdims=True) + EPS)
        xn = x_ref[r, :] + yn * gate
        if next_chunk is not None:
            xo_ref[r, :] = xn
            hn = xn * lax.rsqrt(jnp.mean(xn * xn, axis=-1, keepdims=True) + EPS)
            ho_ref[r, :] = (hn * scale + shift).astype(ho_ref.dtype)
        else:
            @pl.when(i < rows.nct)
            def _(xn=xn, r=r):
                xo_ref[r, :] = xn

            @pl.when(i >= rows.nct)
            def _(xn=xn, r=r):
                xl_ref[r, :] = xn


def _resid(a, w, li, x, mod, layer, gate_chunk, gpost, rows, nxt=None, nsub=1, name="resid"):
    split_a = isinstance(a, (tuple, list))
    a_parts = list(a) if split_a else [a]
    mt, kdim = rows.mt, a_parts[0].shape[1]
    d = w.shape[2]
    row_bytes = len(a_parts) * 2 * kdim * 2 + d * (8 + 8 + 4 + 12)
    budget = (VMEM_CAP_MIB - 6 << 20) - kdim * d * 2
    tm = _largest_tile(rows.tm, max(SUBLANES, 1 << int(math.log2(budget // row_bytes))))
    sub = _Rows(rows.b, rows.s, rows.db, rows.ds, tm)
    nct = sub.nct
    mspec = lambda l: pl.BlockSpec((None, rows.crows, mod.shape[2]), lambda i: (l, 0, 0))
    gspec = lambda l: pl.BlockSpec((None, 1, d), lambda i: (l, 0, 0))
    if split_a:
        a_specs = [pl.BlockSpec((tm, kdim), lambda i: (jnp.minimum(i, nct - 1), 0)),
                   pl.BlockSpec((tm, kdim), lambda i: (jnp.maximum(i - nct, 0), 0))]
    else:
        a_specs = [pl.BlockSpec((tm, kdim), lambda i: (i, 0))]
    in_specs = a_specs + [
        pl.BlockSpec((None, kdim, d), lambda i: (li, 0, 0), pipeline_mode=pl.Buffered(1)),
        pl.BlockSpec((tm, d), lambda i: (i, 0)), mspec(layer), gspec(layer)]
    args = a_parts + [w, x, mod, gpost]
    if nxt is not None:
        nl, nchunk, gpre = nxt
        in_specs += [mspec(nl), gspec(nl)]
        args += [mod, gpre]
        out_shape = [jax.ShapeDtypeStruct((mt, d), F32), jax.ShapeDtypeStruct((mt, d), MXU)]
        out_specs = [pl.BlockSpec((tm, d), lambda i: (i, 0))] * 2
    else:
        nchunk = None
        out_shape = [jax.ShapeDtypeStruct((rows.n_ctx, d), F32),
                     jax.ShapeDtypeStruct((rows.n_lat, d), F32)]
        out_specs = [pl.BlockSpec((tm, d), lambda i: (jnp.minimum(i, nct - 1), 0)),
                     pl.BlockSpec((tm, d), lambda i: (jnp.maximum(i - nct, 0), 0))]
    vmem = (kdim * d * 2 + tm * row_bytes) // (1 << 20) + 8
    res = pl.pallas_call(
        functools.partial(_resid_kernel, rows=sub, d=d, nsub=nsub, gate_chunk=gate_chunk,
                          next_chunk=nchunk, split_a=split_a),
        out_shape=tuple(out_shape),
        grid=(mt // tm,),
        in_specs=in_specs,
        out_specs=tuple(out_specs),
        compiler_params=_cparams(("arbitrary",), vmem),
        name=name,
    )(*args)
    return res


def _with_ones(v):
    return jnp.concatenate([v, jnp.ones((v.shape[0], LANES), v.dtype)], axis=1)


def _attend(q, keys, vaugs, scale=None, masks=None):
    parts = [_dot_nt(q, kk) for kk in keys]
    if masks is not None:
        parts = [s if mk is None else jnp.where(mk, s, NEG) for s, mk in zip(parts, masks)]
    if scale is not None:
        parts = [s * scale for s in parts]
    m = functools.reduce(jnp.maximum, [jnp.max(s, axis=-1, keepdims=True) for s in parts])
    acc = functools.reduce(lambda a, b: a + b,
                           [_dot(jnp.exp(s - m), va) for s, va in zip(parts, vaugs)])
    return acc, m


def _attn_a_kernel(*refs, hb, qc, lam_init, has_cache):
    q_ref, k_ref, v_ref = refs[:3]
    pos = 3
    if has_cache:
        kc_ref, vc_ref = refs[pos:pos + 2]
        pos += 2
    lam_ref, sub_ref, o_ref = refs[pos:pos + 3]
    sq = q_ref.shape[0]
    dv = 2 * A_DH
    scale = A_DH ** -0.5
    lp = lam_ref[...]
    lam = (jnp.exp(jnp.sum(lp[0:1] * lp[1:2], axis=1, keepdims=True))
           - jnp.exp(jnp.sum(lp[2:3] * lp[3:4], axis=1, keepdims=True)) + lam_init)
    lo = lax.broadcasted_iota(jnp.int32, (1, dv), 1) < A_DH
    sub = sub_ref[...] * (1.0 - lam_init)
    for h in range(hb):
        cols = slice(h * dv, (h + 1) * dv)
        keys = [k_ref[:, cols].astype(MXU)]
        vals = [_with_ones(v_ref[:, cols].astype(MXU))]
        if has_cache:
            keys.append(kc_ref[:, cols].astype(MXU))
            vals.append(_with_ones(vc_ref[:, cols].astype(MXU)))

        def body(c, carry, cols=cols, keys=keys, vals=vals):
            r = pl.ds(pl.multiple_of(c * qc, qc), qc)
            q = q_ref[r, cols] * scale
            zero = jnp.zeros_like(q)
            a1, _ = _attend(jnp.where(lo, q, zero), keys, vals)
            a2, _ = _attend(jnp.where(lo, zero, q), keys, vals)
            o = a1[:, :dv] / a1[:, dv:] - lam * (a2[:, :dv] / a2[:, dv:])
            ms = jnp.mean(o * o, axis=-1, keepdims=True)
            o_ref[r, cols] = (o * lax.rsqrt(ms + EPS) * sub).astype(o_ref.dtype)
            return carry

        lax.fori_loop(0, sq // qc, body, 0, unroll=True)


def _attn_a(q, k, v, cache_k, cache_v, lam_p, subln, layer_idx, j, rows):
    d = q.shape[1]
    lam_init = 0.8 - 0.6 * math.exp(-0.3 * layer_idx)
    has_cache = cache_k is not None
    if has_cache:
        t, nb, roff = rows.ds, rows.db, rows.n_ctx // rows.ds
        hb = 4
    else:
        t, nb, roff = rows.s, rows.b, 0
        hb = A_HEADS
    cw = hb * 2 * A_DH
    qc = _largest_tile(t, 256)
    tok = lambda b, g: (b + roff, g)
    in_specs = [pl.BlockSpec((t, cw), tok)] * 3
    args = [q, k, v]
    if has_cache:
        p = cache_k.shape[2]
        cspec = pl.BlockSpec((None, None, p, cw), lambda b, g: (b, j, 0, g))
        in_specs += [cspec, cspec]
        args += [cache_k, cache_v]
    in_specs += [pl.BlockSpec((None, 4, A_DH), lambda b, g: (j, 0, 0)),
                 pl.BlockSpec((None, 1, 2 * A_DH), lambda b, g: (j, 0, 0))]
    args += [lam_p, subln]
    return pl.pallas_call(
        functools.partial(_attn_a_kernel, hb=hb, qc=qc, lam_init=lam_init, has_cache=has_cache),
        out_shape=jax.ShapeDtypeStruct((nb * t, d), MXU),
        grid=(nb, d // cw),
        in_specs=in_specs,
        out_specs=pl.BlockSpec((t, cw), lambda b, g: (b, g)),
        compiler_params=_cparams(("parallel", "parallel"), 40),
        name="attn_a_lat" if has_cache else "attn_a_ctx",
    )(*args)


def _gla_kernel(*refs, t, has_s0, has_sout):
    q_ref, v_ref, f0_ref, f1_ref, g_ref, gn_ref = refs[:6]
    pos = 6
    if has_s0:
        s0f_ref, s0b_ref = refs[pos:pos + 2]
        pos += 2
    o_ref = refs[pos]
    pos += 1
    if has_sout:
        sf_ref, sb_ref = refs[pos:pos + 2]
        pos += 2
    qdf, kif, qdb, kib, vb, bcf, bsb, of, ob, uf, ub = refs[pos:]
    c = B_CHUNK
    n = t // c
    r32 = lax.broadcasted_iota(jnp.int32, (t, 1), 0) % c

    def prefix(x):
        s = 1
        while s < c:
            x = x + jnp.where(r32 >= s, pltpu.roll(x, s, 0), 0.0)
            s *= 2
        return x

    def suffix(x):
        s = 1
        while s < c:
            x = x + jnp.where(r32 < c - s, pltpu.roll(x, t - s, 0), 0.0)
            s *= 2
        return x

    q = q_ref[...]
    vb[...] = v_ref[...].astype(vb.dtype)
    f = f0_ref[...]
    bc = prefix(jnp.log(f))
    bcf[...] = bc
    qdf[...] = (q * jnp.exp(bc)).astype(qdf.dtype)
    kif[...] = ((1.0 - f) * jnp.exp(-bc)).astype(kif.dtype)
    f = f1_ref[...]
    bs = suffix(jnp.log(f))
    bsb[...] = bs
    qdb[...] = (q * jnp.exp(bs)).astype(qdb.dtype)
    kib[...] = ((1.0 - f) * jnp.exp(-bs)).astype(kib.dtype)

    blk = _largest_tile(t, 256)
    ti = lax.broadcasted_iota(jnp.int32, (blk, blk), 0)
    si = lax.broadcasted_iota(jnp.int32, (blk, blk), 1)
    same = (ti // c) == (si // c)
    lower, upper = same & (ti >= si), same & (ti <= si)
    for bi in range(t // blk):
        rb = slice(bi * blk, (bi + 1) * blk)
        vv = vb[rb, :]
        of[rb, :] = _dot(jnp.where(lower, _dot_nt(qdf[rb, :], kif[rb, :]), 0.0), vv)
        ob[rb, :] = _dot(jnp.where(upper, _dot_nt(qdb[rb, :], kib[rb, :]), 0.0), vv)

    def incr(i, carry):
        r = pl.ds(pl.multiple_of(i * c, c), c)
        vv = vb[r, :]
        bc_c, bs_c = bcf[r, :], bsb[r, :]
        uf[i] = _dot_tn(vv, (1.0 - f0_ref[r, :]) * jnp.exp(bc_c[c - 1:c, :] - bc_c))
        ub[i] = _dot_tn(vv, (1.0 - f1_ref[r, :]) * jnp.exp(bs_c[0:1, :] - bs_c))
        return carry

    lax.fori_loop(0, n, incr, 0, unroll=min(8, n))

    def step(i, carry):
        sf, sb = carry
        rf = pl.ds(pl.multiple_of(i * c, c), c)
        of[rf, :] += _dot_nt(qdf[rf, :], sf)
        sf = sf * jnp.exp(bcf[pl.ds(i * c + c - 1, 1), :]) + uf[i]
        ib = n - 1 - i
        rb = pl.ds(pl.multiple_of(ib * c, c), c)
        ob[rb, :] += _dot_nt(qdb[rb, :], sb)
        sb = sb * jnp.exp(bsb[pl.ds(ib * c, 1), :]) + ub[ib]
        return sf, sb

    if has_s0:
        init = (s0f_ref[...].T, s0b_ref[...].T)
    else:
        init = (jnp.zeros((B_DK, B_DK), F32), jnp.zeros((B_DK, B_DK), F32))
    sf, sb = lax.fori_loop(0, n, step, init, unroll=min(8, n))
    if has_sout:
        sf_ref[...] = sf.T
        sb_ref[...] = sb.T
    o = of[...] + ob[...]
    o_ref[...] = (_rms(o, gn_ref[...]) * _silu(g_ref[...])).astype(o_ref.dtype)


def _gla(q, v, f0, f1, g, gnorm, s0f, s0b, j, rows):
    d = q.shape[1]
    has_s0 = s0f is not None
    if has_s0:
        t, nb, roff = rows.ds, rows.db, rows.n_ctx // rows.ds
    else:
        t, nb, roff = rows.s, rows.b, 0
    dv = d // B_HEADS
    assert dv == LANES and B_DK == LANES
    tok = pl.BlockSpec((t, LANES), lambda b, h: (b + roff, h))
    in_specs = [tok] * 5 + [pl.BlockSpec((None, 1, dv), lambda b, h: (j, 0, 0))]
    args = [q, v, f0, f1, g, gnorm]
    if has_s0:
        sspec = pl.BlockSpec((None, None, None, B_DK, dv), lambda b, h: (b, j, h, 0, 0))
        in_specs += [sspec, sspec]
        args += [s0f, s0b]
    out_shape = [jax.ShapeDtypeStruct((nb * t, d), MXU)]
    out_specs = [pl.BlockSpec((t, LANES), lambda b, h: (b, h))]
    if not has_s0:
        ospec = pl.BlockSpec((None, None, B_DK, dv), lambda b, h: (b, h, 0, 0))
        out_shape += [jax.ShapeDtypeStruct((nb, B_HEADS, B_DK, dv), F32)] * 2
        out_specs += [ospec, ospec]
    scratch = ([pltpu.VMEM((t, LANES), MXU) for _ in range(5)]
               + [pltpu.VMEM((t, LANES), F32) for _ in range(4)]
               + [pltpu.VMEM((t // B_CHUNK, dv, B_DK), F32) for _ in range(2)])
    return pl.pallas_call(
        functools.partial(_gla_kernel, t=t, has_s0=has_s0, has_sout=not has_s0),
        out_shape=tuple(out_shape),
        grid=(nb, B_HEADS),
        in_specs=in_specs,
        out_specs=tuple(out_specs),
        scratch_shapes=scratch,
        compiler_params=_cparams(("parallel", "parallel"), 32),
        name="gla_lat" if has_s0 else "gla_ctx",
    )(*args)


def _attn_c_kernel(*refs, qc, nh, has_cache):
    qn_ref, qp_ref, ckv_ref, kpe_ref = refs[:4]
    pos = 4
    if has_cache:
        ckvc_ref, kpec_ref = refs[pos:pos + 2]
        pos += 2
    wuk_ref, wuv_ref, o_ref = refs[pos:pos + 3]
    scr = refs[pos + 3:]
    sq = qn_ref.shape[0]
    scale = (C_NOPE + C_ROPE) ** -0.5
    wuk = wuk_ref[...].astype(MXU)
    wuv = wuv_ref[...].astype(MXU)

    kw = C_NOPE + 2 * C_ROPE
    vw = C_VD + LANES

    def expand(ckv_r, kpe_r, kcat_r, vcat_r):
        ck = ckv_r[...].astype(MXU)
        kn = jnp.dot(ck, wuk, preferred_element_type=F32).astype(kcat_r.dtype)
        vn = jnp.dot(ck, wuv, preferred_element_type=F32).astype(vcat_r.dtype)
        kp = kpe_r[...].astype(kcat_r.dtype)
        ones = jnp.ones((ck.shape[0], LANES), vcat_r.dtype)
        for hh in range(nh):
            kcat_r[:, hh * kw:(hh + 1) * kw] = jnp.concatenate(
                [kn[:, hh * C_NOPE:(hh + 1) * C_NOPE], kp, kp], axis=1)
            vcat_r[:, hh * vw:(hh + 1) * vw] = jnp.concatenate(
                [vn[:, hh * C_VD:(hh + 1) * C_VD], ones], axis=1)

    groups = [scr[0:2]]
    expand(ckv_ref, kpe_ref, *scr[0:2])
    if has_cache:
        groups.append(scr[2:4])
        expand(ckvc_ref, kpec_ref, *scr[2:4])
    lo = lax.broadcasted_iota(jnp.int32, (1, LANES), 1) < C_ROPE

    def body(c, carry):
        r = pl.ds(pl.multiple_of(c * qc, qc), qc)
        for hh in range(nh):
            cols = slice(hh * C_NOPE, (hh + 1) * C_NOPE)
            qp = qp_ref[r, (hh // 2) * LANES:(hh // 2 + 1) * LANES]
            zero = jnp.zeros_like(qp)
            qph = jnp.where(lo, qp, zero) if hh % 2 == 0 else jnp.where(lo, zero, qp)
            q = jnp.concatenate([qn_ref[r, cols], qph], axis=1)
            acc, _ = _attend(q, [kc[:, hh * kw:(hh + 1) * kw] for kc, _ in groups],
                             [vc[:, hh * vw:(hh + 1) * vw] for _, vc in groups], scale=scale)
            o_ref[r, cols] = (acc[:, :C_VD] / acc[:, C_VD:]).astype(o_ref.dtype)
        return carry

    lax.fori_loop(0, sq // qc, body, 0, unroll=True)


def _attn_c(qn, qp, ckv, kpe, cache_ckv, cache_kpe, wuk, wuv, j, rows):
    has_cache = cache_ckv is not None
    if has_cache:
        t, nb, roff = rows.ds, rows.db, rows.n_ctx // rows.ds
    else:
        t, nb, roff = rows.s, rows.b, 0
    nh = 2 if has_cache else 8
    hw = nh * C_NOPE
    qc = _largest_tile(t, 256)
    in_specs = [pl.BlockSpec((t, hw), lambda b, g: (b + roff, g)),
                pl.BlockSpec((t, nh * C_ROPE), lambda b, g: (b + roff, g)),
                pl.BlockSpec((t, C_KVLORA), lambda b, g: (b + roff, 0)),
                pl.BlockSpec((t, C_ROPE), lambda b, g: (b + roff, 0))]
    args = [qn, qp, ckv, kpe]
    kcw, vcw = nh * (C_NOPE + 2 * C_ROPE), nh * (C_VD + LANES)
    scratch = [pltpu.VMEM((t, kcw), MXU), pltpu.VMEM((t, vcw), MXU)]
    if has_cache:
        p = cache_ckv.shape[2]
        in_specs += [pl.BlockSpec((None, None, p, C_KVLORA), lambda b, g: (b, j, 0, 0)),
                     pl.BlockSpec((None, None, p, C_ROPE), lambda b, g: (b, j, 0, 0))]
        args += [cache_ckv, cache_kpe]
        scratch += [pltpu.VMEM((p, kcw), MXU), pltpu.VMEM((p, vcw), MXU)]
    in_specs += [pl.BlockSpec((None, C_KVLORA, hw), lambda b, g: (j, 0, g))] * 2
    args += [wuk, wuv]
    return pl.pallas_call(
        functools.partial(_attn_c_kernel, qc=qc, nh=nh, has_cache=has_cache),
        out_shape=jax.ShapeDtypeStruct((nb * t, C_HEADS * C_VD), MXU),
        grid=(nb, C_HEADS // nh),
        in_specs=in_specs,
        out_specs=pl.BlockSpec((t, hw), lambda b, g: (b, g)),
        scratch_shapes=scratch,
        compiler_params=_cparams(("parallel", "parallel"), 32),
        name="attn_c_lat" if has_cache else "attn_c_ctx",
    )(*args)


def _attn_d_kernel(*refs, has_cache):
    q_ref, k_ref, v_ref = refs[:3]
    pos = 3
    if has_cache:
        kc_ref, vc_ref = refs[pos:pos + 2]
        pos += 2
    sink_ref, o_ref = refs[pos:pos + 2]
    k2, v2 = refs[pos + 2:pos + 4]
    if has_cache:
        k2c, v2c = refs[pos + 4:pos + 6]
    t = q_ref.shape[0]
    qb = Q_BLOCK
    scale = D_DH ** -0.5
    r = D_HEADS // D_KV_HEADS
    win = min(t, qb + 2 * D_WINDOW)
    lo = lax.broadcasted_iota(jnp.int32, (1, LANES), 1) < D_DH
    rowblk = lax.broadcasted_iota(jnp.int32, (r * qb, 1), 0) // qb

    def stage(k_r, v_r, k2_r, v2_r, g):
        gs = slice(g * D_DH, (g + 1) * D_DH)
        kg, vg = k_r[:, gs].astype(k2_r.dtype), v_r[:, gs].astype(v2_r.dtype)
        k2_r[...] = jnp.concatenate([kg, kg], axis=1)
        v2_r[...] = _with_ones(jnp.concatenate([vg, vg], axis=1))

    for g in range(D_KV_HEADS):
        stage(k_ref, v_ref, k2, v2, g)
        if has_cache:
            stage(kc_ref, vc_ref, k2c, v2c, g)
        sink = jnp.zeros((r * qb, 1), F32)
        for e in range(r):
            h = g * r + e
            sink = jnp.where(rowblk == e, sink_ref[0:1, h:h + 1], sink)

        def body(c, carry, g=g, sink=sink):
            rq = pl.ds(pl.multiple_of(c * qb, qb), qb)
            tiles = []
            for e in range(r):
                cols = slice((g * r + e - e % 2) * D_DH, (g * r + e - e % 2 + 2) * D_DH)
                qp = q_ref[rq, cols] * scale
                zero = jnp.zeros_like(qp)
                tiles.append(jnp.where(lo, qp, zero) if e % 2 == 0 else jnp.where(lo, zero, qp))
            qs = jnp.concatenate(tiles, axis=0)
            if has_cache:
                ws = pl.multiple_of(jnp.clip(c * qb - D_WINDOW, 0, t - win), qb)
                rk = pl.ds(ws, win)
                qpos = c * qb + lax.broadcasted_iota(jnp.int32, (r * qb, 1), 0) % qb
                kpos = ws + lax.broadcasted_iota(jnp.int32, (1, win), 1)
                valid = jnp.abs(qpos - kpos) <= D_WINDOW
                parts = [_dot_nt(qs, k2c[...]), jnp.where(valid, _dot_nt(qs, k2[rk, :]), NEG)]
                vs = [v2c[...], v2[rk, :]]
            else:
                parts = [_dot_nt(qs, k2[...])]
                vs = [v2[...]]
            m = functools.reduce(jnp.maximum, [jnp.max(x, axis=-1, keepdims=True) for x in parts])
            m = jnp.maximum(m, sink)
            acc = functools.reduce(lambda a, b: a + b,
                                   [_dot(jnp.exp(x - m), vx) for x, vx in zip(parts, vs)])
            o = acc[:, :LANES] / (acc[:, LANES:] + jnp.exp(sink - m))
            for pr in range(r // 2):
                cols = slice((g * r + 2 * pr) * D_DH, (g * r + 2 * pr + 2) * D_DH)
                pair = jnp.where(lo, o[2 * pr * qb:(2 * pr + 1) * qb, :],
                                 o[(2 * pr + 1) * qb:(2 * pr + 2) * qb, :])
                o_ref[rq, cols] = pair.astype(o_ref.dtype)
            return carry

        lax.fori_loop(0, t // qb, body, 0, unroll=4)


def _attn_d(q, k, v, cache_k, cache_v, sink, j, rows):
    d = q.shape[1]
    kvw = D_KV_HEADS * D_DH
    has_cache = cache_k is not None
    if has_cache:
        t, nb, roff = rows.ds, rows.db, rows.n_ctx // rows.ds
    else:
        t, nb, roff = rows.s, rows.b, 0
    assert t % Q_BLOCK == 0
    in_specs = [pl.BlockSpec((t, d), lambda b: (b + roff, 0)),
                pl.BlockSpec((t, kvw), lambda b: (b + roff, 0)),
                pl.BlockSpec((t, kvw), lambda b: (b + roff, 0))]
    args = [q, k, v]
    scratch = [pltpu.VMEM((t, LANES), MXU), pltpu.VMEM((t, 2 * LANES), MXU)]
    if has_cache:
        p = cache_k.shape[2]
        cspec = pl.BlockSpec((None, None, p, kvw), lambda b: (b, j, 0, 0))
        in_specs += [cspec, cspec]
        args += [cache_k, cache_v]
        scratch += [pltpu.VMEM((p, LANES), MXU), pltpu.VMEM((p, 2 * LANES), MXU)]
    in_specs.append(pl.BlockSpec((None, 1, D_HEADS), lambda b: (j, 0, 0)))
    args.append(sink)
    return pl.pallas_call(
        functools.partial(_attn_d_kernel, has_cache=has_cache),
        out_shape=jax.ShapeDtypeStruct((nb * t, d), MXU),
        grid=(nb,),
        in_specs=in_specs,
        out_specs=pl.BlockSpec((t, d), lambda b: (b, 0)),
        scratch_shapes=scratch,
        compiler_params=_cparams(("parallel",), 32),
        name="attn_d_lat" if has_cache else "attn_d_ctx",
    )(*args)


def kernel(x_prompt, x_sample, cache_a_k, cache_a_v, state_b_fwd, state_b_bwd, cache_c_ckv,
           cache_c_kpe, cache_d_k, cache_d_v, c, c_ctx, ada_w, ada_b, norm_mix_pre,
           norm_mix_post, norm_ffn_pre, norm_ffn_post, a_wq, a_wk, a_wv, a_wo, a_lambda,
           a_subln, b_wq, b_wi, b_wf, b_lower, b_wg, b_gnorm, b_wo, c_wdq, c_qnorm, c_wuq,
           c_wdkv, c_kvnorm, c_wuk, c_wuv, c_wo, d_wq, d_wk, d_wv, d_sink, d_wo, ffn_wg,
           ffn_wu, ffn_wd):
    b, s, d = x_prompt.shape
    db, ds, _ = x_sample.shape
    depth = ada_w.shape[0]
    rows = _Rows(b, s, db, ds, _largest_tile(math.gcd(b * s, ds), 1024))
    n_ctx = rows.n_ctx

    cond = jnp.zeros((rows.crows, d), F32).at[:db].set(c).at[db].set(c_ctx)
    mod = _ada(cond, ada_w, ada_b)
    g3 = lambda a: a.reshape(a.shape[0], 1, a.shape[1])
    n_mix_pre, n_mix_post = g3(norm_mix_pre), g3(norm_mix_post)
    n_ffn_pre, n_ffn_post = g3(norm_ffn_pre), g3(norm_ffn_post)
    tabs = _rope_tables(ds)
    rope_epi = lambda tn: (_make_epi_rope(rows, tn), _rope_extra(tabs, rows))

    x, h = _prep(x_prompt.reshape(n_ctx, d), x_sample.reshape(rows.n_lat, d),
                 n_mix_pre, mod, 0, rows)

    flat4 = lambda a: a.reshape(a.shape[:3] + (-1,))
    wo_b = {0: _cast_mxu(a_wo), 1: _cast_mxu(b_wo), 2: _cast_mxu(c_wo), 3: _cast_mxu(d_wo)}
    wd_b = _cast_mxu(ffn_wd)
    wf = b_wf.reshape((-1,) + b_wf.shape[2:])
    outs = {k_: [] for k_ in ("a_k", "a_v", "b_f", "b_b", "c_ckv", "c_kpe", "d_k", "d_v")}
    for i in range(depth):
        m, j = i % N_MIXERS, i // N_MIXERS
        if m == 0:
            tn = _largest_tile(a_wq.shape[2], 1024)
            epi, ext = rope_epi(tn)
            q, = _proj(h, [(a_wq, j)], epi, [(tn, MXU)], rows, tn, ext, name="a_q")
            kv_outs = [(tn, MXU), (tn, F32, "ctx")]
            k, kc = _proj(h, [(a_wk, j)], _with_ctx_copy(epi, rows), kv_outs, rows, tn, ext,
                          name="a_k")
            v, vc = _proj(h, [(a_wv, j)], _with_ctx_copy(_epi_store, rows), kv_outs, rows, tn,
                          name="a_v")
            o = (_attn_a(q, k, v, None, None, a_lambda, g3(a_subln), i, j, rows),
                 _attn_a(q, k, v, flat4(cache_a_k), flat4(cache_a_v), a_lambda, g3(a_subln),
                         i, j, rows))
            outs["a_k"].append(kc.reshape(b, s, A_HEADS, 2 * A_DH))
            outs["a_v"].append(vc.reshape(b, s, A_HEADS, 2 * A_DH))
        elif m == 1:
            tn = _largest_tile(b_wq.shape[2], 1024)
            q, = _proj(h, [(b_wq, j)], _epi_silu, [(tn, F32)], rows, tn, nsub=2, name="b_q")
            vi, = _proj(h, [(b_wi, j)], _epi_store, [(tn, MXU)], rows, tn, nsub=2, name="b_i")
            g, = _proj(h, [(b_wg, j)], _epi_store, [(tn, F32)], rows, tn, nsub=2, name="b_g")
            fs = []
            for dr in range(2):
                bl_spec = pl.BlockSpec((None, depth, tn), lambda jn, mm, dr=dr: (dr, 0, jn))
                f, = _proj(h, [(wf, 2 * j + dr)], _make_epi_forget(i, depth), [(tn, F32)], rows,
                           tn, [(jnp.swapaxes(b_lower, 0, 1), bl_spec)], nsub=2,
                           name="b_f%d" % dr)
                fs.append(f)
            oc, sf, sb = _gla(q, vi, fs[0], fs[1], g, g3(b_gnorm), None, None, j, rows)
            ol, = _gla(q, vi, fs[0], fs[1], g, g3(b_gnorm), state_b_fwd, state_b_bwd, j, rows)
            o = (oc, ol)
            outs["b_f"].append(sf)
            outs["b_b"].append(sb)
        elif m == 2:
            nq = c_wdq.shape[2]
            cq, = _proj(h, [(c_wdq, j)], _make_epi_rmsnorm(), [(nq, MXU)], rows, nq,
                        [(g3(c_qnorm), pl.BlockSpec((None, 1, nq), lambda jn, mm: (j, 0, 0)))],
                        name="c_dq")
            nkv = c_wdkv.shape[2]
            ext = [(g3(c_kvnorm), pl.BlockSpec((None, 1, C_KVLORA), lambda jn, mm: (j, 0, 0)))]
            ext += _rope_extra(tabs, rows)
            ckv, kpe, ckv_c, kpe_c = _proj(
                h, [(c_wdkv, j)], _make_epi_ckv(rows),
                [(C_KVLORA, MXU), (C_ROPE, MXU), (C_KVLORA, F32, "ctx"), (C_ROPE, F32, "ctx")],
                rows, nkv, ext, name="c_dkv")
            wuq = c_wuq.reshape(c_wuq.shape[0], nq, C_HEADS, C_NOPE + C_ROPE)
            wuq_n = wuq[:, :, :, :C_NOPE].reshape(-1, nq, C_HEADS * C_NOPE)
            wuq_p = wuq[:, :, :, C_NOPE:].reshape(-1, nq, C_HEADS * C_ROPE)
            tn = _largest_tile(wuq_n.shape[2], 1024)
            qn, = _proj(cq, [(wuq_n, j)], _epi_store, [(tn, MXU)], rows, tn, name="c_qn")
            tnp = wuq_p.shape[2]
            epi, ext = rope_epi(tnp)
            qp, = _proj(cq, [(wuq_p, j)], epi, [(tnp, MXU)], rows, tnp, ext, name="c_qp")
            o = (_attn_c(qn, qp, ckv, kpe, None, None, c_wuk, c_wuv, j, rows),
                 _attn_c(qn, qp, ckv, kpe, cache_c_ckv, cache_c_kpe, c_wuk, c_wuv, j, rows))
            outs["c_ckv"].append(ckv_c.reshape(b, s, C_KVLORA))
            outs["c_kpe"].append(kpe_c.reshape(b, s, C_ROPE))
        else:
            tn = _largest_tile(d_wq.shape[2], 1024)
            epi, ext = rope_epi(tn)
            q, = _proj(h, [(d_wq, j)], epi, [(tn, MXU)], rows, tn, ext, name="d_q")
            kvw = d_wk.shape[2]
            epi, ext = rope_epi(kvw)
            kv_outs = [(kvw, MXU), (kvw, F32, "ctx")]
            k, kc = _proj(h, [(d_wk, j)], _with_ctx_copy(epi, rows), kv_outs, rows, kvw, ext,
                          name="d_k")
            v, vc = _proj(h, [(d_wv, j)], _with_ctx_copy(_epi_store, rows), kv_outs, rows, kvw,
                          name="d_v")
            o = (_attn_d(q, k, v, None, None, g3(d_sink), j, rows),
                 _attn_d(q, k, v, flat4(cache_d_k), flat4(cache_d_v), g3(d_sink), j, rows))
            outs["d_k"].append(kc.reshape(b, s, D_KV_HEADS, D_DH))
            outs["d_v"].append(vc.reshape(b, s, D_KV_HEADS, D_DH))

        x, h = _resid(o, wo_b[m], j, x, mod, i, 2, n_mix_post, rows, nxt=(i, 3, n_ffn_pre),
                      nsub=2, name="mix_out")
        tf = _largest_tile(ffn_wg.shape[2], 512)
        a, = _proj(h, [(ffn_wg, i), (ffn_wu, i)], _epi_swiglu, [(tf, MXU)], rows, tf,
                   tm=_largest_tile(rows.mt, 2048), nsub=2, name="ffn_gu")
        nxt = (i + 1, 0, n_mix_pre) if i + 1 < depth else None
        x, h = _resid(a, wd_b, i, x, mod, i, 5, n_ffn_post, rows, nxt=nxt, name="ffn_down")

    stack = lambda lst: jnp.stack(lst, axis=1)
    return (x.reshape(b, s, d), h.reshape(db, ds, d),
            stack(outs["a_k"]), stack(outs["a_v"]), stack(outs["b_f"]), stack(outs["b_b"]),
            stack(outs["c_ckv"]), stack(outs["c_kpe"]), stack(outs["d_k"]), stack(outs["d_v"]))
```

```python
import functools
import math

import numpy as np
import jax
import jax.numpy as jnp
from jax import lax
from jax.experimental import pallas as pl
from jax.experimental.pallas import tpu as pltpu

EPS = 1e-6
ROPE_BASE = 10000.0
GRID_W = 64
NEG = -1e30
N_MIXERS = 4
A_HEADS, A_DH = 16, 64
B_HEADS, B_DK, B_CHUNK = 16, 128, 32
C_HEADS, C_NOPE, C_ROPE, C_VD = 16, 128, 64, 128
D_HEADS, D_KV_HEADS, D_DH, D_WINDOW, Q_BLOCK = 32, 4, 64, 128, 128

LANES = 128
SUBLANES = 8
VMEM_CAP_MIB = 56

MXU = jnp.bfloat16
F32 = jnp.float32


def _cparams(sem, vmem_mib):
    return pltpu.CompilerParams(dimension_semantics=sem,
                                vmem_limit_bytes=min(vmem_mib, VMEM_CAP_MIB) << 20)


def _dot(a, b):
    return jnp.dot(a.astype(MXU), b.astype(MXU), preferred_element_type=F32)


def _dot_nt(a, b):
    return lax.dot_general(a.astype(MXU), b.astype(MXU), (((1,), (1,)), ((), ())),
                           preferred_element_type=F32)


def _dot_tn(a, b):
    return lax.dot_general(a.astype(MXU), b.astype(MXU), (((0,), (0,)), ((), ())),
                           preferred_element_type=F32)


def _sigmoid(x):
    return 1.0 / (1.0 + jnp.exp(-x))


def _silu(x):
    return x * _sigmoid(x)


def _rms(x, g):
    ms = jnp.mean(x * x, axis=-1, keepdims=True)
    return x * lax.rsqrt(ms + EPS) * g


def _largest_tile(n, pref):
    t = min(n, pref)
    while n % t:
        t -= SUBLANES
    assert t > 0 and n % t == 0
    return t


class _Rows:
    def __init__(self, b, s, db, ds, tm):
        self.b, self.s, self.db, self.ds = b, s, db, ds
        self.n_ctx, self.n_lat = b * s, db * ds
        self.mt = self.n_ctx + self.n_lat
        self.tm = tm
        assert self.n_ctx % tm == 0 and ds % tm == 0 and self.n_ctx % ds == 0
        self.nct = self.n_ctx // tm
        self.ntiles = self.mt // tm
        self.crows = -(-(db + 1) // SUBLANES) * SUBLANES

    def cond_row(self, i):
        lat = ((i - self.nct) * self.tm) // self.ds
        return jnp.where(i < self.nct, self.db, lat)

    def pos_block(self, i):
        return (jnp.maximum(i - self.nct, 0) * self.tm % self.ds) // self.tm


def _rope_tables(n_tok):
    nf = 16
    inv = (ROPE_BASE ** (-np.arange(nf, dtype=np.float32) / nf)).astype(np.float32)
    t = np.arange(n_tok)
    row, col = (t // GRID_W).astype(np.float32), (t % GRID_W).astype(np.float32)
    lane = np.arange(64)
    pos = np.where(lane[None, :] < 32, row[:, None], col[:, None]).astype(np.float32)
    ang = (pos * inv[lane % nf][None, :]).astype(np.float32)
    cos, sin = np.cos(ang).astype(np.float32), np.sin(ang).astype(np.float32)
    first = (lane % 32) < 16
    sa = np.where(first[None, :], -sin, 0.0).astype(np.float32)
    sb = np.where(first[None, :], 0.0, sin).astype(np.float32)
    tile2 = lambda a: np.concatenate([a, a], axis=1)
    return jnp.asarray(tile2(cos)), jnp.asarray(tile2(sa)), jnp.asarray(tile2(sb))


def _rope128(y, cos, sa, sb):
    return (y * cos + pltpu.roll(y, LANES - 16, 1) * sa + pltpu.roll(y, 16, 1) * sb)


def _ada_kernel(c_ref, w_ref, b_ref, o_ref):
    o_ref[...] = _dot(_silu(c_ref[...]), w_ref[...]) + b_ref[...]


def _ada(cond, ada_w, ada_b):
    depth, d, n = ada_w.shape
    cr = cond.shape[0]
    tn = _largest_tile(n, 1024)
    return pl.pallas_call(
        _ada_kernel,
        out_shape=jax.ShapeDtypeStruct((depth, cr, n), F32),
        grid=(depth, n // tn),
        in_specs=[pl.BlockSpec((cr, d), lambda l, j: (0, 0)),
                  pl.BlockSpec((None, d, tn), lambda l, j: (l, 0, j)),
                  pl.BlockSpec((None, 1, tn), lambda l, j: (l, 0, j))],
        out_specs=pl.BlockSpec((None, cr, tn), lambda l, j: (l, 0, j)),
        compiler_params=_cparams(("parallel", "parallel"), 40),
        name="ada_mod",
    )(cond, ada_w, ada_b.reshape(depth, 1, n))


def _prep_kernel(xp_ref, xs_ref, g_ref, mod_ref, h_ref, *, rows, d):
    i = pl.program_id(0)
    x = jnp.where(i < rows.nct, xp_ref[...], xs_ref[...])
    cr = rows.cond_row(i)
    shift = mod_ref[pl.ds(cr, 1), pl.ds(0, d)]
    scale = mod_ref[pl.ds(cr, 1), pl.ds(d, d)]
    h_ref[...] = (_rms(x, g_ref[...]) * (1.0 + scale) + shift).astype(h_ref.dtype)


def _prep(xp, xs, g, mod, layer, rows):
    d = xp.shape[1]
    rows = _Rows(rows.b, rows.s, rows.db, rows.ds, _largest_tile(rows.tm, 512))
    tm, nct = rows.tm, rows.nct
    return pl.pallas_call(
        functools.partial(_prep_kernel, rows=rows, d=d),
        out_shape=jax.ShapeDtypeStruct((rows.mt, d), MXU),
        grid=(rows.ntiles,),
        in_specs=[pl.BlockSpec((tm, d), lambda i: (jnp.minimum(i, nct - 1), 0)),
                  pl.BlockSpec((tm, d), lambda i: (jnp.maximum(i - nct, 0), 0)),
                  pl.BlockSpec((None, 1, d), lambda i: (layer, 0, 0)),
                  pl.BlockSpec((None, rows.crows, mod.shape[2]), lambda i: (layer, 0, 0))],
        out_specs=pl.BlockSpec((tm, d), lambda i: (i, 0)),
        compiler_params=_cparams(("parallel",), 32 * tm * d // (1 << 20) + 8),
        name="prep",
    )(xp, xs, g, mod)


def _proj_kernel(*refs, n_w, n_extra, n_out, nsub, epi):
    h_ref = refs[0]
    w_refs = refs[1:1 + n_w]
    extra = refs[1 + n_w:1 + n_w + n_extra]
    outs = refs[1 + n_w + n_extra:1 + n_w + n_extra + n_out]
    wb_refs = refs[1 + n_w + n_extra + n_out:]
    m = pl.program_id(1)

    @pl.when(m == 0)
    def _():
        for w_ref, wb_ref in zip(w_refs, wb_refs):
            wb_ref[...] = w_ref[...].astype(wb_ref.dtype)

    rs = h_ref.shape[0] // nsub
    for s in range(nsub):
        r = slice(s * rs, (s + 1) * rs)
        h = h_ref[r, :]
        ys = [jnp.dot(h, wb_ref[...], preferred_element_type=F32) for wb_ref in wb_refs]
        epi(ys, m, extra, outs, r)


def _proj(h, ws, epi, outs, rows, tn, extra=(), tm=None, nsub=1, name="proj"):
    mt, k = h.shape
    n = ws[0][0].shape[2]
    assert n % tn == 0
    tm = rows.tm if tm is None else tm
    assert mt % tm == 0
    nj = n // tn
    in_specs = [pl.BlockSpec((tm, k), lambda j, m: (m, 0))]
    in_specs += [pl.BlockSpec((None, k, tn), lambda j, m, li=li: (li, 0, j)) for _, li in ws]
    in_specs += [s for _, s in extra]
    nct = rows.nct
    ctx_only = [len(o) > 2 and o[2] == "ctx" for o in outs]
    outs = [o[:2] for o in outs]
    out_shape = tuple(jax.ShapeDtypeStruct((rows.n_ctx if co else mt, c * nj), dt)
                      for (c, dt), co in zip(outs, ctx_only))
    out_specs = tuple(pl.BlockSpec((tm, c), (lambda j, m: (jnp.minimum(m, nct - 1), j)) if co
                                   else (lambda j, m: (m, j)))
                      for (c, _), co in zip(outs, ctx_only))
    assert not any(ctx_only) or tm == rows.tm
    out_bytes = sum(tm * c * jnp.dtype(dt).itemsize for c, dt in outs)
    vmem = (2 * tm * k * 2 + len(ws) * (2 * k * tn * 4 + k * tn * 2) + 2 * out_bytes
            + len(ws) * tm * tn * 8) // (1 << 20) + 6
    res = pl.pallas_call(
        functools.partial(_proj_kernel, n_w=len(ws), n_extra=len(extra), n_out=len(outs),
                          nsub=nsub, epi=epi),
        out_shape=out_shape,
        grid=(nj, mt // tm),
        in_specs=in_specs,
        out_specs=out_specs,
        scratch_shapes=[pltpu.VMEM((k, tn), MXU) for _ in ws],
        compiler_params=_cparams(("parallel", "arbitrary"), vmem),
        name=name,
    )(h, *[w for w, _ in ws], *[a for a, _ in extra])
    return res


def _epi_store(ys, m, extra, outs, r):
    outs[0][r, :] = ys[0].astype(outs[0].dtype)


def _epi_silu(ys, m, extra, outs, r):
    outs[0][r, :] = _silu(ys[0]).astype(outs[0].dtype)


def _epi_swiglu(ys, m, extra, outs, r):
    outs[0][r, :] = (_silu(ys[0]) * ys[1]).astype(outs[0].dtype)


def _make_epi_rope(rows, tn):
    def epi(ys, m, extra, outs, r):
        cos_ref, sa_ref, sb_ref = extra
        y, o_ref = ys[0], outs[0]

        @pl.when(m < rows.nct)
        def _():
            o_ref[r, :] = y.astype(o_ref.dtype)

        @pl.when(m >= rows.nct)
        def _():
            cos, sa, sb = cos_ref[r, :], sa_ref[r, :], sb_ref[r, :]
            for c in range(tn // LANES):
                sl = slice(c * LANES, (c + 1) * LANES)
                o_ref[r, sl] = _rope128(y[:, sl], cos, sa, sb).astype(o_ref.dtype)
    return epi


def _with_ctx_copy(epi, rows):
    def wrapped(ys, m, extra, outs, r):
        epi(ys, m, extra, outs[:-1], r)

        @pl.when(m < rows.nct)
        def _():
            outs[-1][r, :] = ys[0]
    return wrapped


def _rope_extra(tabs, rows):
    tm = rows.tm
    spec = pl.BlockSpec((tm, LANES), lambda j, m: (rows.pos_block(m), 0))
    return [(t, spec) for t in tabs]


def _make_epi_forget(layer_idx, depth):
    def epi(ys, m, extra, outs, r):
        bl = extra[0][...]
        e = jnp.exp(bl - jnp.max(bl, axis=0, keepdims=True))
        p = e / jnp.sum(e, axis=0, keepdims=True)
        lb = jnp.sum(p[0:layer_idx + 1], axis=0, keepdims=True) - p[0:1]
        outs[0][r, :] = lb + (1.0 - lb) * _sigmoid(ys[0])
    return epi


def _make_epi_rmsnorm():
    def epi(ys, m, extra, outs, r):
        outs[0][r, :] = _rms(ys[0], extra[0][...]).astype(outs[0].dtype)
    return epi


def _make_epi_ckv(rows):
    def epi(ys, m, extra, outs, r):
        g_ref, cos_ref, sa_ref, sb_ref = extra
        ckv_all, kpe_all, ckv_ctx, kpe_ctx = outs
        y = ys[0]
        ckv = _rms(y[:, :C_KVLORA], g_ref[...])
        ckv_all[r, :] = ckv.astype(ckv_all.dtype)
        kpe = y[:, C_KVLORA:C_KVLORA + C_ROPE]

        @pl.when(m < rows.nct)
        def _():
            kpe_all[r, :] = kpe.astype(kpe_all.dtype)
            ckv_ctx[r, :] = ckv
            kpe_ctx[r, :] = kpe

        @pl.when(m >= rows.nct)
        def _():
            k2 = jnp.concatenate([kpe, kpe], axis=1)
            kr = _rope128(k2, cos_ref[r, :], sa_ref[r, :], sb_ref[r, :])
            kpe_all[r, :] = kr[:, :C_ROPE].astype(kpe_all.dtype)
    return epi


C_KVLORA = 256


def _cast_kernel(w_ref, o_ref):
    o_ref[...] = w_ref[...].astype(o_ref.dtype)


def _cast_mxu(w):
    l, k, n = w.shape
    tr = _largest_tile(k, 512)
    return pl.pallas_call(
        _cast_kernel,
        out_shape=jax.ShapeDtypeStruct(w.shape, MXU),
        grid=(l, k // tr),
        in_specs=[pl.BlockSpec((None, tr, n), lambda i, r: (i, r, 0))],
        out_specs=pl.BlockSpec((None, tr, n), lambda i, r: (i, r, 0)),
        compiler_params=_cparams(("parallel", "parallel"), 12 * tr * n // (1 << 20) + 8),
        name="cast_w",
    )(w)


def _resid_kernel(*refs, rows, d, nsub, gate_chunk, next_chunk, split_a, split_x):
    refs = list(refs)
    a_refs = [refs.pop(0) for _ in range(2 if split_a else 1)]
    w_ref = refs.pop(0)
    x_refs = [refs.pop(0) for _ in range(2 if split_x else 1)]
    modc_ref, gpost_ref = refs.pop(0), refs.pop(0)
    if next_chunk is not None:
        modn_ref, gpre_ref = refs.pop(0), refs.pop(0)
        xo_ref, ho_ref = refs
    else:
        xo_ref, xl_ref = refs
    i = pl.program_id(0)

    def rows_of(parts, r):
        if len(parts) == 1:
            return parts[0][r, :]
        return jnp.where(i < rows.nct, parts[0][r, :], parts[1][r, :])

    cr = rows.cond_row(i)
    gate = modc_ref[pl.ds(cr, 1), pl.ds(gate_chunk * d, d)] * gpost_ref[...]
    if next_chunk is not None:
        shift = modn_ref[pl.ds(cr, 1), pl.ds(next_chunk * d, d)]
        scale = (1.0 + modn_ref[pl.ds(cr, 1), pl.ds((next_chunk + 1) * d, d)]) * gpre_ref[...]
    w = w_ref[...]
    rs = a_refs[0].shape[0] // nsub
    slabs = [slice(s * rs, (s + 1) * rs) for s in range(nsub)]
    ys = [jnp.dot(rows_of(a_refs, r), w, preferred_element_type=F32) for r in slabs]
    for r, y in zip(slabs, ys):
        yn = y * lax.rsqrt(jnp.mean(y * y, axis=-1, keepdims=True) + EPS)
        xn = rows_of(x_refs, r) + yn * gate
        if next_chunk is not None:
            xo_ref[r, :] = xn
            hn = xn * lax.rsqrt(jnp.mean(xn * xn, axis=-1, keepdims=True) + EPS)
            ho_ref[r, :] = (hn * scale + shift).astype(ho_ref.dtype)
        else:
            @pl.when(i < rows.nct)
            def _(xn=xn, r=r):
                xo_ref[r, :] = xn

            @pl.when(i >= rows.nct)
            def _(xn=xn, r=r):
                xl_ref[r, :] = xn


def _resid(a, w, li, x, mod, layer, gate_chunk, gpost, rows, nxt=None, nsub=1, name="resid"):
    split_a = isinstance(a, (tuple, list))
    a_parts = list(a) if split_a else [a]
    mt, kdim = rows.mt, a_parts[0].shape[1]
    d = w.shape[2]
    n_x = 2 if isinstance(x, (tuple, list)) else 1
    row_bytes = len(a_parts) * 2 * kdim * 2 + d * (8 * n_x + 8 + 4 + 12)
    budget = (VMEM_CAP_MIB - 6 << 20) - kdim * d * 2
    tm = _largest_tile(rows.tm, max(SUBLANES, 1 << int(math.log2(budget // row_bytes))))
    sub = _Rows(rows.b, rows.s, rows.db, rows.ds, tm)
    nct = sub.nct
    mspec = lambda l: pl.BlockSpec((None, rows.crows, mod.shape[2]), lambda i: (l, 0, 0))
    gspec = lambda l: pl.BlockSpec((None, 1, d), lambda i: (l, 0, 0))
    def row_specs(width, split):
        if split:
            return [pl.BlockSpec((tm, width), lambda i: (jnp.minimum(i, nct - 1), 0)),
                    pl.BlockSpec((tm, width), lambda i: (jnp.maximum(i - nct, 0), 0))]
        return [pl.BlockSpec((tm, width), lambda i: (i, 0))]

    split_x = isinstance(x, (tuple, list))
    x_parts = list(x) if split_x else [x]
    in_specs = row_specs(kdim, split_a) + [
        pl.BlockSpec((None, kdim, d), lambda i: (li, 0, 0), pipeline_mode=pl.Buffered(1))
    ] + row_specs(d, split_x) + [mspec(layer), gspec(layer)]
    args = a_parts + [w] + x_parts + [mod, gpost]
    if nxt is not None:
        nl, nchunk, gpre = nxt
        in_specs += [mspec(nl), gspec(nl)]
        args += [mod, gpre]
        out_shape = [jax.ShapeDtypeStruct((mt, d), F32), jax.ShapeDtypeStruct((mt, d), MXU)]
        out_specs = [pl.BlockSpec((tm, d), lambda i: (i, 0))] * 2
    else:
        nchunk = None
        out_shape = [jax.ShapeDtypeStruct((rows.n_ctx, d), F32),
                     jax.ShapeDtypeStruct((rows.n_lat, d), F32)]
        out_specs = [pl.BlockSpec((tm, d), lambda i: (jnp.minimum(i, nct - 1), 0)),
                     pl.BlockSpec((tm, d), lambda i: (jnp.maximum(i - nct, 0), 0))]
    vmem = (kdim * d * 2 + tm * row_bytes) // (1 << 20) + 8
    res = pl.pallas_call(
        functools.partial(_resid_kernel, rows=sub, d=d, nsub=nsub, gate_chunk=gate_chunk,
                          next_chunk=nchunk, split_a=split_a, split_x=split_x),
        out_shape=tuple(out_shape),
        grid=(mt // tm,),
        in_specs=in_specs,
        out_specs=tuple(out_specs),
        compiler_params=_cparams(("arbitrary",), vmem),
        name=name,
    )(*args)
    return res


def _with_ones(v):
    return jnp.concatenate([v, jnp.ones((v.shape[0], LANES), v.dtype)], axis=1)


def _attend(q, keys, vaugs, scale=None, masks=None):
    parts = [_dot_nt(q, kk) for kk in keys]
    if masks is not None:
        parts = [s if mk is None else jnp.where(mk, s, NEG) for s, mk in zip(parts, masks)]
    if scale is not None:
        parts = [s * scale for s in parts]
    m = functools.reduce(jnp.maximum, [jnp.max(s, axis=-1, keepdims=True) for s in parts])
    acc = functools.reduce(lambda a, b: a + b,
                           [_dot(jnp.exp(s - m), va) for s, va in zip(parts, vaugs)])
    return acc, m


def _attn_a_kernel(*refs, hb, qc, lam_init, has_cache):
    q_ref, k_ref, v_ref = refs[:3]
    pos = 3
    if has_cache:
        kc_ref, vc_ref = refs[pos:pos + 2]
        pos += 2
    lam_ref, sub_ref, o_ref = refs[pos:pos + 3]
    sq = q_ref.shape[0]
    dv = 2 * A_DH
    scale = A_DH ** -0.5
    lp = lam_ref[...]
    lam = (jnp.exp(jnp.sum(lp[0:1] * lp[1:2], axis=1, keepdims=True))
           - jnp.exp(jnp.sum(lp[2:3] * lp[3:4], axis=1, keepdims=True)) + lam_init)
    lo = lax.broadcasted_iota(jnp.int32, (1, dv), 1) < A_DH
    sub = sub_ref[...] * (1.0 - lam_init)
    for h in range(hb):
        cols = slice(h * dv, (h + 1) * dv)
        keys = [k_ref[:, cols].astype(MXU)]
        vals = [_with_ones(v_ref[:, cols].astype(MXU))]
        if has_cache:
            keys.append(kc_ref[:, cols].astype(MXU))
            vals.append(_with_ones(vc_ref[:, cols].astype(MXU)))

        def body(c, carry, cols=cols, keys=keys, vals=vals):
            r = pl.ds(pl.multiple_of(c * qc, qc), qc)
            q = q_ref[r, cols] * scale
            zero = jnp.zeros_like(q)
            a1, _ = _attend(jnp.where(lo, q, zero), keys, vals)
            a2, _ = _attend(jnp.where(lo, zero, q), keys, vals)
            o = a1[:, :dv] / a1[:, dv:] - lam * (a2[:, :dv] / a2[:, dv:])
            ms = jnp.mean(o * o, axis=-1, keepdims=True)
            o_ref[r, cols] = (o * lax.rsqrt(ms + EPS) * sub).astype(o_ref.dtype)
            return carry

        lax.fori_loop(0, sq // qc, body, 0, unroll=True)


def _attn_a(q, k, v, cache_k, cache_v, lam_p, subln, layer_idx, j, rows):
    d = q.shape[1]
    lam_init = 0.8 - 0.6 * math.exp(-0.3 * layer_idx)
    has_cache = cache_k is not None
    if has_cache:
        t, nb, roff = rows.ds, rows.db, rows.n_ctx // rows.ds
        hb = 4
    else:
        t, nb, roff = rows.s, rows.b, 0
        hb = A_HEADS
    cw = hb * 2 * A_DH
    qc = _largest_tile(t, 256)
    tok = lambda b, g: (b + roff, g)
    in_specs = [pl.BlockSpec((t, cw), tok)] * 3
    args = [q, k, v]
    if has_cache:
        p = cache_k.shape[2]
        cspec = pl.BlockSpec((None, None, p, cw), lambda b, g: (b, j, 0, g))
        in_specs += [cspec, cspec]
        args += [cache_k, cache_v]
    in_specs += [pl.BlockSpec((None, 4, A_DH), lambda b, g: (j, 0, 0)),
                 pl.BlockSpec((None, 1, 2 * A_DH), lambda b, g: (j, 0, 0))]
    args += [lam_p, subln]
    return pl.pallas_call(
        functools.partial(_attn_a_kernel, hb=hb, qc=qc, lam_init=lam_init, has_cache=has_cache),
        out_shape=jax.ShapeDtypeStruct((nb * t, d), MXU),
        grid=(nb, d // cw),
        in_specs=in_specs,
        out_specs=pl.BlockSpec((t, cw), lambda b, g: (b, g)),
        compiler_params=_cparams(("parallel", "parallel"), 40),
        name="attn_a_lat" if has_cache else "attn_a_ctx",
    )(*args)


def _gla_kernel(*refs, t, nhb, has_s0, has_sout):
    q_ref, v_ref, f0_ref, f1_ref, g_ref, gn_ref = refs[:6]
    pos = 6
    if has_s0:
        s0f_ref, s0b_ref = refs[pos:pos + 2]
        pos += 2
    o_ref = refs[pos]
    pos += 1
    if has_sout:
        sf_ref, sb_ref = refs[pos:pos + 2]
        pos += 2
    qdf, kif, qdb, kib, vb, bcf, bsb, of, ob, uf, ub, stf, stb = refs[pos:]
    c = B_CHUNK
    n = t // c
    heads = [slice(hi * LANES, (hi + 1) * LANES) for hi in range(nhb)]
    r32 = lax.broadcasted_iota(jnp.int32, (t, 1), 0) % c

    def prefix(x):
        s = 1
        while s < c:
            x = x + jnp.where(r32 >= s, pltpu.roll(x, s, 0), 0.0)
            s *= 2
        return x

    def suffix(x):
        s = 1
        while s < c:
            x = x + jnp.where(r32 < c - s, pltpu.roll(x, t - s, 0), 0.0)
            s *= 2
        return x

    q = q_ref[...]
    vb[...] = v_ref[...].astype(vb.dtype)
    f = f0_ref[...]
    bc = prefix(jnp.log(f))
    bcf[...] = bc
    qdf[...] = (q * jnp.exp(bc)).astype(qdf.dtype)
    kif[...] = ((1.0 - f) * jnp.exp(-bc)).astype(kif.dtype)
    f = f1_ref[...]
    bs = suffix(jnp.log(f))
    bsb[...] = bs
    qdb[...] = (q * jnp.exp(bs)).astype(qdb.dtype)
    kib[...] = ((1.0 - f) * jnp.exp(-bs)).astype(kib.dtype)

    blk = _largest_tile(t, 256)
    ti = lax.broadcasted_iota(jnp.int32, (blk, blk), 0)
    si = lax.broadcasted_iota(jnp.int32, (blk, blk), 1)
    same = (ti // c) == (si // c)
    lower, upper = same & (ti >= si), same & (ti <= si)
    for bi in range(t // blk):
        rb = slice(bi * blk, (bi + 1) * blk)
        for hc in heads:
            vv = vb[rb, hc]
            of[rb, hc] = _dot(jnp.where(lower, _dot_nt(qdf[rb, hc], kif[rb, hc]), 0.0), vv)
            ob[rb, hc] = _dot(jnp.where(upper, _dot_nt(qdb[rb, hc], kib[rb, hc]), 0.0), vv)

    def incr(i, carry):
        r = pl.ds(pl.multiple_of(i * c, c), c)
        vv = vb[r, :]
        bc_c, bs_c = bcf[r, :], bsb[r, :]
        kef = (1.0 - f0_ref[r, :]) * jnp.exp(bc_c[c - 1:c, :] - bc_c)
        keb = (1.0 - f1_ref[r, :]) * jnp.exp(bs_c[0:1, :] - bs_c)
        for hi, hc in enumerate(heads):
            uf[hi, i] = _dot_tn(vv[:, hc], kef[:, hc])
            ub[hi, i] = _dot_tn(vv[:, hc], keb[:, hc])
        return carry

    lax.fori_loop(0, n, incr, 0, unroll=min(8, n))

    for hi in range(nhb):
        if has_s0:
            stf[hi] = s0f_ref[hi].T
            stb[hi] = s0b_ref[hi].T
        else:
            stf[hi] = jnp.zeros((B_DK, B_DK), F32)
            stb[hi] = jnp.zeros((B_DK, B_DK), F32)

    def step(i, carry):
        rf = pl.ds(pl.multiple_of(i * c, c), c)
        ib = n - 1 - i
        rb = pl.ds(pl.multiple_of(ib * c, c), c)
        decf = jnp.exp(bcf[pl.ds(i * c + c - 1, 1), :])
        decb = jnp.exp(bsb[pl.ds(ib * c, 1), :])
        for hi, hc in enumerate(heads):
            sf = stf[hi]
            of[rf, hc] += _dot_nt(qdf[rf, hc], sf)
            stf[hi] = sf * decf[:, hc] + uf[hi, i]
            sb = stb[hi]
            ob[rb, hc] += _dot_nt(qdb[rb, hc], sb)
            stb[hi] = sb * decb[:, hc] + ub[hi, ib]
        return carry

    lax.fori_loop(0, n, step, 0, unroll=min(8, n))
    gn = gn_ref[...]
    for hi, hc in enumerate(heads):
        if has_sout:
            sf_ref[hi] = stf[hi].T
            sb_ref[hi] = stb[hi].T
        o = of[:, hc] + ob[:, hc]
        o_ref[:, hc] = (_rms(o, gn) * _silu(g_ref[:, hc])).astype(o_ref.dtype)


def _gla(q, v, f0, f1, g, gnorm, s0f, s0b, j, rows):
    d = q.shape[1]
    has_s0 = s0f is not None
    if has_s0:
        t, nb, roff, nhb = rows.ds, rows.db, rows.n_ctx // rows.ds, 2
    else:
        t, nb, roff, nhb = rows.s, rows.b, 0, 4
    dv = d // B_HEADS
    assert dv == LANES and B_DK == LANES and B_HEADS % nhb == 0
    w = nhb * LANES
    tok = pl.BlockSpec((t, w), lambda b, h: (b + roff, h))
    in_specs = [tok] * 5 + [pl.BlockSpec((None, 1, dv), lambda b, h: (j, 0, 0))]
    args = [q, v, f0, f1, g, gnorm]
    if has_s0:
        sspec = pl.BlockSpec((None, None, nhb, B_DK, dv), lambda b, h: (b, j, h, 0, 0))
        in_specs += [sspec, sspec]
        args += [s0f, s0b]
    out_shape = [jax.ShapeDtypeStruct((nb * t, d), MXU)]
    out_specs = [pl.BlockSpec((t, w), lambda b, h: (b, h))]
    if not has_s0:
        ospec = pl.BlockSpec((None, nhb, B_DK, dv), lambda b, h: (b, h, 0, 0))
        out_shape += [jax.ShapeDtypeStruct((nb, B_HEADS, B_DK, dv), F32)] * 2
        out_specs += [ospec, ospec]
    scratch = ([pltpu.VMEM((t, w), MXU) for _ in range(5)]
               + [pltpu.VMEM((t, w), F32) for _ in range(4)]
               + [pltpu.VMEM((nhb, t // B_CHUNK, dv, B_DK), F32) for _ in range(2)]
               + [pltpu.VMEM((nhb, dv, B_DK), F32) for _ in range(2)])
    return pl.pallas_call(
        functools.partial(_gla_kernel, t=t, nhb=nhb, has_s0=has_s0, has_sout=not has_s0),
        out_shape=tuple(out_shape),
        grid=(nb, B_HEADS // nhb),
        in_specs=in_specs,
        out_specs=tuple(out_specs),
        scratch_shapes=scratch,
        compiler_params=_cparams(("parallel", "parallel"), 40),
        name="gla_lat" if has_s0 else "gla_ctx",
    )(*args)


def _attn_c_kernel(*refs, qc, nh, has_cache):
    qn_ref, qp_ref, ckv_ref, kpe_ref = refs[:4]
    pos = 4
    if has_cache:
        ckvc_ref, kpec_ref = refs[pos:pos + 2]
        pos += 2
    wuk_ref, wuv_ref, o_ref = refs[pos:pos + 3]
    scr = refs[pos + 3:]
    sq = qn_ref.shape[0]
    scale = (C_NOPE + C_ROPE) ** -0.5
    wuk = wuk_ref[...].astype(MXU)
    wuv = wuv_ref[...].astype(MXU)

    kw = C_NOPE + 2 * C_ROPE
    vw = C_VD + LANES

    def expand(ckv_r, kpe_r, kcat_r, vcat_r):
        ck = ckv_r[...].astype(MXU)
        kn = jnp.dot(ck, wuk, preferred_element_type=F32).astype(kcat_r.dtype)
        vn = jnp.dot(ck, wuv, preferred_element_type=F32).astype(vcat_r.dtype)
        kp = kpe_r[...].astype(kcat_r.dtype)
        ones = jnp.ones((ck.shape[0], LANES), vcat_r.dtype)
        for hh in range(nh):
            kcat_r[:, hh * kw:(hh + 1) * kw] = jnp.concatenate(
                [kn[:, hh * C_NOPE:(hh + 1) * C_NOPE], kp, kp], axis=1)
            vcat_r[:, hh * vw:(hh + 1) * vw] = jnp.concatenate(
                [vn[:, hh * C_VD:(hh + 1) * C_VD], ones], axis=1)

    groups = [scr[0:2]]
    expand(ckv_ref, kpe_ref, *scr[0:2])
    if has_cache:
        groups.append(scr[2:4])
        expand(ckvc_ref, kpec_ref, *scr[2:4])
    lo = lax.broadcasted_iota(jnp.int32, (1, LANES), 1) < C_ROPE

    def body(c, carry):
        r = pl.ds(pl.multiple_of(c * qc, qc), qc)
        for hh in range(nh):
            cols = slice(hh * C_NOPE, (hh + 1) * C_NOPE)
            qp = qp_ref[r, (hh // 2) * LANES:(hh // 2 + 1) * LANES]
            zero = jnp.zeros_like(qp)
            qph = jnp.where(lo, qp, zero) if hh % 2 == 0 else jnp.where(lo, zero, qp)
            q = jnp.concatenate([qn_ref[r, cols], qph], axis=1)
            acc, _ = _attend(q, [kc[:, hh * kw:(hh + 1) * kw] for kc, _ in groups],
                             [vc[:, hh * vw:(hh + 1) * vw] for _, vc in groups], scale=scale)
            o_ref[r, cols] = (acc[:, :C_VD] / acc[:, C_VD:]).astype(o_ref.dtype)
        return carry

    lax.fori_loop(0, sq // qc, body, 0, unroll=True)


def _attn_c(qn, qp, ckv, kpe, cache_ckv, cache_kpe, wuk, wuv, j, rows):
    has_cache = cache_ckv is not None
    if has_cache:
        t, nb, roff = rows.ds, rows.db, rows.n_ctx // rows.ds
    else:
        t, nb, roff = rows.s, rows.b, 0
    nh = 2 if has_cache else 8
    hw = nh * C_NOPE
    qc = _largest_tile(t, 256)
    in_specs = [pl.BlockSpec((t, hw), lambda b, g: (b + roff, g)),
                pl.BlockSpec((t, nh * C_ROPE), lambda b, g: (b + roff, g)),
                pl.BlockSpec((t, C_KVLORA), lambda b, g: (b + roff, 0)),
                pl.BlockSpec((t, C_ROPE), lambda b, g: (b + roff, 0))]
    args = [qn, qp, ckv, kpe]
    kcw, vcw = nh * (C_NOPE + 2 * C_ROPE), nh * (C_VD + LANES)
    scratch = [pltpu.VMEM((t, kcw), MXU), pltpu.VMEM((t, vcw), MXU)]
    if has_cache:
        p = cache_ckv.shape[2]
        in_specs += [pl.BlockSpec((None, None, p, C_KVLORA), lambda b, g: (b, j, 0, 0)),
                     pl.BlockSpec((None, None, p, C_ROPE), lambda b, g: (b, j, 0, 0))]
        args += [cache_ckv, cache_kpe]
        scratch += [pltpu.VMEM((p, kcw), MXU), pltpu.VMEM((p, vcw), MXU)]
    in_specs += [pl.BlockSpec((None, C_KVLORA, hw), lambda b, g: (j, 0, g))] * 2
    args += [wuk, wuv]
    return pl.pallas_call(
        functools.partial(_attn_c_kernel, qc=qc, nh=nh, has_cache=has_cache),
        out_shape=jax.ShapeDtypeStruct((nb * t, C_HEADS * C_VD), MXU),
        grid=(nb, C_HEADS // nh),
        in_specs=in_specs,
        out_specs=pl.BlockSpec((t, hw), lambda b, g: (b, g)),
        scratch_shapes=scratch,
        compiler_params=_cparams(("parallel", "parallel"), 32),
        name="attn_c_lat" if has_cache else "attn_c_ctx",
    )(*args)


def _attn_d_kernel(*refs, has_cache):
    q_ref, k_ref, v_ref = refs[:3]
    pos = 3
    if has_cache:
        kc_ref, vc_ref = refs[pos:pos + 2]
        pos += 2
    sink_ref, o_ref = refs[pos:pos + 2]
    k2, v2 = refs[pos + 2:pos + 4]
    if has_cache:
        k2c, v2c = refs[pos + 4:pos + 6]
    t = q_ref.shape[0]
    qb = Q_BLOCK
    scale = D_DH ** -0.5
    r = D_HEADS // D_KV_HEADS
    win = min(t, qb + 2 * D_WINDOW)
    lo = lax.broadcasted_iota(jnp.int32, (1, LANES), 1) < D_DH
    rowblk = lax.broadcasted_iota(jnp.int32, (r * qb, 1), 0) // qb

    def stage(k_r, v_r, k2_r, v2_r, g):
        gs = slice(g * D_DH, (g + 1) * D_DH)
        kg, vg = k_r[:, gs].astype(k2_r.dtype), v_r[:, gs].astype(v2_r.dtype)
        k2_r[...] = jnp.concatenate([kg, kg], axis=1)
        v2_r[...] = _with_ones(jnp.concatenate([vg, vg], axis=1))

    for g in range(D_KV_HEADS):
        stage(k_ref, v_ref, k2, v2, g)
        if has_cache:
            stage(kc_ref, vc_ref, k2c, v2c, g)
        sink = jnp.zeros((r * qb, 1), F32)
        for e in range(r):
            h = g * r + e
            sink = jnp.where(rowblk == e, sink_ref[0:1, h:h + 1], sink)

        def body(c, carry, g=g, sink=sink):
            rq = pl.ds(pl.multiple_of(c * qb, qb), qb)
            tiles = []
            for e in range(r):
                cols = slice((g * r + e - e % 2) * D_DH, (g * r + e - e % 2 + 2) * D_DH)
                qp = q_ref[rq, cols] * scale
                zero = jnp.zeros_like(qp)
                tiles.append(jnp.where(lo, qp, zero) if e % 2 == 0 else jnp.where(lo, zero, qp))
            qs = jnp.concatenate(tiles, axis=0)
            if has_cache:
                ws = pl.multiple_of(jnp.clip(c * qb - D_WINDOW, 0, t - win), qb)
                rk = pl.ds(ws, win)
                qpos = c * qb + lax.broadcasted_iota(jnp.int32, (r * qb, 1), 0) % qb
                kpos = ws + lax.broadcasted_iota(jnp.int32, (1, win), 1)
                valid = jnp.abs(qpos - kpos) <= D_WINDOW
                parts = [_dot_nt(qs, k2c[...]), jnp.where(valid, _dot_nt(qs, k2[rk, :]), NEG)]
                vs = [v2c[...], v2[rk, :]]
            else:
                parts = [_dot_nt(qs, k2[...])]
                vs = [v2[...]]
            m = functools.reduce(jnp.maximum, [jnp.max(x, axis=-1, keepdims=True) for x in parts])
            m = jnp.maximum(m, sink)
            acc = functools.reduce(lambda a, b: a + b,
                                   [_dot(jnp.exp(x - m), vx) for x, vx in zip(parts, vs)])
            o = acc[:, :LANES] / (acc[:, LANES:] + jnp.exp(sink - m))
            for pr in range(r // 2):
                cols = slice((g * r + 2 * pr) * D_DH, (g * r + 2 * pr + 2) * D_DH)
                pair = jnp.where(lo, o[2 * pr * qb:(2 * pr + 1) * qb, :],
                                 o[(2 * pr + 1) * qb:(2 * pr + 2) * qb, :])
                o_ref[rq, cols] = pair.astype(o_ref.dtype)
            return carry

        lax.fori_loop(0, t // qb, body, 0, unroll=4)


def _attn_d(q, k, v, cache_k, cache_v, sink, j, rows):
    d = q.shape[1]
    kvw = D_KV_HEADS * D_DH
    has_cache = cache_k is not None
    if has_cache:
        t, nb, roff = rows.ds, rows.db, rows.n_ctx // rows.ds
    else:
        t, nb, roff = rows.s, rows.b, 0
    assert t % Q_BLOCK == 0
    in_specs = [pl.BlockSpec((t, d), lambda b: (b + roff, 0)),
                pl.BlockSpec((t, kvw), lambda b: (b + roff, 0)),
                pl.BlockSpec((t, kvw), lambda b: (b + roff, 0))]
    args = [q, k, v]
    scratch = [pltpu.VMEM((t, LANES), MXU), pltpu.VMEM((t, 2 * LANES), MXU)]
    if has_cache:
        p = cache_k.shape[2]
        cspec = pl.BlockSpec((None, None, p, kvw), lambda b: (b, j, 0, 0))
        in_specs += [cspec, cspec]
        args += [cache_k, cache_v]
        scratch += [pltpu.VMEM((p, LANES), MXU), pltpu.VMEM((p, 2 * LANES), MXU)]
    in_specs.append(pl.BlockSpec((None, 1, D_HEADS), lambda b: (j, 0, 0)))
    args.append(sink)
    return pl.pallas_call(
        functools.partial(_attn_d_kernel, has_cache=has_cache),
        out_shape=jax.ShapeDtypeStruct((nb * t, d), MXU),
        grid=(nb,),
        in_specs=in_specs,
        out_specs=pl.BlockSpec((t, d), lambda b: (b, 0)),
        scratch_shapes=scratch,
        compiler_params=_cparams(("parallel",), 32),
        name="attn_d_lat" if has_cache else "attn_d_ctx",
    )(*args)


def kernel(x_prompt, x_sample, cache_a_k, cache_a_v, state_b_fwd, state_b_bwd, cache_c_ckv,
           cache_c_kpe, cache_d_k, cache_d_v, c, c_ctx, ada_w, ada_b, norm_mix_pre,
           norm_mix_post, norm_ffn_pre, norm_ffn_post, a_wq, a_wk, a_wv, a_wo, a_lambda,
           a_subln, b_wq, b_wi, b_wf, b_lower, b_wg, b_gnorm, b_wo, c_wdq, c_qnorm, c_wuq,
           c_wdkv, c_kvnorm, c_wuk, c_wuv, c_wo, d_wq, d_wk, d_wv, d_sink, d_wo, ffn_wg,
           ffn_wu, ffn_wd):
    b, s, d = x_prompt.shape
    db, ds, _ = x_sample.shape
    depth = ada_w.shape[0]
    rows = _Rows(b, s, db, ds, _largest_tile(math.gcd(b * s, ds), 1024))
    n_ctx = rows.n_ctx

    cond = jnp.zeros((rows.crows, d), F32).at[:db].set(c).at[db].set(c_ctx)
    mod = _ada(cond, ada_w, ada_b)
    g3 = lambda a: a.reshape(a.shape[0], 1, a.shape[1])
    n_mix_pre, n_mix_post = g3(norm_mix_pre), g3(norm_mix_post)
    n_ffn_pre, n_ffn_post = g3(norm_ffn_pre), g3(norm_ffn_post)
    tabs = _rope_tables(ds)
    rope_epi = lambda tn: (_make_epi_rope(rows, tn), _rope_extra(tabs, rows))

    x = (x_prompt.reshape(n_ctx, d), x_sample.reshape(rows.n_lat, d))
    h = _prep(x[0], x[1], n_mix_pre, mod, 0, rows)

    flat4 = lambda a: a.reshape(a.shape[:3] + (-1,))
    wo_b = {0: _cast_mxu(a_wo), 1: _cast_mxu(b_wo), 2: _cast_mxu(c_wo), 3: _cast_mxu(d_wo)}
    wd_b = _cast_mxu(ffn_wd)
    wf = b_wf.reshape((-1,) + b_wf.shape[2:])
    outs = {k_: [] for k_ in ("a_k", "a_v", "b_f", "b_b", "c_ckv", "c_kpe", "d_k", "d_v")}
    for i in range(depth):
        m, j = i % N_MIXERS, i // N_MIXERS
        if m == 0:
            tn = _largest_tile(a_wq.shape[2], 1024)
            epi, ext = rope_epi(tn)
            q, = _proj(h, [(a_wq, j)], epi, [(tn, MXU)], rows, tn, ext, name="a_q")
            kv_outs = [(tn, MXU), (tn, F32, "ctx")]
            k, kc = _proj(h, [(a_wk, j)], _with_ctx_copy(epi, rows), kv_outs, rows, tn, ext,
                          name="a_k")
            v, vc = _proj(h, [(a_wv, j)], _with_ctx_copy(_epi_store, rows), kv_outs, rows, tn,
                          name="a_v")
            o = (_attn_a(q, k, v, None, None, a_lambda, g3(a_subln), i, j, rows),
                 _attn_a(q, k, v, flat4(cache_a_k), flat4(cache_a_v), a_lambda, g3(a_subln),
                         i, j, rows))
            outs["a_k"].append(kc.reshape(b, s, A_HEADS, 2 * A_DH))
            outs["a_v"].append(vc.reshape(b, s, A_HEADS, 2 * A_DH))
        elif m == 1:
            tn = _largest_tile(b_wq.shape[2], 1024)
            q, = _proj(h, [(b_wq, j)], _epi_silu, [(tn, F32)], rows, tn, nsub=2, name="b_q")
            vi, = _proj(h, [(b_wi, j)], _epi_store, [(tn, MXU)], rows, tn, nsub=2, name="b_i")
            g, = _proj(h, [(b_wg, j)], _epi_store, [(tn, F32)], rows, tn, nsub=2, name="b_g")
            fs = []
            for dr in range(2):
                bl_spec = pl.BlockSpec((None, depth, tn), lambda jn, mm, dr=dr: (dr, 0, jn))
                f, = _proj(h, [(wf, 2 * j + dr)], _make_epi_forget(i, depth), [(tn, F32)], rows,
                           tn, [(jnp.swapaxes(b_lower, 0, 1), bl_spec)], nsub=2,
                           name="b_f%d" % dr)
                fs.append(f)
            oc, sf, sb = _gla(q, vi, fs[0], fs[1], g, g3(b_gnorm), None, None, j, rows)
            ol, = _gla(q, vi, fs[0], fs[1], g, g3(b_gnorm), state_b_fwd, state_b_bwd, j, rows)
            o = (oc, ol)
            outs["b_f"].append(sf)
            outs["b_b"].append(sb)
        elif m == 2:
            nq = c_wdq.shape[2]
            cq, = _proj(h, [(c_wdq, j)], _make_epi_rmsnorm(), [(nq, MXU)], rows, nq,
                        [(g3(c_qnorm), pl.BlockSpec((None, 1, nq), lambda jn, mm: (j, 0, 0)))],
                        name="c_dq")
            nkv = c_wdkv.shape[2]
            ext = [(g3(c_kvnorm), pl.BlockSpec((None, 1, C_KVLORA), lambda jn, mm: (j, 0, 0)))]
            ext += _rope_extra(tabs, rows)
            ckv, kpe, ckv_c, kpe_c = _proj(
                h, [(c_wdkv, j)], _make_epi_ckv(rows),
                [(C_KVLORA, MXU), (C_ROPE, MXU), (C_KVLORA, F32, "ctx"), (C_ROPE, F32, "ctx")],
                rows, nkv, ext, name="c_dkv")
            wuq = c_wuq.reshape(c_wuq.shape[0], nq, C_HEADS, C_NOPE + C_ROPE)
            wuq_n = wuq[:, :, :, :C_NOPE].reshape(-1, nq, C_HEADS * C_NOPE)
            wuq_p = wuq[:, :, :, C_NOPE:].reshape(-1, nq, C_HEADS * C_ROPE)
            tn = _largest_tile(wuq_n.shape[2], 1024)
            qn, = _proj(cq, [(wuq_n, j)], _epi_store, [(tn, MXU)], rows, tn, name="c_qn")
            tnp = wuq_p.shape[2]
            epi, ext = rope_epi(tnp)
            qp, = _proj(cq, [(wuq_p, j)], epi, [(tnp, MXU)], rows, tnp, ext, name="c_qp")
            o = (_attn_c(qn, qp, ckv, kpe, None, None, c_wuk, c_wuv, j, rows),
                 _attn_c(qn, qp, ckv, kpe, cache_c_ckv, cache_c_kpe, c_wuk, c_wuv, j, rows))
            outs["c_ckv"].append(ckv_c.reshape(b, s, C_KVLORA))
            outs["c_kpe"].append(kpe_c.reshape(b, s, C_ROPE))
        else:
            tn = _largest_tile(d_wq.shape[2], 1024)
            epi, ext = rope_epi(tn)
            q, = _proj(h, [(d_wq, j)], epi, [(tn, MXU)], rows, tn, ext, name="d_q")
            kvw = d_wk.shape[2]
            epi, ext = rope_epi(kvw)
            kv_outs = [(kvw, MXU), (kvw, F32, "ctx")]
            k, kc = _proj(h, [(d_wk, j)], _with_ctx_copy(epi, rows), kv_outs, rows, kvw, ext,
                          name="d_k")
            v, vc = _proj(h, [(d_wv, j)], _with_ctx_copy(_epi_store, rows), kv_outs, rows, kvw,
                          name="d_v")
            o = (_attn_d(q, k, v, None, None, g3(d_sink), j, rows),
                 _attn_d(q, k, v, flat4(cache_d_k), flat4(cache_d_v), g3(d_sink), j, rows))
            outs["d_k"].append(kc.reshape(b, s, D_KV_HEADS, D_DH))
            outs["d_v"].append(vc.reshape(b, s, D_KV_HEADS, D_DH))

        x, h = _resid(o, wo_b[m], j, x, mod, i, 2, n_mix_post, rows, nxt=(i, 3, n_ffn_pre),
                      nsub=2, name="mix_out")
        tf = _largest_tile(ffn_wg.shape[2], 512)
        a, = _proj(h, [(ffn_wg, i), (ffn_wu, i)], _epi_swiglu, [(tf, MXU)], rows, tf,
                   tm=_largest_tile(rows.mt, 2048), nsub=2, name="ffn_gu")
        nxt = (i + 1, 0, n_mix_pre) if i + 1 < depth else None
        x, h = _resid(a, wd_b, i, x, mod, i, 5, n_ffn_post, rows, nxt=nxt, name="ffn_down")

    stack = lambda lst: jnp.stack(lst, axis=1)
    return (x.reshape(b, s, d), h.reshape(db, ds, d),
            stack(outs["a_k"]), stack(outs["a_v"]), stack(outs["b_f"]), stack(outs["b_b"]),
            stack(outs["c_ckv"]), stack(outs["c_kpe"]), stack(outs["d_k"]), stack(outs["d_v"]))
```

```python
import functools
import math

import numpy as np
import jax
import jax.numpy as jnp
from jax import lax
from jax.experimental import pallas as pl
from jax.experimental.pallas import tpu as pltpu

EPS = 1e-6
ROPE_BASE = 10000.0
GRID_W = 64
NEG = -1e30
N_MIXERS = 4
A_HEADS, A_DH = 16, 64
B_HEADS, B_DK, B_CHUNK = 16, 128, 32
C_HEADS, C_NOPE, C_ROPE, C_VD = 16, 128, 64, 128
D_HEADS, D_KV_HEADS, D_DH, D_WINDOW, Q_BLOCK = 32, 4, 64, 128, 128

LANES = 128
SUBLANES = 8
VMEM_CAP_MIB = 56

MXU = jnp.bfloat16
F32 = jnp.float32


def _cparams(sem, vmem_mib):
    return pltpu.CompilerParams(dimension_semantics=sem,
                                vmem_limit_bytes=min(vmem_mib, VMEM_CAP_MIB) << 20)


def _dot(a, b):
    return jnp.dot(a.astype(MXU), b.astype(MXU), preferred_element_type=F32)


def _dot_nt(a, b):
    return lax.dot_general(a.astype(MXU), b.astype(MXU), (((1,), (1,)), ((), ())),
                           preferred_element_type=F32)


def _dot_tn(a, b):
    return lax.dot_general(a.astype(MXU), b.astype(MXU), (((0,), (0,)), ((), ())),
                           preferred_element_type=F32)


def _sigmoid(x):
    return 1.0 / (1.0 + jnp.exp(-x))


def _silu(x):
    return x * _sigmoid(x)


def _rms(x, g):
    ms = jnp.mean(x * x, axis=-1, keepdims=True)
    return x * lax.rsqrt(ms + EPS) * g


def _largest_tile(n, pref):
    t = min(n, pref)
    while n % t:
        t -= SUBLANES
    assert t > 0 and n % t == 0
    return t


class _Rows:
    def __init__(self, b, s, db, ds, tm):
        self.b, self.s, self.db, self.ds = b, s, db, ds
        self.n_ctx, self.n_lat = b * s, db * ds
        self.mt = self.n_ctx + self.n_lat
        self.tm = tm
        assert self.n_ctx % tm == 0 and ds % tm == 0 and self.n_ctx % ds == 0
        self.nct = self.n_ctx // tm
        self.ntiles = self.mt // tm
        self.crows = -(-(db + 1) // SUBLANES) * SUBLANES

    def cond_row(self, i):
        lat = ((i - self.nct) * self.tm) // self.ds
        return jnp.where(i < self.nct, self.db, lat)

    def pos_block(self, i):
        lat = ((i - self.nct) * self.tm % self.ds) // self.tm
        return jnp.where(i < self.nct, self.ds // self.tm, lat)


def _rope_tables(n_tok, n_identity):
    nf = 16
    inv = (ROPE_BASE ** (-np.arange(nf, dtype=np.float32) / nf)).astype(np.float32)
    t = np.arange(n_tok)
    row, col = (t // GRID_W).astype(np.float32), (t % GRID_W).astype(np.float32)
    lane = np.arange(64)
    pos = np.where(lane[None, :] < 32, row[:, None], col[:, None]).astype(np.float32)
    ang = (pos * inv[lane % nf][None, :]).astype(np.float32)
    cos, sin = np.cos(ang).astype(np.float32), np.sin(ang).astype(np.float32)
    first = (lane % 32) < 16
    sa = np.where(first[None, :], -sin, 0.0).astype(np.float32)
    sb = np.where(first[None, :], 0.0, sin).astype(np.float32)
    def table(a, fill):
        a = np.concatenate([a, np.full((n_identity, 64), fill, np.float32)], axis=0)
        return jnp.asarray(np.concatenate([a, a], axis=1))

    return table(cos, 1.0), table(sa, 0.0), table(sb, 0.0)


def _rope128(y, cos, sa, sb):
    return (y * cos + pltpu.roll(y, LANES - 16, 1) * sa + pltpu.roll(y, 16, 1) * sb)


def _ada_kernel(c_ref, w_ref, b_ref, o_ref):
    o_ref[...] = _dot(_silu(c_ref[...]), w_ref[...]) + b_ref[...]


def _ada(cond, ada_w, ada_b):
    depth, d, n = ada_w.shape
    cr = cond.shape[0]
    tn = _largest_tile(n, 1024)
    return pl.pallas_call(
        _ada_kernel,
        out_shape=jax.ShapeDtypeStruct((depth, cr, n), F32),
        grid=(depth, n // tn),
        in_specs=[pl.BlockSpec((cr, d), lambda l, j: (0, 0)),
                  pl.BlockSpec((None, d, tn), lambda l, j: (l, 0, j)),
                  pl.BlockSpec((None, 1, tn), lambda l, j: (l, 0, j))],
        out_specs=pl.BlockSpec((None, cr, tn), lambda l, j: (l, 0, j)),
        compiler_params=_cparams(("parallel", "parallel"), 40),
        name="ada_mod",
    )(cond, ada_w, ada_b.reshape(depth, 1, n))


def _prep_kernel(xp_ref, xs_ref, g_ref, mod_ref, h_ref, *, rows, d):
    i = pl.program_id(0)
    x = jnp.where(i < rows.nct, xp_ref[...], xs_ref[...])
    cr = rows.cond_row(i)
    shift = mod_ref[pl.ds(cr, 1), pl.ds(0, d)]
    scale = mod_ref[pl.ds(cr, 1), pl.ds(d, d)]
    h_ref[...] = (_rms(x, g_ref[...]) * (1.0 + scale) + shift).astype(h_ref.dtype)


def _prep(xp, xs, g, mod, layer, rows):
    d = xp.shape[1]
    rows = _Rows(rows.b, rows.s, rows.db, rows.ds, _largest_tile(rows.tm, 512))
    tm, nct = rows.tm, rows.nct
    return pl.pallas_call(
        functools.partial(_prep_kernel, rows=rows, d=d),
        out_shape=jax.ShapeDtypeStruct((rows.mt, d), MXU),
        grid=(rows.ntiles,),
        in_specs=[pl.BlockSpec((tm, d), lambda i: (jnp.minimum(i, nct - 1), 0)),
                  pl.BlockSpec((tm, d), lambda i: (jnp.maximum(i - nct, 0), 0)),
                  pl.BlockSpec((None, 1, d), lambda i: (layer, 0, 0)),
                  pl.BlockSpec((None, rows.crows, mod.shape[2]), lambda i: (layer, 0, 0))],
        out_specs=pl.BlockSpec((tm, d), lambda i: (i, 0)),
        compiler_params=_cparams(("parallel",), 32 * tm * d // (1 << 20) + 8),
        name="prep",
    )(xp, xs, g, mod)


def _proj_kernel(*refs, n_w, n_extra, n_out, nsub, epi, fin):
    h_ref = refs[0]
    w_refs = refs[1:1 + n_w]
    extra = refs[1 + n_w:1 + n_w + n_extra]
    outs = refs[1 + n_w + n_extra:1 + n_w + n_extra + n_out]
    wb_refs = refs[1 + n_w + n_extra + n_out:]
    m = pl.program_id(1)

    @pl.when(m == 0)
    def _():
        for w_ref, wb_ref in zip(w_refs, wb_refs):
            wb_ref[...] = w_ref[...].astype(wb_ref.dtype)

    rs = h_ref.shape[0] // nsub
    slabs = [slice(s * rs, (s + 1) * rs) for s in range(nsub)]
    ys_all = [[jnp.dot(h_ref[r, :], wb_ref[...], preferred_element_type=F32) for wb_ref in wb_refs]
              for r in slabs]
    for r, ys in zip(slabs, ys_all):
        epi(ys, m, extra, outs, r)
    if fin is not None:
        fin(ys_all, slabs, m, extra, outs)


def _proj(h, ws, epi, outs, rows, tn, extra=(), tm=None, nsub=1, name="proj"):
    mt, k = h.shape
    n = ws[0][0].shape[2]
    assert n % tn == 0
    tm = rows.tm if tm is None else tm
    assert mt % tm == 0
    nj = n // tn
    in_specs = [pl.BlockSpec((tm, k), lambda j, m: (m, 0))]
    in_specs += [pl.BlockSpec((None, k, tn), lambda j, m, li=li: (li, 0, j)) for _, li in ws]
    in_specs += [s for _, s in extra]
    nct = rows.nct
    ctx_only = [len(o) > 2 and o[2] == "ctx" for o in outs]
    outs = [o[:2] for o in outs]
    out_shape = tuple(jax.ShapeDtypeStruct((rows.n_ctx if co else mt, c * nj), dt)
                      for (c, dt), co in zip(outs, ctx_only))
    out_specs = tuple(pl.BlockSpec((tm, c), (lambda j, m: (jnp.minimum(m, nct - 1), j)) if co
                                   else (lambda j, m: (m, j)))
                      for (c, _), co in zip(outs, ctx_only))
    assert not any(ctx_only) or tm == rows.tm
    out_bytes = sum(tm * c * jnp.dtype(dt).itemsize for c, dt in outs)
    vmem = (2 * tm * k * 2 + len(ws) * (2 * k * tn * 4 + k * tn * 2) + 2 * out_bytes
            + len(ws) * tm * tn * 8) // (1 << 20) + 6
    epi, fin = epi if isinstance(epi, tuple) else (epi, None)
    res = pl.pallas_call(
        functools.partial(_proj_kernel, n_w=len(ws), n_extra=len(extra), n_out=len(outs),
                          nsub=nsub, epi=epi, fin=fin),
        out_shape=out_shape,
        grid=(nj, mt // tm),
        in_specs=in_specs,
        out_specs=out_specs,
        scratch_shapes=[pltpu.VMEM((k, tn), MXU) for _ in ws],
        compiler_params=_cparams(("parallel", "arbitrary"), vmem),
        name=name,
    )(h, *[w for w, _ in ws], *[a for a, _ in extra])
    return res


def _epi_store(ys, m, extra, outs, r):
    outs[0][r, :] = ys[0].astype(outs[0].dtype)


def _epi_silu(ys, m, extra, outs, r):
    outs[0][r, :] = _silu(ys[0]).astype(outs[0].dtype)


def _epi_swiglu(ys, m, extra, outs, r):
    outs[0][r, :] = (_silu(ys[0]) * ys[1]).astype(outs[0].dtype)


def _make_epi_rope(tn):
    def epi(ys, m, extra, outs, r):
        cos_ref, sa_ref, sb_ref = extra
        y, o_ref = ys[0], outs[0]
        cos, sa, sb = cos_ref[r, :], sa_ref[r, :], sb_ref[r, :]
        for c in range(tn // LANES):
            sl = slice(c * LANES, (c + 1) * LANES)
            o_ref[r, sl] = _rope128(y[:, sl], cos, sa, sb).astype(o_ref.dtype)
    return epi


def _with_ctx_copy(epi, rows):
    def fin(ys_all, slabs, m, extra, outs):
        @pl.when(m < rows.nct)
        def _():
            for r, ys in zip(slabs, ys_all):
                outs[-1][r, :] = ys[0]
    return epi, fin


def _rope_extra(tabs, rows):
    tm = rows.tm
    spec = pl.BlockSpec((tm, LANES), lambda j, m: (rows.pos_block(m), 0))
    return [(t, spec) for t in tabs]


def _make_epi_forget(layer_idx, depth):
    def epi(ys, m, extra, outs, r):
        bl = extra[0][...]
        e = jnp.exp(bl - jnp.max(bl, axis=0, keepdims=True))
        p = e / jnp.sum(e, axis=0, keepdims=True)
        lb = jnp.sum(p[0:layer_idx + 1], axis=0, keepdims=True) - p[0:1]
        outs[0][r, :] = lb + (1.0 - lb) * _sigmoid(ys[0])
    return epi


def _make_epi_rmsnorm():
    def epi(ys, m, extra, outs, r):
        outs[0][r, :] = _rms(ys[0], extra[0][...]).astype(outs[0].dtype)
    return epi


def _make_epi_ckv(rows):
    def epi(ys, m, extra, outs, r):
        g_ref, cos_ref, sa_ref, sb_ref = extra
        ckv_all, kpe_all, ckv_ctx, kpe_ctx = outs
        y = ys[0]
        ckv = _rms(y[:, :C_KVLORA], g_ref[...])
        ckv_all[r, :] = ckv.astype(ckv_all.dtype)
        kpe = y[:, C_KVLORA:C_KVLORA + C_ROPE]

        @pl.when(m < rows.nct)
        def _():
            kpe_all[r, :] = kpe.astype(kpe_all.dtype)
            ckv_ctx[r, :] = ckv
            kpe_ctx[r, :] = kpe

        @pl.when(m >= rows.nct)
        def _():
            k2 = jnp.concatenate([kpe, kpe], axis=1)
            kr = _rope128(k2, cos_ref[r, :], sa_ref[r, :], sb_ref[r, :])
            kpe_all[r, :] = kr[:, :C_ROPE].astype(kpe_all.dtype)
    return epi


C_KVLORA = 256


def _cast_kernel(w_ref, o_ref):
    o_ref[...] = w_ref[...].astype(o_ref.dtype)


def _cast_mxu(w):
    l, k, n = w.shape
    tr = _largest_tile(k, 512)
    return pl.pallas_call(
        _cast_kernel,
        out_shape=jax.ShapeDtypeStruct(w.shape, MXU),
        grid=(l, k // tr),
        in_specs=[pl.BlockSpec((None, tr, n), lambda i, r: (i, r, 0))],
        out_specs=pl.BlockSpec((None, tr, n), lambda i, r: (i, r, 0)),
        compiler_params=_cparams(("parallel", "parallel"), 12 * tr * n // (1 << 20) + 8),
        name="cast_w",
    )(w)


def _resid_kernel(*refs, rows, d, nsub, gate_chunk, next_chunk, split_a, split_x):
    refs = list(refs)
    a_refs = [refs.pop(0) for _ in range(2 if split_a else 1)]
    w_ref = refs.pop(0)
    x_refs = [refs.pop(0) for _ in range(2 if split_x else 1)]
    modc_ref, gpost_ref = refs.pop(0), refs.pop(0)
    if next_chunk is not None:
        modn_ref, gpre_ref = refs.pop(0), refs.pop(0)
        xo_ref, ho_ref = refs
    else:
        xo_ref, xl_ref = refs
    i = pl.program_id(0)

    def rows_of(parts, r):
        if len(parts) == 1:
            return parts[0][r, :]
        return jnp.where(i < rows.nct, parts[0][r, :], parts[1][r, :])

    cr = rows.cond_row(i)
    gate = modc_ref[pl.ds(cr, 1), pl.ds(gate_chunk * d, d)] * gpost_ref[...]
    if next_chunk is not None:
        shift = modn_ref[pl.ds(cr, 1), pl.ds(next_chunk * d, d)]
        scale = (1.0 + modn_ref[pl.ds(cr, 1), pl.ds((next_chunk + 1) * d, d)]) * gpre_ref[...]
    w = w_ref[...]
    rs = a_refs[0].shape[0] // nsub
    slabs = [slice(s * rs, (s + 1) * rs) for s in range(nsub)]
    ys = [jnp.dot(rows_of(a_refs, r), w, preferred_element_type=F32) for r in slabs]
    for r, y in zip(slabs, ys):
        yn = y * lax.rsqrt(jnp.mean(y * y, axis=-1, keepdims=True) + EPS)
        xn = rows_of(x_refs, r) + yn * gate
        if next_chunk is not None:
            xo_ref[r, :] = xn
            hn = xn * lax.rsqrt(jnp.mean(xn * xn, axis=-1, keepdims=True) + EPS)
            ho_ref[r, :] = (hn * scale + shift).astype(ho_ref.dtype)
        else:
            @pl.when(i < rows.nct)
            def _(xn=xn, r=r):
                xo_ref[r, :] = xn

            @pl.when(i >= rows.nct)
            def _(xn=xn, r=r):
                xl_ref[r, :] = xn


def _resid(a, w, li, x, mod, layer, gate_chunk, gpost, rows, nxt=None, nsub=1, name="resid"):
    split_a = isinstance(a, (tuple, list))
    a_parts = list(a) if split_a else [a]
    mt, kdim = rows.mt, a_parts[0].shape[1]
    d = w.shape[2]
    n_x = 2 if isinstance(x, (tuple, list)) else 1
    row_bytes = len(a_parts) * 2 * kdim * 2 + d * (8 * n_x + 8 + 4 + 12)
    budget = (VMEM_CAP_MIB - 6 << 20) - kdim * d * 2
    tm = _largest_tile(rows.tm, max(SUBLANES, 1 << int(math.log2(budget // row_bytes))))
    sub = _Rows(rows.b, rows.s, rows.db, rows.ds, tm)
    nct = sub.nct
    mspec = lambda l: pl.BlockSpec((None, rows.crows, mod.shape[2]), lambda i: (l, 0, 0))
    gspec = lambda l: pl.BlockSpec((None, 1, d), lambda i: (l, 0, 0))
    def row_specs(width, split):
        if split:
            return [pl.BlockSpec((tm, width), lambda i: (jnp.minimum(i, nct - 1), 0)),
                    pl.BlockSpec((tm, width), lambda i: (jnp.maximum(i - nct, 0), 0))]
        return [pl.BlockSpec((tm, width), lambda i: (i, 0))]

    split_x = isinstance(x, (tuple, list))
    x_parts = list(x) if split_x else [x]
    in_specs = row_specs(kdim, split_a) + [
        pl.BlockSpec((None, kdim, d), lambda i: (li, 0, 0), pipeline_mode=pl.Buffered(1))
    ] + row_specs(d, split_x) + [mspec(layer), gspec(layer)]
    args = a_parts + [w] + x_parts + [mod, gpost]
    if nxt is not None:
        nl, nchunk, gpre = nxt
        in_specs += [mspec(nl), gspec(nl)]
        args += [mod, gpre]
        out_shape = [jax.ShapeDtypeStruct((mt, d), F32), jax.ShapeDtypeStruct((mt, d), MXU)]
        out_specs = [pl.BlockSpec((tm, d), lambda i: (i, 0))] * 2
    else:
        nchunk = None
        out_shape = [jax.ShapeDtypeStruct((rows.n_ctx, d), F32),
                     jax.ShapeDtypeStruct((rows.n_lat, d), F32)]
        out_specs = [pl.BlockSpec((tm, d), lambda i: (jnp.minimum(i, nct - 1), 0)),
                     pl.BlockSpec((tm, d), lambda i: (jnp.maximum(i - nct, 0), 0))]
    vmem = (kdim * d * 2 + tm * row_bytes) // (1 << 20) + 8
    res = pl.pallas_call(
        functools.partial(_resid_kernel, rows=sub, d=d, nsub=nsub, gate_chunk=gate_chunk,
                          next_chunk=nchunk, split_a=split_a, split_x=split_x),
        out_shape=tuple(out_shape),
        grid=(mt // tm,),
        in_specs=in_specs,
        out_specs=tuple(out_specs),
        compiler_params=_cparams(("arbitrary",), vmem),
        name=name,
    )(*args)
    return res


def _with_ones(v):
    return jnp.concatenate([v, jnp.ones((v.shape[0], LANES), v.dtype)], axis=1)


def _attend(q, keys, vaugs, scale=None, masks=None):
    parts = [_dot_nt(q, kk) for kk in keys]
    if masks is not None:
        parts = [s if mk is None else jnp.where(mk, s, NEG) for s, mk in zip(parts, masks)]
    if scale is not None:
        parts = [s * scale for s in parts]
    m = functools.reduce(jnp.maximum, [jnp.max(s, axis=-1, keepdims=True) for s in parts])
    acc = functools.reduce(lambda a, b: a + b,
                           [_dot(jnp.exp(s - m), va) for s, va in zip(parts, vaugs)])
    return acc, m


def _attn_a_kernel(*refs, hb, qc, lam_init, has_cache):
    q_ref, k_ref, v_ref = refs[:3]
    pos = 3
    if has_cache:
        kc_ref, vc_ref = refs[pos:pos + 2]
        pos += 2
    lam_ref, sub_ref, o_ref = refs[pos:pos + 3]
    sq = q_ref.shape[0]
    dv = 2 * A_DH
    scale = A_DH ** -0.5
    lp = lam_ref[...]
    lam = (jnp.exp(jnp.sum(lp[0:1] * lp[1:2], axis=1, keepdims=True))
           - jnp.exp(jnp.sum(lp[2:3] * lp[3:4], axis=1, keepdims=True)) + lam_init)
    lo = lax.broadcasted_iota(jnp.int32, (1, dv), 1) < A_DH
    sub = sub_ref[...] * (1.0 - lam_init)
    for h in range(hb):
        cols = slice(h * dv, (h + 1) * dv)
        keys = [k_ref[:, cols].astype(MXU)]
        vals = [_with_ones(v_ref[:, cols].astype(MXU))]
        if has_cache:
            keys.append(kc_ref[:, cols].astype(MXU))
            vals.append(_with_ones(vc_ref[:, cols].astype(MXU)))

        def body(c, carry, cols=cols, keys=keys, vals=vals):
            r = pl.ds(pl.multiple_of(c * qc, qc), qc)
            q = q_ref[r, cols] * scale
            zero = jnp.zeros_like(q)
            a1, _ = _attend(jnp.where(lo, q, zero), keys, vals)
            a2, _ = _attend(jnp.where(lo, zero, q), keys, vals)
            o = a1[:, :dv] / a1[:, dv:] - lam * (a2[:, :dv] / a2[:, dv:])
            ms = jnp.mean(o * o, axis=-1, keepdims=True)
            o_ref[r, cols] = (o * lax.rsqrt(ms + EPS) * sub).astype(o_ref.dtype)
            return carry

        lax.fori_loop(0, sq // qc, body, 0, unroll=True)


def _attn_a(q, k, v, cache_k, cache_v, lam_p, subln, layer_idx, j, rows):
    d = q.shape[1]
    lam_init = 0.8 - 0.6 * math.exp(-0.3 * layer_idx)
    has_cache = cache_k is not None
    if has_cache:
        t, nb, roff = rows.ds, rows.db, rows.n_ctx // rows.ds
        hb = 4
    else:
        t, nb, roff = rows.s, rows.b, 0
        hb = A_HEADS
    cw = hb * 2 * A_DH
    qc = _largest_tile(t, 256)
    tok = lambda b, g: (b + roff, g)
    in_specs = [pl.BlockSpec((t, cw), tok)] * 3
    args = [q, k, v]
    if has_cache:
        p = cache_k.shape[2]
        cspec = pl.BlockSpec((None, None, p, cw), lambda b, g: (b, j, 0, g))
        in_specs += [cspec, cspec]
        args += [cache_k, cache_v]
    in_specs += [pl.BlockSpec((None, 4, A_DH), lambda b, g: (j, 0, 0)),
                 pl.BlockSpec((None, 1, 2 * A_DH), lambda b, g: (j, 0, 0))]
    args += [lam_p, subln]
    return pl.pallas_call(
        functools.partial(_attn_a_kernel, hb=hb, qc=qc, lam_init=lam_init, has_cache=has_cache),
        out_shape=jax.ShapeDtypeStruct((nb * t, d), MXU),
        grid=(nb, d // cw),
        in_specs=in_specs,
        out_specs=pl.BlockSpec((t, cw), lambda b, g: (b, g)),
        compiler_params=_cparams(("parallel", "parallel"), 40),
        name="attn_a_lat" if has_cache else "attn_a_ctx",
    )(*args)


def _gla_kernel(*refs, t, nhb, has_s0, has_sout):
    q_ref, v_ref, f0_ref, f1_ref, g_ref, gn_ref = refs[:6]
    pos = 6
    if has_s0:
        s0f_ref, s0b_ref = refs[pos:pos + 2]
        pos += 2
    o_ref = refs[pos]
    pos += 1
    if has_sout:
        sf_ref, sb_ref = refs[pos:pos + 2]
        pos += 2
    qdf, kif, qdb, kib, vb, bcf, bsb, of, ob, uf, ub, stf, stb = refs[pos:]
    c = B_CHUNK
    n = t // c
    heads = [slice(hi * LANES, (hi + 1) * LANES) for hi in range(nhb)]
    r32 = lax.broadcasted_iota(jnp.int32, (t, 1), 0) % c

    def prefix(x):
        s = 1
        while s < c:
            x = x + jnp.where(r32 >= s, pltpu.roll(x, s, 0), 0.0)
            s *= 2
        return x

    def suffix(x):
        s = 1
        while s < c:
            x = x + jnp.where(r32 < c - s, pltpu.roll(x, t - s, 0), 0.0)
            s *= 2
        return x

    q = q_ref[...]
    vb[...] = v_ref[...].astype(vb.dtype)
    f = f0_ref[...]
    bc = prefix(jnp.log(f))
    bcf[...] = bc
    qdf[...] = (q * jnp.exp(bc)).astype(qdf.dtype)
    kif[...] = ((1.0 - f) * jnp.exp(-bc)).astype(kif.dtype)
    f = f1_ref[...]
    bs = suffix(jnp.log(f))
    bsb[...] = bs
    qdb[...] = (q * jnp.exp(bs)).astype(qdb.dtype)
    kib[...] = ((1.0 - f) * jnp.exp(-bs)).astype(kib.dtype)

    blk = _largest_tile(t, 256)
    ti = lax.broadcasted_iota(jnp.int32, (blk, blk), 0)
    si = lax.broadcasted_iota(jnp.int32, (blk, blk), 1)
    same = (ti // c) == (si // c)
    lower, upper = same & (ti >= si), same & (ti <= si)
    for bi in range(t // blk):
        rb = slice(bi * blk, (bi + 1) * blk)
        for hc in heads:
            vv = vb[rb, hc]
            of[rb, hc] = _dot(jnp.where(lower, _dot_nt(qdf[rb, hc], kif[rb, hc]), 0.0), vv)
            ob[rb, hc] = _dot(jnp.where(upper, _dot_nt(qdb[rb, hc], kib[rb, hc]), 0.0), vv)

    def incr(i, carry):
        r = pl.ds(pl.multiple_of(i * c, c), c)
        vv = vb[r, :]
        bc_c, bs_c = bcf[r, :], bsb[r, :]
        kef = (1.0 - f0_ref[r, :]) * jnp.exp(bc_c[c - 1:c, :] - bc_c)
        keb = (1.0 - f1_ref[r, :]) * jnp.exp(bs_c[0:1, :] - bs_c)
        for hi, hc in enumerate(heads):
            uf[hi, i] = _dot_tn(vv[:, hc], kef[:, hc])
            ub[hi, i] = _dot_tn(vv[:, hc], keb[:, hc])
        return carry

    lax.fori_loop(0, n, incr, 0, unroll=min(8, n))

    for hi in range(nhb):
        if has_s0:
            stf[hi] = s0f_ref[hi].T
            stb[hi] = s0b_ref[hi].T
        else:
            stf[hi] = jnp.zeros((B_DK, B_DK), F32)
            stb[hi] = jnp.zeros((B_DK, B_DK), F32)

    def step(i, carry):
        rf = pl.ds(pl.multiple_of(i * c, c), c)
        ib = n - 1 - i
        rb = pl.ds(pl.multiple_of(ib * c, c), c)
        decf = jnp.exp(bcf[pl.ds(i * c + c - 1, 1), :])
        decb = jnp.exp(bsb[pl.ds(ib * c, 1), :])
        for hi, hc in enumerate(heads):
            sf = stf[hi]
            of[rf, hc] += _dot_nt(qdf[rf, hc], sf)
            stf[hi] = sf * decf[:, hc] + uf[hi, i]
            sb = stb[hi]
            ob[rb, hc] += _dot_nt(qdb[rb, hc], sb)
            stb[hi] = sb * decb[:, hc] + ub[hi, ib]
        return carry

    lax.fori_loop(0, n, step, 0, unroll=min(8, n))
    gn = gn_ref[...]
    for hi, hc in enumerate(heads):
        if has_sout:
            sf_ref[hi] = stf[hi].T
            sb_ref[hi] = stb[hi].T
        o = of[:, hc] + ob[:, hc]
        o_ref[:, hc] = (_rms(o, gn) * _silu(g_ref[:, hc])).astype(o_ref.dtype)


def _gla(q, v, f0, f1, g, gnorm, s0f, s0b, j, rows):
    d = q.shape[1]
    has_s0 = s0f is not None
    if has_s0:
        t, nb, roff, nhb = rows.ds, rows.db, rows.n_ctx // rows.ds, 2
    else:
        t, nb, roff, nhb = rows.s, rows.b, 0, 4
    dv = d // B_HEADS
    assert dv == LANES and B_DK == LANES and B_HEADS % nhb == 0
    w = nhb * LANES
    tok = pl.BlockSpec((t, w), lambda b, h: (b + roff, h))
    in_specs = [tok] * 5 + [pl.BlockSpec((None, 1, dv), lambda b, h: (j, 0, 0))]
    args = [q, v, f0, f1, g, gnorm]
    if has_s0:
        sspec = pl.BlockSpec((None, None, nhb, B_DK, dv), lambda b, h: (b, j, h, 0, 0))
        in_specs += [sspec, sspec]
        args += [s0f, s0b]
    out_shape = [jax.ShapeDtypeStruct((nb * t, d), MXU)]
    out_specs = [pl.BlockSpec((t, w), lambda b, h: (b, h))]
    if not has_s0:
        ospec = pl.BlockSpec((None, nhb, B_DK, dv), lambda b, h: (b, h, 0, 0))
        out_shape += [jax.ShapeDtypeStruct((nb, B_HEADS, B_DK, dv), F32)] * 2
        out_specs += [ospec, ospec]
    scratch = ([pltpu.VMEM((t, w), MXU) for _ in range(5)]
               + [pltpu.VMEM((t, w), F32) for _ in range(4)]
               + [pltpu.VMEM((nhb, t // B_CHUNK, dv, B_DK), F32) for _ in range(2)]
               + [pltpu.VMEM((nhb, dv, B_DK), F32) for _ in range(2)])
    return pl.pallas_call(
        functools.partial(_gla_kernel, t=t, nhb=nhb, has_s0=has_s0, has_sout=not has_s0),
        out_shape=tuple(out_shape),
        grid=(nb, B_HEADS // nhb),
        in_specs=in_specs,
        out_specs=tuple(out_specs),
        scratch_shapes=scratch,
        compiler_params=_cparams(("parallel", "parallel"), 40),
        name="gla_lat" if has_s0 else "gla_ctx",
    )(*args)


def _attn_c_kernel(*refs, qc, nh, has_cache):
    qn_ref, qp_ref, ckv_ref, kpe_ref = refs[:4]
    pos = 4
    if has_cache:
        ckvc_ref, kpec_ref = refs[pos:pos + 2]
        pos += 2
    wuk_ref, wuv_ref, o_ref = refs[pos:pos + 3]
    scr = refs[pos + 3:]
    sq = qn_ref.shape[0]
    scale = (C_NOPE + C_ROPE) ** -0.5
    wuk = wuk_ref[...].astype(MXU)
    wuv = wuv_ref[...].astype(MXU)

    kw = C_NOPE + 2 * C_ROPE
    vw = C_VD + LANES

    def expand(ckv_r, kpe_r, kcat_r, vcat_r):
        ck = ckv_r[...].astype(MXU)
        kn = jnp.dot(ck, wuk, preferred_element_type=F32).astype(kcat_r.dtype)
        vn = jnp.dot(ck, wuv, preferred_element_type=F32).astype(vcat_r.dtype)
        kp = kpe_r[...].astype(kcat_r.dtype)
        ones = jnp.ones((ck.shape[0], LANES), vcat_r.dtype)
        for hh in range(nh):
            kcat_r[:, hh * kw:(hh + 1) * kw] = jnp.concatenate(
                [kn[:, hh * C_NOPE:(hh + 1) * C_NOPE], kp, kp], axis=1)
            vcat_r[:, hh * vw:(hh + 1) * vw] = jnp.concatenate(
                [vn[:, hh * C_VD:(hh + 1) * C_VD], ones], axis=1)

    groups = [scr[0:2]]
    expand(ckv_ref, kpe_ref, *scr[0:2])
    if has_cache:
        groups.append(scr[2:4])
        expand(ckvc_ref, kpec_ref, *scr[2:4])
    lo = lax.broadcasted_iota(jnp.int32, (1, LANES), 1) < C_ROPE

    def body(c, carry):
        r = pl.ds(pl.multiple_of(c * qc, qc), qc)
        for hh in range(nh):
            cols = slice(hh * C_NOPE, (hh + 1) * C_NOPE)
            qp = qp_ref[r, (hh // 2) * LANES:(hh // 2 + 1) * LANES]
            zero = jnp.zeros_like(qp)
            qph = jnp.where(lo, qp, zero) if hh % 2 == 0 else jnp.where(lo, zero, qp)
            q = jnp.concatenate([qn_ref[r, cols], qph], axis=1)
            acc, _ = _attend(q, [kc[:, hh * kw:(hh + 1) * kw] for kc, _ in groups],
                             [vc[:, hh * vw:(hh + 1) * vw] for _, vc in groups], scale=scale)
            o_ref[r, cols] = (acc[:, :C_VD] / acc[:, C_VD:]).astype(o_ref.dtype)
        return carry

    lax.fori_loop(0, sq // qc, body, 0, unroll=True)


def _attn_c(qn, qp, ckv, kpe, cache_ckv, cache_kpe, wuk, wuv, j, rows):
    has_cache = cache_ckv is not None
    if has_cache:
        t, nb, roff = rows.ds, rows.db, rows.n_ctx // rows.ds
    else:
        t, nb, roff = rows.s, rows.b, 0
    nh = 2 if has_cache else 8
    hw = nh * C_NOPE
    qc = _largest_tile(t, 256)
    in_specs = [pl.BlockSpec((t, hw), lambda b, g: (b + roff, g)),
                pl.BlockSpec((t, nh * C_ROPE), lambda b, g: (b + roff, g)),
                pl.BlockSpec((t, C_KVLORA), lambda b, g: (b + roff, 0)),
                pl.BlockSpec((t, C_ROPE), lambda b, g: (b + roff, 0))]
    args = [qn, qp, ckv, kpe]
    kcw, vcw = nh * (C_NOPE + 2 * C_ROPE), nh * (C_VD + LANES)
    scratch = [pltpu.VMEM((t, kcw), MXU), pltpu.VMEM((t, vcw), MXU)]
    if has_cache:
        p = cache_ckv.shape[2]
        in_specs += [pl.BlockSpec((None, None, p, C_KVLORA), lambda b, g: (b, j, 0, 0)),
                     pl.BlockSpec((None, None, p, C_ROPE), lambda b, g: (b, j, 0, 0))]
        args += [cache_ckv, cache_kpe]
        scratch += [pltpu.VMEM((p, kcw), MXU), pltpu.VMEM((p, vcw), MXU)]
    in_specs += [pl.BlockSpec((None, C_KVLORA, hw), lambda b, g: (j, 0, g))] * 2
    args += [wuk, wuv]
    return pl.pallas_call(
        functools.partial(_attn_c_kernel, qc=qc, nh=nh, has_cache=has_cache),
        out_shape=jax.ShapeDtypeStruct((nb * t, C_HEADS * C_VD), MXU),
        grid=(nb, C_HEADS // nh),
        in_specs=in_specs,
        out_specs=pl.BlockSpec((t, hw), lambda b, g: (b, g)),
        scratch_shapes=scratch,
        compiler_params=_cparams(("parallel", "parallel"), 32),
        name="attn_c_lat" if has_cache else "attn_c_ctx",
    )(*args)


def _attn_d_kernel(*refs, has_cache):
    q_ref, k_ref, v_ref = refs[:3]
    pos = 3
    if has_cache:
        kc_ref, vc_ref = refs[pos:pos + 2]
        pos += 2
    sink_ref, o_ref = refs[pos:pos + 2]
    k2, v2 = refs[pos + 2:pos + 4]
    if has_cache:
        k2c, v2c, bias = refs[pos + 4:pos + 7]
    t = q_ref.shape[0]
    qb = Q_BLOCK
    scale = D_DH ** -0.5
    r = D_HEADS // D_KV_HEADS
    win = min(t, qb + 2 * D_WINDOW)
    lo = lax.broadcasted_iota(jnp.int32, (1, LANES), 1) < D_DH
    rowblk = lax.broadcasted_iota(jnp.int32, (r * qb, 1), 0) // qb
    if has_cache:
        qi = lax.broadcasted_iota(jnp.int32, (r * qb, 1), 0) % qb
        kj = lax.broadcasted_iota(jnp.int32, (1, win), 1)
        for p in range(bias.shape[0]):
            bias[p] = jnp.where(jnp.abs(qi - kj + p * qb) <= D_WINDOW, 0.0, NEG)

    def stage(k_r, v_r, k2_r, v2_r, g):
        gs = slice(g * D_DH, (g + 1) * D_DH)
        kg, vg = k_r[:, gs].astype(k2_r.dtype), v_r[:, gs].astype(v2_r.dtype)
        k2_r[...] = jnp.concatenate([kg, kg], axis=1)
        v2_r[...] = _with_ones(jnp.concatenate([vg, vg], axis=1))

    for g in range(D_KV_HEADS):
        stage(k_ref, v_ref, k2, v2, g)
        if has_cache:
            stage(kc_ref, vc_ref, k2c, v2c, g)
        sink = jnp.zeros((r * qb, 1), F32)
        for e in range(r):
            h = g * r + e
            sink = jnp.where(rowblk == e, sink_ref[0:1, h:h + 1], sink)

        def body(c, carry, g=g, sink=sink):
            rq = pl.ds(pl.multiple_of(c * qb, qb), qb)
            tiles = []
            for e in range(r):
                cols = slice((g * r + e - e % 2) * D_DH, (g * r + e - e % 2 + 2) * D_DH)
                qp = q_ref[rq, cols] * scale
                zero = jnp.zeros_like(qp)
                tiles.append(jnp.where(lo, qp, zero) if e % 2 == 0 else jnp.where(lo, zero, qp))
            qs = jnp.concatenate(tiles, axis=0)
            if has_cache:
                ws = pl.multiple_of(jnp.clip(c * qb - D_WINDOW, 0, t - win), qb)
                rk = pl.ds(ws, win)
                parts = [_dot_nt(qs, k2c[...]), _dot_nt(qs, k2[rk, :]) + bias[(c * qb - ws) // qb]]
                vs = [v2c[...], v2[rk, :]]
            else:
                parts = [_dot_nt(qs, k2[...])]
                vs = [v2[...]]
            m = functools.reduce(jnp.maximum, [jnp.max(x, axis=-1, keepdims=True) for x in parts])
            m = jnp.maximum(m, sink)
            acc = functools.reduce(lambda a, b: a + b,
                                   [_dot(jnp.exp(x - m), vx) for x, vx in zip(parts, vs)])
            o = acc[:, :LANES] / (acc[:, LANES:] + jnp.exp(sink - m))
            for pr in range(r // 2):
                cols = slice((g * r + 2 * pr) * D_DH, (g * r + 2 * pr + 2) * D_DH)
                pair = jnp.where(lo, o[2 * pr * qb:(2 * pr + 1) * qb, :],
                                 o[(2 * pr + 1) * qb:(2 * pr + 2) * qb, :])
                o_ref[rq, cols] = pair.astype(o_ref.dtype)
            return carry

        lax.fori_loop(0, t // qb, body, 0, unroll=4)


def _attn_d(q, k, v, cache_k, cache_v, sink, j, rows):
    d = q.shape[1]
    kvw = D_KV_HEADS * D_DH
    has_cache = cache_k is not None
    if has_cache:
        t, nb, roff = rows.ds, rows.db, rows.n_ctx // rows.ds
    else:
        t, nb, roff = rows.s, rows.b, 0
    assert t % Q_BLOCK == 0
    in_specs = [pl.BlockSpec((t, d), lambda b: (b + roff, 0)),
                pl.BlockSpec((t, kvw), lambda b: (b + roff, 0)),
                pl.BlockSpec((t, kvw), lambda b: (b + roff, 0))]
    args = [q, k, v]
    scratch = [pltpu.VMEM((t, LANES), MXU), pltpu.VMEM((t, 2 * LANES), MXU)]
    if has_cache:
        p = cache_k.shape[2]
        cspec = pl.BlockSpec((None, None, p, kvw), lambda b: (b, j, 0, 0))
        in_specs += [cspec, cspec]
        args += [cache_k, cache_v]
        win = min(t, Q_BLOCK + 2 * D_WINDOW)
        n_pat = 3 if t > win else t // Q_BLOCK
        scratch += [pltpu.VMEM((p, LANES), MXU), pltpu.VMEM((p, 2 * LANES), MXU),
                    pltpu.VMEM((n_pat, D_HEADS // D_KV_HEADS * Q_BLOCK, win), F32)]
    in_specs.append(pl.BlockSpec((None, 1, D_HEADS), lambda b: (j, 0, 0)))
    args.append(sink)
    return pl.pallas_call(
        functools.partial(_attn_d_kernel, has_cache=has_cache),
        out_shape=jax.ShapeDtypeStruct((nb * t, d), MXU),
        grid=(nb,),
        in_specs=in_specs,
        out_specs=pl.BlockSpec((t, d), lambda b: (b, 0)),
        scratch_shapes=scratch,
        compiler_params=_cparams(("parallel",), 44),
        name="attn_d_lat" if has_cache else "attn_d_ctx",
    )(*args)


def kernel(x_prompt, x_sample, cache_a_k, cache_a_v, state_b_fwd, state_b_bwd, cache_c_ckv,
           cache_c_kpe, cache_d_k, cache_d_v, c, c_ctx, ada_w, ada_b, norm_mix_pre,
           norm_mix_post, norm_ffn_pre, norm_ffn_post, a_wq, a_wk, a_wv, a_wo, a_lambda,
           a_subln, b_wq, b_wi, b_wf, b_lower, b_wg, b_gnorm, b_wo, c_wdq, c_qnorm, c_wuq,
           c_wdkv, c_kvnorm, c_wuk, c_wuv, c_wo, d_wq, d_wk, d_wv, d_sink, d_wo, ffn_wg,
           ffn_wu, ffn_wd):
    b, s, d = x_prompt.shape
    db, ds, _ = x_sample.shape
    depth = ada_w.shape[0]
    rows = _Rows(b, s, db, ds, _largest_tile(math.gcd(b * s, ds), 1024))
    n_ctx = rows.n_ctx

    cond = jnp.zeros((rows.crows, d), F32).at[:db].set(c).at[db].set(c_ctx)
    mod = _ada(cond, ada_w, ada_b)
    g3 = lambda a: a.reshape(a.shape[0], 1, a.shape[1])
    n_mix_pre, n_mix_post = g3(norm_mix_pre), g3(norm_mix_post)
    n_ffn_pre, n_ffn_post = g3(norm_ffn_pre), g3(norm_ffn_post)
    tabs = _rope_tables(ds, rows.tm)
    rope_epi = lambda tn: (_make_epi_rope(tn), _rope_extra(tabs, rows))
    ns = 4

    x = (x_prompt.reshape(n_ctx, d), x_sample.reshape(rows.n_lat, d))
    h = _prep(x[0], x[1], n_mix_pre, mod, 0, rows)

    flat4 = lambda a: a.reshape(a.shape[:3] + (-1,))
    wo_b = {0: _cast_mxu(a_wo), 1: _cast_mxu(b_wo), 2: _cast_mxu(c_wo), 3: _cast_mxu(d_wo)}
    wd_b = _cast_mxu(ffn_wd)
    wf = b_wf.reshape((-1,) + b_wf.shape[2:])
    outs = {k_: [] for k_ in ("a_k", "a_v", "b_f", "b_b", "c_ckv", "c_kpe", "d_k", "d_v")}
    for i in range(depth):
        m, j = i % N_MIXERS, i // N_MIXERS
        if m == 0:
            tn = _largest_tile(a_wq.shape[2], 1024)
            epi, ext = rope_epi(tn)
            q, = _proj(h, [(a_wq, j)], epi, [(tn, MXU)], rows, tn, ext, nsub=ns, name="a_q")
            kv_outs = [(tn, MXU), (tn, F32, "ctx")]
            k, kc = _proj(h, [(a_wk, j)], _with_ctx_copy(epi, rows), kv_outs, rows, tn, ext,
                          nsub=ns, name="a_k")
            v, vc = _proj(h, [(a_wv, j)], _with_ctx_copy(_epi_store, rows), kv_outs, rows, tn,
                          nsub=ns, name="a_v")
            o = (_attn_a(q, k, v, None, None, a_lambda, g3(a_subln), i, j, rows),
                 _attn_a(q, k, v, flat4(cache_a_k), flat4(cache_a_v), a_lambda, g3(a_subln),
                         i, j, rows))
            outs["a_k"].append(kc.reshape(b, s, A_HEADS, 2 * A_DH))
            outs["a_v"].append(vc.reshape(b, s, A_HEADS, 2 * A_DH))
        elif m == 1:
            tn = _largest_tile(b_wq.shape[2], 1024)
            q, = _proj(h, [(b_wq, j)], _epi_silu, [(tn, F32)], rows, tn, nsub=ns, name="b_q")
            vi, = _proj(h, [(b_wi, j)], _epi_store, [(tn, MXU)], rows, tn, nsub=ns, name="b_i")
            g, = _proj(h, [(b_wg, j)], _epi_store, [(tn, F32)], rows, tn, nsub=ns, name="b_g")
            fs = []
            for dr in range(2):
                bl_spec = pl.BlockSpec((None, depth, tn), lambda jn, mm, dr=dr: (dr, 0, jn))
                f, = _proj(h, [(wf, 2 * j + dr)], _make_epi_forget(i, depth), [(tn, F32)], rows,
                           tn, [(jnp.swapaxes(b_lower, 0, 1), bl_spec)], nsub=ns,
                           name="b_f%d" % dr)
                fs.append(f)
            oc, sf, sb = _gla(q, vi, fs[0], fs[1], g, g3(b_gnorm), None, None, j, rows)
            ol, = _gla(q, vi, fs[0], fs[1], g, g3(b_gnorm), state_b_fwd, state_b_bwd, j, rows)
            o = (oc, ol)
            outs["b_f"].append(sf)
            outs["b_b"].append(sb)
        elif m == 2:
            nq = c_wdq.shape[2]
            cq, = _proj(h, [(c_wdq, j)], _make_epi_rmsnorm(), [(nq, MXU)], rows, nq,
                        [(g3(c_qnorm), pl.BlockSpec((None, 1, nq), lambda jn, mm: (j, 0, 0)))],
                        name="c_dq")
            nkv = c_wdkv.shape[2]
            ext = [(g3(c_kvnorm), pl.BlockSpec((None, 1, C_KVLORA), lambda jn, mm: (j, 0, 0)))]
            ext += _rope_extra(tabs, rows)
            ckv, kpe, ckv_c, kpe_c = _proj(
                h, [(c_wdkv, j)], _make_epi_ckv(rows),
                [(C_KVLORA, MXU), (C_ROPE, MXU), (C_KVLORA, F32, "ctx"), (C_ROPE, F32, "ctx")],
                rows, nkv, ext, name="c_dkv")
            wuq = c_wuq.reshape(c_wuq.shape[0], nq, C_HEADS, C_NOPE + C_ROPE)
            wuq_n = wuq[:, :, :, :C_NOPE].reshape(-1, nq, C_HEADS * C_NOPE)
            wuq_p = wuq[:, :, :, C_NOPE:].reshape(-1, nq, C_HEADS * C_ROPE)
            tn = _largest_tile(wuq_n.shape[2], 1024)
            qn, = _proj(cq, [(wuq_n, j)], _epi_store, [(tn, MXU)], rows, tn, name="c_qn")
            tnp = wuq_p.shape[2]
            epi, ext = rope_epi(tnp)
            qp, = _proj(cq, [(wuq_p, j)], epi, [(tnp, MXU)], rows, tnp, ext, name="c_qp")
            o = (_attn_c(qn, qp, ckv, kpe, None, None, c_wuk, c_wuv, j, rows),
                 _attn_c(qn, qp, ckv, kpe, cache_c_ckv, cache_c_kpe, c_wuk, c_wuv, j, rows))
            outs["c_ckv"].append(ckv_c.reshape(b, s, C_KVLORA))
            outs["c_kpe"].append(kpe_c.reshape(b, s, C_ROPE))
        else:
            tn = _largest_tile(d_wq.shape[2], 1024)
            epi, ext = rope_epi(tn)
            q, = _proj(h, [(d_wq, j)], epi, [(tn, MXU)], rows, tn, ext, nsub=ns, name="d_q")
            kvw = d_wk.shape[2]
            epi, ext = rope_epi(kvw)
            kv_outs = [(kvw, MXU), (kvw, F32, "ctx")]
            k, kc = _proj(h, [(d_wk, j)], _with_ctx_copy(epi, rows), kv_outs, rows, kvw, ext,
                          name="d_k")
            v, vc = _proj(h, [(d_wv, j)], _with_ctx_copy(_epi_store, rows), kv_outs, rows, kvw,
                          name="d_v")
            o = (_attn_d(q, k, v, None, None, g3(d_sink), j, rows),
                 _attn_d(q, k, v, flat4(cache_d_k), flat4(cache_d_v), g3(d_sink), j, rows))
            outs["d_k"].append(kc.reshape(b, s, D_KV_HEADS, D_DH))
            outs["d_v"].append(vc.reshape(b, s, D_KV_HEADS, D_DH))

        x, h = _resid(o, wo_b[m], j, x, mod, i, 2, n_mix_post, rows, nxt=(i, 3, n_ffn_pre),
                      nsub=4, name="mix_out")
        tf = _largest_tile(ffn_wg.shape[2], 512)
        a, = _proj(h, [(ffn_wg, i), (ffn_wu, i)], _epi_swiglu, [(tf, MXU)], rows, tf,
                   tm=_largest_tile(rows.mt, 2048), nsub=2, name="ffn_gu")
        nxt = (i + 1, 0, n_mix_pre) if i + 1 < depth else None
        x, h = _resid(a, wd_b, i, x, mod, i, 5, n_ffn_post, rows, nxt=nxt, name="ffn_down")

    stack = lambda lst: jnp.stack(lst, axis=1)
    return (x.reshape(b, s, d), h.reshape(db, ds, d),
            stack(outs["a_k"]), stack(outs["a_v"]), stack(outs["b_f"]), stack(outs["b_b"]),
            stack(outs["c_ckv"]), stack(outs["c_kpe"]), stack(outs["d_k"]), stack(outs["d_v"]))
```

```python
import functools
import math

import numpy as np
import jax
import jax.numpy as jnp
from jax import lax
from jax.experimental import pallas as pl
from jax.experimental.pallas import tpu as pltpu

EPS = 1e-6
ROPE_BASE = 10000.0
GRID_W = 64
NEG = -1e30
N_MIXERS = 4
A_HEADS, A_DH = 16, 64
B_HEADS, B_DK, B_CHUNK = 16, 128, 32
C_HEADS, C_NOPE, C_ROPE, C_VD = 16, 128, 64, 128
D_HEADS, D_KV_HEADS, D_DH, D_WINDOW, Q_BLOCK = 32, 4, 64, 128, 128

LANES = 128
SUBLANES = 8
VMEM_CAP_MIB = 56
MIN_SLAB_ROWS = 256

MXU = jnp.bfloat16
F32 = jnp.float32


def _cparams(sem, vmem_mib):
    return pltpu.CompilerParams(dimension_semantics=sem,
                                vmem_limit_bytes=min(vmem_mib, VMEM_CAP_MIB) << 20)


def _dot(a, b):
    return jnp.dot(a.astype(MXU), b.astype(MXU), preferred_element_type=F32)


def _dot_nt(a, b):
    return lax.dot_general(a.astype(MXU), b.astype(MXU), (((1,), (1,)), ((), ())),
                           preferred_element_type=F32)


def _dot_tn(a, b):
    return lax.dot_general(a.astype(MXU), b.astype(MXU), (((0,), (0,)), ((), ())),
                           preferred_element_type=F32)


def _sigmoid(x):
    return 1.0 / (1.0 + jnp.exp(-x))


def _silu(x):
    return x * _sigmoid(x)


def _rms(x, g):
    ms = jnp.mean(x * x, axis=-1, keepdims=True)
    return x * lax.rsqrt(ms + EPS) * g


def _largest_tile(n, pref):
    t = min(n, pref)
    while n % t:
        t -= SUBLANES
    assert t > 0 and n % t == 0
    return t


class _Rows:
    def __init__(self, b, s, db, ds, tm):
        self.b, self.s, self.db, self.ds = b, s, db, ds
        self.n_ctx, self.n_lat = b * s, db * ds
        self.mt = self.n_ctx + self.n_lat
        self.tm = tm
        assert self.n_ctx % tm == 0 and ds % tm == 0 and self.n_ctx % ds == 0
        self.nct = self.n_ctx // tm
        self.ntiles = self.mt // tm
        self.crows = -(-(db + 1) // SUBLANES) * SUBLANES

    def cond_row(self, i):
        lat = ((i - self.nct) * self.tm) // self.ds
        return jnp.where(i < self.nct, self.db, lat)

    def pos_block(self, i):
        lat = ((i - self.nct) * self.tm % self.ds) // self.tm
        return jnp.where(i < self.nct, self.ds // self.tm, lat)


def _rope_tables(n_tok, n_identity):
    nf = 16
    inv = (ROPE_BASE ** (-np.arange(nf, dtype=np.float32) / nf)).astype(np.float32)
    t = np.arange(n_tok)
    row, col = (t // GRID_W).astype(np.float32), (t % GRID_W).astype(np.float32)
    lane = np.arange(64)
    pos = np.where(lane[None, :] < 32, row[:, None], col[:, None]).astype(np.float32)
    ang = (pos * inv[lane % nf][None, :]).astype(np.float32)
    cos, sin = np.cos(ang).astype(np.float32), np.sin(ang).astype(np.float32)
    first = (lane % 32) < 16
    sa = np.where(first[None, :], -sin, 0.0).astype(np.float32)
    sb = np.where(first[None, :], 0.0, sin).astype(np.float32)
    def table(a, fill):
        a = np.concatenate([a, np.full((n_identity, 64), fill, np.float32)], axis=0)
        return jnp.asarray(np.concatenate([a, a], axis=1))

    return table(cos, 1.0), table(sa, 0.0), table(sb, 0.0)


def _rope128(y, cos, sa, sb):
    return (y * cos + pltpu.roll(y, LANES - 16, 1) * sa + pltpu.roll(y, 16, 1) * sb)


def _ada_kernel(c_ref, w_ref, b_ref, o_ref):
    o_ref[...] = _dot(_silu(c_ref[...]), w_ref[...]) + b_ref[...]


def _ada(cond, ada_w, ada_b):
    depth, d, n = ada_w.shape
    cr = cond.shape[0]
    tn = _largest_tile(n, 1024)
    return pl.pallas_call(
        _ada_kernel,
        out_shape=jax.ShapeDtypeStruct((depth, cr, n), F32),
        grid=(depth, n // tn),
        in_specs=[pl.BlockSpec((cr, d), lambda l, j: (0, 0)),
                  pl.BlockSpec((None, d, tn), lambda l, j: (l, 0, j)),
                  pl.BlockSpec((None, 1, tn), lambda l, j: (l, 0, j))],
        out_specs=pl.BlockSpec((None, cr, tn), lambda l, j: (l, 0, j)),
        compiler_params=_cparams(("parallel", "parallel"), 40),
        name="ada_mod",
    )(cond, ada_w, ada_b.reshape(depth, 1, n))


def _prep_kernel(xp_ref, xs_ref, g_ref, mod_ref, h_ref, *, rows, d):
    i = pl.program_id(0)
    x = jnp.where(i < rows.nct, xp_ref[...], xs_ref[...])
    cr = rows.cond_row(i)
    shift = mod_ref[pl.ds(cr, 1), pl.ds(0, d)]
    scale = mod_ref[pl.ds(cr, 1), pl.ds(d, d)]
    h_ref[...] = (_rms(x, g_ref[...]) * (1.0 + scale) + shift).astype(h_ref.dtype)


def _prep(xp, xs, g, mod, layer, rows):
    d = xp.shape[1]
    rows = _Rows(rows.b, rows.s, rows.db, rows.ds, _largest_tile(rows.tm, 512))
    tm, nct = rows.tm, rows.nct
    return pl.pallas_call(
        functools.partial(_prep_kernel, rows=rows, d=d),
        out_shape=jax.ShapeDtypeStruct((rows.mt, d), MXU),
        grid=(rows.ntiles,),
        in_specs=[pl.BlockSpec((tm, d), lambda i: (jnp.minimum(i, nct - 1), 0)),
                  pl.BlockSpec((tm, d), lambda i: (jnp.maximum(i - nct, 0), 0)),
                  pl.BlockSpec((None, 1, d), lambda i: (layer, 0, 0)),
                  pl.BlockSpec((None, rows.crows, mod.shape[2]), lambda i: (layer, 0, 0))],
        out_specs=pl.BlockSpec((tm, d), lambda i: (i, 0)),
        compiler_params=_cparams(("parallel",), 32 * tm * d // (1 << 20) + 8),
        name="prep",
    )(xp, xs, g, mod)


def _proj_kernel(*refs, n_w, n_extra, n_out, nsub, epi, fin):
    h_ref = refs[0]
    w_refs = refs[1:1 + n_w]
    extra = refs[1 + n_w:1 + n_w + n_extra]
    outs = refs[1 + n_w + n_extra:1 + n_w + n_extra + n_out]
    wb_refs = refs[1 + n_w + n_extra + n_out:]
    m = pl.program_id(1)

    @pl.when(m == 0)
    def _():
        for w_ref, wb_ref in zip(w_refs, wb_refs):
            wb_ref[...] = w_ref[...].astype(wb_ref.dtype)

    rs = h_ref.shape[0] // nsub
    slabs = [slice(s * rs, (s + 1) * rs) for s in range(nsub)]
    ys_all = [[jnp.dot(h_ref[r, :], wb_ref[...], preferred_element_type=F32) for wb_ref in wb_refs]
              for r in slabs]
    for r, ys in zip(slabs, ys_all):
        epi(ys, m, extra, outs, r)
    if fin is not None:
        fin(ys_all, slabs, m, extra, outs)


def _proj(h, ws, epi, outs, rows, tn, extra=(), tm=None, nsub=1, name="proj"):
    mt, k = h.shape
    n = ws[0][0].shape[2]
    assert n % tn == 0
    tm = rows.tm if tm is None else tm
    assert mt % tm == 0
    nj = n // tn
    in_specs = [pl.BlockSpec((tm, k), lambda j, m: (m, 0))]
    in_specs += [pl.BlockSpec((None, k, tn), lambda j, m, li=li: (li, 0, j)) for _, li in ws]
    in_specs += [s for _, s in extra]
    nct = rows.nct
    ctx_only = [len(o) > 2 and o[2] == "ctx" for o in outs]
    outs = [o[:2] for o in outs]
    out_shape = tuple(jax.ShapeDtypeStruct((rows.n_ctx if co else mt, c * nj), dt)
                      for (c, dt), co in zip(outs, ctx_only))
    out_specs = tuple(pl.BlockSpec((tm, c), (lambda j, m: (jnp.minimum(m, nct - 1), j)) if co
                                   else (lambda j, m: (m, j)))
                      for (c, _), co in zip(outs, ctx_only))
    assert not any(ctx_only) or tm == rows.tm
    out_bytes = sum(tm * c * jnp.dtype(dt).itemsize for c, dt in outs)
    vmem = (2 * tm * k * 2 + len(ws) * (2 * k * tn * 4 + k * tn * 2) + 2 * out_bytes
            + len(ws) * tm * tn * 8) // (1 << 20) + 6
    epi, fin = epi if isinstance(epi, tuple) else (epi, None)
    res = pl.pallas_call(
        functools.partial(_proj_kernel, n_w=len(ws), n_extra=len(extra), n_out=len(outs),
                          nsub=nsub, epi=epi, fin=fin),
        out_shape=out_shape,
        grid=(nj, mt // tm),
        in_specs=in_specs,
        out_specs=out_specs,
        scratch_shapes=[pltpu.VMEM((k, tn), MXU) for _ in ws],
        compiler_params=_cparams(("parallel", "arbitrary"), vmem),
        name=name,
    )(h, *[w for w, _ in ws], *[a for a, _ in extra])
    return res


def _epi_store(ys, m, extra, outs, r):
    outs[0][r, :] = ys[0].astype(outs[0].dtype)


def _epi_silu(ys, m, extra, outs, r):
    outs[0][r, :] = _silu(ys[0]).astype(outs[0].dtype)


def _epi_swiglu(ys, m, extra, outs, r):
    outs[0][r, :] = (_silu(ys[0]) * ys[1]).astype(outs[0].dtype)


def _make_epi_rope(tn):
    def epi(ys, m, extra, outs, r):
        cos_ref, sa_ref, sb_ref = extra
        y, o_ref = ys[0], outs[0]
        cos, sa, sb = cos_ref[r, :], sa_ref[r, :], sb_ref[r, :]
        for c in range(tn // LANES):
            sl = slice(c * LANES, (c + 1) * LANES)
            o_ref[r, sl] = _rope128(y[:, sl], cos, sa, sb).astype(o_ref.dtype)
    return epi


def _with_ctx_copy(epi, rows):
    def fin(ys_all, slabs, m, extra, outs):
        @pl.when(m < rows.nct)
        def _():
            for r, ys in zip(slabs, ys_all):
                outs[-1][r, :] = ys[0]
    return epi, fin


def _rope_extra(tabs, rows):
    tm = rows.tm
    spec = pl.BlockSpec((tm, LANES), lambda j, m: (rows.pos_block(m), 0))
    return [(t, spec) for t in tabs]


def _make_epi_forget(layer_idx, depth):
    def epi(ys, m, extra, outs, r):
        bl = extra[0][...]
        e = jnp.exp(bl - jnp.max(bl, axis=0, keepdims=True))
        p = e / jnp.sum(e, axis=0, keepdims=True)
        lb = jnp.sum(p[0:layer_idx + 1], axis=0, keepdims=True) - p[0:1]
        outs[0][r, :] = lb + (1.0 - lb) * _sigmoid(ys[0])
    return epi


def _make_epi_rmsnorm():
    def epi(ys, m, extra, outs, r):
        outs[0][r, :] = _rms(ys[0], extra[0][...]).astype(outs[0].dtype)
    return epi


def _make_epi_ckv(rows):
    def epi(ys, m, extra, outs, r):
        g_ref, cos_ref, sa_ref, sb_ref = extra
        ckv_all, kpe_all, ckv_ctx, kpe_ctx = outs
        y = ys[0]
        ckv = _rms(y[:, :C_KVLORA], g_ref[...])
        ckv_all[r, :] = ckv.astype(ckv_all.dtype)
        kpe = y[:, C_KVLORA:C_KVLORA + C_ROPE]

        @pl.when(m < rows.nct)
        def _():
            kpe_all[r, :] = kpe.astype(kpe_all.dtype)
            ckv_ctx[r, :] = ckv
            kpe_ctx[r, :] = kpe

        @pl.when(m >= rows.nct)
        def _():
            k2 = jnp.concatenate([kpe, kpe], axis=1)
            kr = _rope128(k2, cos_ref[r, :], sa_ref[r, :], sb_ref[r, :])
            kpe_all[r, :] = kr[:, :C_ROPE].astype(kpe_all.dtype)
    return epi


C_KVLORA = 256


def _cast_kernel(w_ref, o_ref):
    o_ref[...] = w_ref[...].astype(o_ref.dtype)


def _cast_mxu(w):
    l, k, n = w.shape
    tr = _largest_tile(k, 512)
    return pl.pallas_call(
        _cast_kernel,
        out_shape=jax.ShapeDtypeStruct(w.shape, MXU),
        grid=(l, k // tr),
        in_specs=[pl.BlockSpec((None, tr, n), lambda i, r: (i, r, 0))],
        out_specs=pl.BlockSpec((None, tr, n), lambda i, r: (i, r, 0)),
        compiler_params=_cparams(("parallel", "parallel"), 12 * tr * n // (1 << 20) + 8),
        name="cast_w",
    )(w)


def _resid_kernel(*refs, rows, d, nsub, gate_chunk, next_chunk, split_a, split_x):
    refs = list(refs)
    a_refs = [refs.pop(0) for _ in range(2 if split_a else 1)]
    w_ref = refs.pop(0)
    x_refs = [refs.pop(0) for _ in range(2 if split_x else 1)]
    modc_ref, gpost_ref = refs.pop(0), refs.pop(0)
    if next_chunk is not None:
        modn_ref, gpre_ref = refs.pop(0), refs.pop(0)
        xo_ref, ho_ref = refs
    else:
        xo_ref, xl_ref = refs
    i = pl.program_id(0)

    def rows_of(parts, r):
        if len(parts) == 1:
            return parts[0][r, :]
        return jnp.where(i < rows.nct, parts[0][r, :], parts[1][r, :])

    cr = rows.cond_row(i)
    gate = modc_ref[pl.ds(cr, 1), pl.ds(gate_chunk * d, d)] * gpost_ref[...]
    if next_chunk is not None:
        shift = modn_ref[pl.ds(cr, 1), pl.ds(next_chunk * d, d)]
        scale = (1.0 + modn_ref[pl.ds(cr, 1), pl.ds((next_chunk + 1) * d, d)]) * gpre_ref[...]
    w = w_ref[...]
    rs = a_refs[0].shape[0] // nsub
    slabs = [slice(s * rs, (s + 1) * rs) for s in range(nsub)]
    ys = [jnp.dot(rows_of(a_refs, r), w, preferred_element_type=F32) for r in slabs]
    for r, y in zip(slabs, ys):
        yn = y * lax.rsqrt(jnp.mean(y * y, axis=-1, keepdims=True) + EPS)
        xn = rows_of(x_refs, r) + yn * gate
        if next_chunk is not None:
            xo_ref[r, :] = xn
            hn = xn * lax.rsqrt(jnp.mean(xn * xn, axis=-1, keepdims=True) + EPS)
            ho_ref[r, :] = (hn * scale + shift).astype(ho_ref.dtype)
        else:
            @pl.when(i < rows.nct)
            def _(xn=xn, r=r):
                xo_ref[r, :] = xn

            @pl.when(i >= rows.nct)
            def _(xn=xn, r=r):
                xl_ref[r, :] = xn


def _resid(a, w, li, x, mod, layer, gate_chunk, gpost, rows, nxt=None, name="resid"):
    split_a = isinstance(a, (tuple, list))
    a_parts = list(a) if split_a else [a]
    mt, kdim = rows.mt, a_parts[0].shape[1]
    d = w.shape[2]
    n_x = 2 if isinstance(x, (tuple, list)) else 1
    row_bytes = len(a_parts) * 2 * kdim * 2 + d * (8 * n_x + 8 + 4 + 12)
    budget = (VMEM_CAP_MIB - 6 << 20) - kdim * d * 2
    tm = _largest_tile(rows.tm, max(SUBLANES, 1 << int(math.log2(budget // row_bytes))))
    nsub = max(1, tm // MIN_SLAB_ROWS)
    sub = _Rows(rows.b, rows.s, rows.db, rows.ds, tm)
    nct = sub.nct
    mspec = lambda l: pl.BlockSpec((None, rows.crows, mod.shape[2]), lambda i: (l, 0, 0))
    gspec = lambda l: pl.BlockSpec((None, 1, d), lambda i: (l, 0, 0))
    def row_specs(width, split):
        if split:
            return [pl.BlockSpec((tm, width), lambda i: (jnp.minimum(i, nct - 1), 0)),
                    pl.BlockSpec((tm, width), lambda i: (jnp.maximum(i - nct, 0), 0))]
        return [pl.BlockSpec((tm, width), lambda i: (i, 0))]

    split_x = isinstance(x, (tuple, list))
    x_parts = list(x) if split_x else [x]
    in_specs = row_specs(kdim, split_a) + [
        pl.BlockSpec((None, kdim, d), lambda i: (li, 0, 0), pipeline_mode=pl.Buffered(1))
    ] + row_specs(d, split_x) + [mspec(layer), gspec(layer)]
    args = a_parts + [w] + x_parts + [mod, gpost]
    if nxt is not None:
        nl, nchunk, gpre = nxt
        in_specs += [mspec(nl), gspec(nl)]
        args += [mod, gpre]
        out_shape = [jax.ShapeDtypeStruct((mt, d), F32), jax.ShapeDtypeStruct((mt, d), MXU)]
        out_specs = [pl.BlockSpec((tm, d), lambda i: (i, 0))] * 2
    else:
        nchunk = None
        out_shape = [jax.ShapeDtypeStruct((rows.n_ctx, d), F32),
                     jax.ShapeDtypeStruct((rows.n_lat, d), F32)]
        out_specs = [pl.BlockSpec((tm, d), lambda i: (jnp.minimum(i, nct - 1), 0)),
                     pl.BlockSpec((tm, d), lambda i: (jnp.maximum(i - nct, 0), 0))]
    vmem = (kdim * d * 2 + tm * row_bytes) // (1 << 20) + 8
    res = pl.pallas_call(
        functools.partial(_resid_kernel, rows=sub, d=d, nsub=nsub, gate_chunk=gate_chunk,
                          next_chunk=nchunk, split_a=split_a, split_x=split_x),
        out_shape=tuple(out_shape),
        grid=(mt // tm,),
        in_specs=in_specs,
        out_specs=tuple(out_specs),
        compiler_params=_cparams(("arbitrary",), vmem),
        name=name,
    )(*args)
    return res


def _with_ones(v):
    return jnp.concatenate([v, jnp.ones((v.shape[0], LANES), v.dtype)], axis=1)


def _attend(q, keys, vaugs, scale=None, masks=None):
    parts = [_dot_nt(q, kk) for kk in keys]
    if masks is not None:
        parts = [s if mk is None else jnp.where(mk, s, NEG) for s, mk in zip(parts, masks)]
    if scale is not None:
        parts = [s * scale for s in parts]
    m = functools.reduce(jnp.maximum, [jnp.max(s, axis=-1, keepdims=True) for s in parts])
    acc = functools.reduce(lambda a, b: a + b,
                           [_dot(jnp.exp(s - m), va) for s, va in zip(parts, vaugs)])
    return acc, m


def _attn_a_kernel(*refs, hb, qc, lam_init, has_cache):
    q_ref, k_ref, v_ref = refs[:3]
    pos = 3
    if has_cache:
        kc_ref, vc_ref = refs[pos:pos + 2]
        pos += 2
    lam_ref, sub_ref, o_ref = refs[pos:pos + 3]
    sq = q_ref.shape[0]
    dv = 2 * A_DH
    scale = A_DH ** -0.5
    lp = lam_ref[...]
    lam = (jnp.exp(jnp.sum(lp[0:1] * lp[1:2], axis=1, keepdims=True))
           - jnp.exp(jnp.sum(lp[2:3] * lp[3:4], axis=1, keepdims=True)) + lam_init)
    lo = lax.broadcasted_iota(jnp.int32, (1, dv), 1) < A_DH
    sub = sub_ref[...] * (1.0 - lam_init)
    for h in range(hb):
        cols = slice(h * dv, (h + 1) * dv)
        keys = [k_ref[:, cols].astype(MXU)]
        vals = [_with_ones(v_ref[:, cols].astype(MXU))]
        if has_cache:
            keys.append(kc_ref[:, cols].astype(MXU))
            vals.append(_with_ones(vc_ref[:, cols].astype(MXU)))

        def body(c, carry, cols=cols, keys=keys, vals=vals):
            r = pl.ds(pl.multiple_of(c * qc, qc), qc)
            q = q_ref[r, cols] * scale
            zero = jnp.zeros_like(q)
            a1, _ = _attend(jnp.where(lo, q, zero), keys, vals)
            a2, _ = _attend(jnp.where(lo, zero, q), keys, vals)
            o = a1[:, :dv] / a1[:, dv:] - lam * (a2[:, :dv] / a2[:, dv:])
            ms = jnp.mean(o * o, axis=-1, keepdims=True)
            o_ref[r, cols] = (o * lax.rsqrt(ms + EPS) * sub).astype(o_ref.dtype)
            return carry

        lax.fori_loop(0, sq // qc, body, 0, unroll=True)


def _attn_a(q, k, v, cache_k, cache_v, lam_p, subln, layer_idx, j, rows):
    d = q.shape[1]
    lam_init = 0.8 - 0.6 * math.exp(-0.3 * layer_idx)
    has_cache = cache_k is not None
    if has_cache:
        t, nb, roff = rows.ds, rows.db, rows.n_ctx // rows.ds
        hb = 4
    else:
        t, nb, roff = rows.s, rows.b, 0
        hb = A_HEADS
    cw = hb * 2 * A_DH
    qc = _largest_tile(t, 256)
    tok = lambda b, g: (b + roff, g)
    in_specs = [pl.BlockSpec((t, cw), tok)] * 3
    args = [q, k, v]
    if has_cache:
        p = cache_k.shape[2]
        cspec = pl.BlockSpec((None, None, p, cw), lambda b, g: (b, j, 0, g))
        in_specs += [cspec, cspec]
        args += [cache_k, cache_v]
    in_specs += [pl.BlockSpec((None, 4, A_DH), lambda b, g: (j, 0, 0)),
                 pl.BlockSpec((None, 1, 2 * A_DH), lambda b, g: (j, 0, 0))]
    args += [lam_p, subln]
    return pl.pallas_call(
        functools.partial(_attn_a_kernel, hb=hb, qc=qc, lam_init=lam_init, has_cache=has_cache),
        out_shape=jax.ShapeDtypeStruct((nb * t, d), MXU),
        grid=(nb, d // cw),
        in_specs=in_specs,
        out_specs=pl.BlockSpec((t, cw), lambda b, g: (b, g)),
        compiler_params=_cparams(("parallel", "parallel"), 40),
        name="attn_a_lat" if has_cache else "attn_a_ctx",
    )(*args)


def _gla_kernel(*refs, t, nhb, has_s0, has_sout):
    q_ref, v_ref, f0_ref, f1_ref, g_ref, gn_ref = refs[:6]
    pos = 6
    if has_s0:
        s0f_ref, s0b_ref = refs[pos:pos + 2]
        pos += 2
    o_ref = refs[pos]
    pos += 1
    if has_sout:
        sf_ref, sb_ref = refs[pos:pos + 2]
        pos += 2
    qdf, kif, qdb, kib, vb, bcf, bsb, of, ob, uf, ub, stf, stb = refs[pos:]
    c = B_CHUNK
    n = t // c
    heads = [slice(hi * LANES, (hi + 1) * LANES) for hi in range(nhb)]
    r32 = lax.broadcasted_iota(jnp.int32, (t, 1), 0) % c

    def prefix(x):
        s = 1
        while s < c:
            x = x + jnp.where(r32 >= s, pltpu.roll(x, s, 0), 0.0)
            s *= 2
        return x

    def suffix(x):
        s = 1
        while s < c:
            x = x + jnp.where(r32 < c - s, pltpu.roll(x, t - s, 0), 0.0)
            s *= 2
        return x

    q = q_ref[...]
    vb[...] = v_ref[...].astype(vb.dtype)
    f = f0_ref[...]
    bc = prefix(jnp.log(f))
    bcf[...] = bc
    qdf[...] = (q * jnp.exp(bc)).astype(qdf.dtype)
    kif[...] = ((1.0 - f) * jnp.exp(-bc)).astype(kif.dtype)
    f = f1_ref[...]
    bs = suffix(jnp.log(f))
    bsb[...] = bs
    qdb[...] = (q * jnp.exp(bs)).astype(qdb.dtype)
    kib[...] = ((1.0 - f) * jnp.exp(-bs)).astype(kib.dtype)

    blk = _largest_tile(t, 256)
    ti = lax.broadcasted_iota(jnp.int32, (blk, blk), 0)
    si = lax.broadcasted_iota(jnp.int32, (blk, blk), 1)
    same = (ti // c) == (si // c)
    lower, upper = same & (ti >= si), same & (ti <= si)
    for bi in range(t // blk):
        rb = slice(bi * blk, (bi + 1) * blk)
        for hc in heads:
            vv = vb[rb, hc]
            of[rb, hc] = _dot(jnp.where(lower, _dot_nt(qdf[rb, hc], kif[rb, hc]), 0.0), vv)
            ob[rb, hc] = _dot(jnp.where(upper, _dot_nt(qdb[rb, hc], kib[rb, hc]), 0.0), vv)

    def incr(i, carry):
        r = pl.ds(pl.multiple_of(i * c, c), c)
        vv = vb[r, :]
        bc_c, bs_c = bcf[r, :], bsb[r, :]
        kef = (1.0 - f0_ref[r, :]) * jnp.exp(bc_c[c - 1:c, :] - bc_c)
        keb = (1.0 - f1_ref[r, :]) * jnp.exp(bs_c[0:1, :] - bs_c)
        for hi, hc in enumerate(heads):
            uf[hi, i] = _dot_tn(vv[:, hc], kef[:, hc])
            ub[hi, i] = _dot_tn(vv[:, hc], keb[:, hc])
        return carry

    lax.fori_loop(0, n, incr, 0, unroll=min(8, n))

    for hi in range(nhb):
        if has_s0:
            stf[hi] = s0f_ref[hi].T
            stb[hi] = s0b_ref[hi].T
        else:
            stf[hi] = jnp.zeros((B_DK, B_DK), F32)
            stb[hi] = jnp.zeros((B_DK, B_DK), F32)

    def step(i, carry):
        rf = pl.ds(pl.multiple_of(i * c, c), c)
        ib = n - 1 - i
        rb = pl.ds(pl.multiple_of(ib * c, c), c)
        decf = jnp.exp(bcf[pl.ds(i * c + c - 1, 1), :])
        decb = jnp.exp(bsb[pl.ds(ib * c, 1), :])
        for hi, hc in enumerate(heads):
            sf = stf[hi]
            of[rf, hc] += _dot_nt(qdf[rf, hc], sf)
            stf[hi] = sf * decf[:, hc] + uf[hi, i]
            sb = stb[hi]
            ob[rb, hc] += _dot_nt(qdb[rb, hc], sb)
            stb[hi] = sb * decb[:, hc] + ub[hi, ib]
        return carry

    lax.fori_loop(0, n, step, 0, unroll=min(8, n))
    gn = gn_ref[...]
    for hi, hc in enumerate(heads):
        if has_sout:
            sf_ref[hi] = stf[hi].T
            sb_ref[hi] = stb[hi].T
        o = of[:, hc] + ob[:, hc]
        o_ref[:, hc] = (_rms(o, gn) * _silu(g_ref[:, hc])).astype(o_ref.dtype)


def _gla(q, v, f0, f1, g, gnorm, s0f, s0b, j, rows):
    d = q.shape[1]
    has_s0 = s0f is not None
    if has_s0:
        t, nb, roff, nhb = rows.ds, rows.db, rows.n_ctx // rows.ds, 2
    else:
        t, nb, roff, nhb = rows.s, rows.b, 0, 8
    dv = d // B_HEADS
    assert dv == LANES and B_DK == LANES and B_HEADS % nhb == 0
    w = nhb * LANES
    tok = pl.BlockSpec((t, w), lambda b, h: (b + roff, h))
    in_specs = [tok] * 5 + [pl.BlockSpec((None, 1, dv), lambda b, h: (j, 0, 0))]
    args = [q, v, f0, f1, g, gnorm]
    if has_s0:
        sspec = pl.BlockSpec((None, None, nhb, B_DK, dv), lambda b, h: (b, j, h, 0, 0))
        in_specs += [sspec, sspec]
        args += [s0f, s0b]
    out_shape = [jax.ShapeDtypeStruct((nb * t, d), MXU)]
    out_specs = [pl.BlockSpec((t, w), lambda b, h: (b, h))]
    if not has_s0:
        ospec = pl.BlockSpec((None, nhb, B_DK, dv), lambda b, h: (b, h, 0, 0))
        out_shape += [jax.ShapeDtypeStruct((nb, B_HEADS, B_DK, dv), F32)] * 2
        out_specs += [ospec, ospec]
    scratch = ([pltpu.VMEM((t, w), MXU) for _ in range(5)]
               + [pltpu.VMEM((t, w), F32) for _ in range(4)]
               + [pltpu.VMEM((nhb, t // B_CHUNK, dv, B_DK), F32) for _ in range(2)]
               + [pltpu.VMEM((nhb, dv, B_DK), F32) for _ in range(2)])
    return pl.pallas_call(
        functools.partial(_gla_kernel, t=t, nhb=nhb, has_s0=has_s0, has_sout=not has_s0),
        out_shape=tuple(out_shape),
        grid=(nb, B_HEADS // nhb),
        in_specs=in_specs,
        out_specs=tuple(out_specs),
        scratch_shapes=scratch,
        compiler_params=_cparams(("parallel", "parallel"), 40),
        name="gla_lat" if has_s0 else "gla_ctx",
    )(*args)


def _attn_c_kernel(*refs, qc, nh, has_cache):
    qn_ref, qp_ref, ckv_ref, kpe_ref = refs[:4]
    pos = 4
    if has_cache:
        ckvc_ref, kpec_ref = refs[pos:pos + 2]
        pos += 2
    wuk_ref, wuv_ref, o_ref = refs[pos:pos + 3]
    scr = refs[pos + 3:]
    sq = qn_ref.shape[0]
    scale = (C_NOPE + C_ROPE) ** -0.5
    wuk = wuk_ref[...].astype(MXU)
    wuv = wuv_ref[...].astype(MXU)

    kw = C_NOPE + 2 * C_ROPE
    vw = C_VD + LANES

    def expand(ckv_r, kpe_r, kcat_r, vcat_r):
        ck = ckv_r[...].astype(MXU)
        kn = jnp.dot(ck, wuk, preferred_element_type=F32).astype(kcat_r.dtype)
        vn = jnp.dot(ck, wuv, preferred_element_type=F32).astype(vcat_r.dtype)
        kp = kpe_r[...].astype(kcat_r.dtype)
        ones = jnp.ones((ck.shape[0], LANES), vcat_r.dtype)
        for hh in range(nh):
            kcat_r[:, hh * kw:(hh + 1) * kw] = jnp.concatenate(
                [kn[:, hh * C_NOPE:(hh + 1) * C_NOPE], kp, kp], axis=1)
            vcat_r[:, hh * vw:(hh + 1) * vw] = jnp.concatenate(
                [vn[:, hh * C_VD:(hh + 1) * C_VD], ones], axis=1)

    groups = [scr[0:2]]
    expand(ckv_ref, kpe_ref, *scr[0:2])
    if has_cache:
        groups.append(scr[2:4])
        expand(ckvc_ref, kpec_ref, *scr[2:4])
    lo = lax.broadcasted_iota(jnp.int32, (1, LANES), 1) < C_ROPE

    def body(c, carry):
        r = pl.ds(pl.multiple_of(c * qc, qc), qc)
        for hh in range(nh):
            cols = slice(hh * C_NOPE, (hh + 1) * C_NOPE)
            qp = qp_ref[r, (hh // 2) * LANES:(hh // 2 + 1) * LANES]
            zero = jnp.zeros_like(qp)
            qph = jnp.where(lo, qp, zero) if hh % 2 == 0 else jnp.where(lo, zero, qp)
            q = jnp.concatenate([qn_ref[r, cols], qph], axis=1)
            acc, _ = _attend(q, [kc[:, hh * kw:(hh + 1) * kw] for kc, _ in groups],
                             [vc[:, hh * vw:(hh + 1) * vw] for _, vc in groups], scale=scale)
            o_ref[r, cols] = (acc[:, :C_VD] / acc[:, C_VD:]).astype(o_ref.dtype)
        return carry

    lax.fori_loop(0, sq // qc, body, 0, unroll=True)


def _attn_c(qn, qp, ckv, kpe, cache_ckv, cache_kpe, wuk, wuv, j, rows):
    has_cache = cache_ckv is not None
    if has_cache:
        t, nb, roff = rows.ds, rows.db, rows.n_ctx // rows.ds
    else:
        t, nb, roff = rows.s, rows.b, 0
    nh = 2 if has_cache else 8
    hw = nh * C_NOPE
    qc = _largest_tile(t, 256)
    in_specs = [pl.BlockSpec((t, hw), lambda b, g: (b + roff, g)),
                pl.BlockSpec((t, nh * C_ROPE), lambda b, g: (b + roff, g)),
                pl.BlockSpec((t, C_KVLORA), lambda b, g: (b + roff, 0)),
                pl.BlockSpec((t, C_ROPE), lambda b, g: (b + roff, 0))]
    args = [qn, qp, ckv, kpe]
    kcw, vcw = nh * (C_NOPE + 2 * C_ROPE), nh * (C_VD + LANES)
    scratch = [pltpu.VMEM((t, kcw), MXU), pltpu.VMEM((t, vcw), MXU)]
    if has_cache:
        p = cache_ckv.shape[2]
        in_specs += [pl.BlockSpec((None, None, p, C_KVLORA), lambda b, g: (b, j, 0, 0)),
                     pl.BlockSpec((None, None, p, C_ROPE), lambda b, g: (b, j, 0, 0))]
        args += [cache_ckv, cache_kpe]
        scratch += [pltpu.VMEM((p, kcw), MXU), pltpu.VMEM((p, vcw), MXU)]
    in_specs += [pl.BlockSpec((None, C_KVLORA, hw), lambda b, g: (j, 0, g))] * 2
    args += [wuk, wuv]
    return pl.pallas_call(
        functools.partial(_attn_c_kernel, qc=qc, nh=nh, has_cache=has_cache),
        out_shape=jax.ShapeDtypeStruct((nb * t, C_HEADS * C_VD), MXU),
        grid=(nb, C_HEADS // nh),
        in_specs=in_specs,
        out_specs=pl.BlockSpec((t, hw), lambda b, g: (b, g)),
        scratch_shapes=scratch,
        compiler_params=_cparams(("parallel", "parallel"), 32),
        name="attn_c_lat" if has_cache else "attn_c_ctx",
    )(*args)


def _attn_d_kernel(*refs, has_cache):
    q_ref, k_ref, v_ref = refs[:3]
    pos = 3
    if has_cache:
        kc_ref, vc_ref = refs[pos:pos + 2]
        pos += 2
    sink_ref, o_ref = refs[pos:pos + 2]
    k2, v2 = refs[pos + 2:pos + 4]
    if has_cache:
        k2c, v2c, bias = refs[pos + 4:pos + 7]
    t = q_ref.shape[0]
    qb = Q_BLOCK
    scale = D_DH ** -0.5
    r = D_HEADS // D_KV_HEADS
    win = min(t, qb + 2 * D_WINDOW)
    lo = lax.broadcasted_iota(jnp.int32, (1, LANES), 1) < D_DH
    rowblk = lax.broadcasted_iota(jnp.int32, (r * qb, 1), 0) // qb
    if has_cache:
        qi = lax.broadcasted_iota(jnp.int32, (r * qb, 1), 0) % qb
        kj = lax.broadcasted_iota(jnp.int32, (1, win), 1)
        for p in range(bias.shape[0]):
            bias[p] = jnp.where(jnp.abs(qi - kj + p * qb) <= D_WINDOW, 0.0, NEG)

    def stage(k_r, v_r, k2_r, v2_r, g):
        gs = slice(g * D_DH, (g + 1) * D_DH)
        kg, vg = k_r[:, gs].astype(k2_r.dtype), v_r[:, gs].astype(v2_r.dtype)
        k2_r[...] = jnp.concatenate([kg, kg], axis=1)
        v2_r[...] = _with_ones(jnp.concatenate([vg, vg], axis=1))

    for g in range(D_KV_HEADS):
        stage(k_ref, v_ref, k2, v2, g)
        if has_cache:
            stage(kc_ref, vc_ref, k2c, v2c, g)
        sink = jnp.zeros((r * qb, 1), F32)
        for e in range(r):
            h = g * r + e
            sink = jnp.where(rowblk == e, sink_ref[0:1, h:h + 1], sink)

        def body(c, carry, g=g, sink=sink):
            rq = pl.ds(pl.multiple_of(c * qb, qb), qb)
            tiles = []
            for e in range(r):
                cols = slice((g * r + e - e % 2) * D_DH, (g * r + e - e % 2 + 2) * D_DH)
                qp = q_ref[rq, cols] * scale
                zero = jnp.zeros_like(qp)
                tiles.append(jnp.where(lo, qp, zero) if e % 2 == 0 else jnp.where(lo, zero, qp))
            qs = jnp.concatenate(tiles, axis=0)
            if has_cache:
                ws = pl.multiple_of(jnp.clip(c * qb - D_WINDOW, 0, t - win), qb)
                rk = pl.ds(ws, win)
                parts = [_dot_nt(qs, k2c[...]), _dot_nt(qs, k2[rk, :]) + bias[(c * qb - ws) // qb]]
                vs = [v2c[...], v2[rk, :]]
            else:
                parts = [_dot_nt(qs, k2[...])]
                vs = [v2[...]]
            m = functools.reduce(jnp.maximum, [jnp.max(x, axis=-1, keepdims=True) for x in parts])
            m = jnp.maximum(m, sink)
            acc = functools.reduce(lambda a, b: a + b,
                                   [_dot(jnp.exp(x - m), vx) for x, vx in zip(parts, vs)])
            o = acc[:, :LANES] / (acc[:, LANES:] + jnp.exp(sink - m))
            for pr in range(r // 2):
                cols = slice((g * r + 2 * pr) * D_DH, (g * r + 2 * pr + 2) * D_DH)
                pair = jnp.where(lo, o[2 * pr * qb:(2 * pr + 1) * qb, :],
                                 o[(2 * pr + 1) * qb:(2 * pr + 2) * qb, :])
                o_ref[rq, cols] = pair.astype(o_ref.dtype)
            return carry

        lax.fori_loop(0, t // qb, body, 0, unroll=4)


def _attn_d(q, k, v, cache_k, cache_v, sink, j, rows):
    d = q.shape[1]
    kvw = D_KV_HEADS * D_DH
    has_cache = cache_k is not None
    if has_cache:
        t, nb, roff = rows.ds, rows.db, rows.n_ctx // rows.ds
    else:
        t, nb, roff = rows.s, rows.b, 0
    assert t % Q_BLOCK == 0
    in_specs = [pl.BlockSpec((t, d), lambda b: (b + roff, 0)),
                pl.BlockSpec((t, kvw), lambda b: (b + roff, 0)),
                pl.BlockSpec((t, kvw), lambda b: (b + roff, 0))]
    args = [q, k, v]
    scratch = [pltpu.VMEM((t, LANES), MXU), pltpu.VMEM((t, 2 * LANES), MXU)]
    if has_cache:
        p = cache_k.shape[2]
        cspec = pl.BlockSpec((None, None, p, kvw), lambda b: (b, j, 0, 0))
        in_specs += [cspec, cspec]
        args += [cache_k, cache_v]
        win = min(t, Q_BLOCK + 2 * D_WINDOW)
        n_pat = 3 if t > win else t // Q_BLOCK
        scratch += [pltpu.VMEM((p, LANES), MXU), pltpu.VMEM((p, 2 * LANES), MXU),
                    pltpu.VMEM((n_pat, D_HEADS // D_KV_HEADS * Q_BLOCK, win), F32)]
    in_specs.append(pl.BlockSpec((None, 1, D_HEADS), lambda b: (j, 0, 0)))
    args.append(sink)
    return pl.pallas_call(
        functools.partial(_attn_d_kernel, has_cache=has_cache),
        out_shape=jax.ShapeDtypeStruct((nb * t, d), MXU),
        grid=(nb,),
        in_specs=in_specs,
        out_specs=pl.BlockSpec((t, d), lambda b: (b, 0)),
        scratch_shapes=scratch,
        compiler_params=_cparams(("parallel",), 44),
        name="attn_d_lat" if has_cache else "attn_d_ctx",
    )(*args)


def kernel(x_prompt, x_sample, cache_a_k, cache_a_v, state_b_fwd, state_b_bwd, cache_c_ckv,
           cache_c_kpe, cache_d_k, cache_d_v, c, c_ctx, ada_w, ada_b, norm_mix_pre,
           norm_mix_post, norm_ffn_pre, norm_ffn_post, a_wq, a_wk, a_wv, a_wo, a_lambda,
           a_subln, b_wq, b_wi, b_wf, b_lower, b_wg, b_gnorm, b_wo, c_wdq, c_qnorm, c_wuq,
           c_wdkv, c_kvnorm, c_wuk, c_wuv, c_wo, d_wq, d_wk, d_wv, d_sink, d_wo, ffn_wg,
           ffn_wu, ffn_wd):
    b, s, d = x_prompt.shape
    db, ds, _ = x_sample.shape
    depth = ada_w.shape[0]
    rows = _Rows(b, s, db, ds, _largest_tile(math.gcd(b * s, ds), 1024))
    n_ctx = rows.n_ctx

    cond = jnp.zeros((rows.crows, d), F32).at[:db].set(c).at[db].set(c_ctx)
    mod = _ada(cond, ada_w, ada_b)
    g3 = lambda a: a.reshape(a.shape[0], 1, a.shape[1])
    n_mix_pre, n_mix_post = g3(norm_mix_pre), g3(norm_mix_post)
    n_ffn_pre, n_ffn_post = g3(norm_ffn_pre), g3(norm_ffn_post)
    tabs = _rope_tables(ds, rows.tm)
    rope_epi = lambda tn: (_make_epi_rope(tn), _rope_extra(tabs, rows))
    ns = max(1, rows.tm // MIN_SLAB_ROWS)

    x = (x_prompt.reshape(n_ctx, d), x_sample.reshape(rows.n_lat, d))
    h = _prep(x[0], x[1], n_mix_pre, mod, 0, rows)

    flat4 = lambda a: a.reshape(a.shape[:3] + (-1,))
    wo_b = {0: _cast_mxu(a_wo), 1: _cast_mxu(b_wo), 2: _cast_mxu(c_wo), 3: _cast_mxu(d_wo)}
    wd_b = _cast_mxu(ffn_wd)
    wf = b_wf.reshape((-1,) + b_wf.shape[2:])
    outs = {k_: [] for k_ in ("a_k", "a_v", "b_f", "b_b", "c_ckv", "c_kpe", "d_k", "d_v")}
    for i in range(depth):
        m, j = i % N_MIXERS, i // N_MIXERS
        if m == 0:
            tn = _largest_tile(a_wq.shape[2], 1024)
            epi, ext = rope_epi(tn)
            q, = _proj(h, [(a_wq, j)], epi, [(tn, MXU)], rows, tn, ext, nsub=ns, name="a_q")
            kv_outs = [(tn, MXU), (tn, F32, "ctx")]
            k, kc = _proj(h, [(a_wk, j)], _with_ctx_copy(epi, rows), kv_outs, rows, tn, ext,
                          nsub=ns, name="a_k")
            v, vc = _proj(h, [(a_wv, j)], _with_ctx_copy(_epi_store, rows), kv_outs, rows, tn,
                          nsub=ns, name="a_v")
            o = (_attn_a(q, k, v, None, None, a_lambda, g3(a_subln), i, j, rows),
                 _attn_a(q, k, v, flat4(cache_a_k), flat4(cache_a_v), a_lambda, g3(a_subln),
                         i, j, rows))
            outs["a_k"].append(kc.reshape(b, s, A_HEADS, 2 * A_DH))
            outs["a_v"].append(vc.reshape(b, s, A_HEADS, 2 * A_DH))
        elif m == 1:
            tn = _largest_tile(b_wq.shape[2], 1024)
            q, = _proj(h, [(b_wq, j)], _epi_silu, [(tn, F32)], rows, tn, nsub=ns, name="b_q")
            vi, = _proj(h, [(b_wi, j)], _epi_store, [(tn, MXU)], rows, tn, nsub=ns, name="b_i")
            g, = _proj(h, [(b_wg, j)], _epi_store, [(tn, F32)], rows, tn, nsub=ns, name="b_g")
            fs = []
            for dr in range(2):
                bl_spec = pl.BlockSpec((None, depth, tn), lambda jn, mm, dr=dr: (dr, 0, jn))
                f, = _proj(h, [(wf, 2 * j + dr)], _make_epi_forget(i, depth), [(tn, F32)], rows,
                           tn, [(jnp.swapaxes(b_lower, 0, 1), bl_spec)], nsub=ns,
                           name="b_f%d" % dr)
                fs.append(f)
            oc, sf, sb = _gla(q, vi, fs[0], fs[1], g, g3(b_gnorm), None, None, j, rows)
            ol, = _gla(q, vi, fs[0], fs[1], g, g3(b_gnorm), state_b_fwd, state_b_bwd, j, rows)
            o = (oc, ol)
            outs["b_f"].append(sf)
            outs["b_b"].append(sb)
        elif m == 2:
            nq = c_wdq.shape[2]
            cq, = _proj(h, [(c_wdq, j)], _make_epi_rmsnorm(), [(nq, MXU)], rows, nq,
                        [(g3(c_qnorm), pl.BlockSpec((None, 1, nq), lambda jn, mm: (j, 0, 0)))],
                        name="c_dq")
            nkv = c_wdkv.shape[2]
            ext = [(g3(c_kvnorm), pl.BlockSpec((None, 1, C_KVLORA), lambda jn, mm: (j, 0, 0)))]
            ext += _rope_extra(tabs, rows)
            ckv, kpe, ckv_c, kpe_c = _proj(
                h, [(c_wdkv, j)], _make_epi_ckv(rows),
                [(C_KVLORA, MXU), (C_ROPE, MXU), (C_KVLORA, F32, "ctx"), (C_ROPE, F32, "ctx")],
                rows, nkv, ext, name="c_dkv")
            wuq = c_wuq.reshape(c_wuq.shape[0], nq, C_HEADS, C_NOPE + C_ROPE)
            wuq_n = wuq[:, :, :, :C_NOPE].reshape(-1, nq, C_HEADS * C_NOPE)
            wuq_p = wuq[:, :, :, C_NOPE:].reshape(-1, nq, C_HEADS * C_ROPE)
            tn = _largest_tile(wuq_n.shape[2], 1024)
            qn, = _proj(cq, [(wuq_n, j)], _epi_store, [(tn, MXU)], rows, tn, name="c_qn")
            tnp = wuq_p.shape[2]
            epi, ext = rope_epi(tnp)
            qp, = _proj(cq, [(wuq_p, j)], epi, [(tnp, MXU)], rows, tnp, ext, name="c_qp")
            o = (_attn_c(qn, qp, ckv, kpe, None, None, c_wuk, c_wuv, j, rows),
                 _attn_c(qn, qp, ckv, kpe, cache_c_ckv, cache_c_kpe, c_wuk, c_wuv, j, rows))
            outs["c_ckv"].append(ckv_c.reshape(b, s, C_KVLORA))
            outs["c_kpe"].append(kpe_c.reshape(b, s, C_ROPE))
        else:
            tn = _largest_tile(d_wq.shape[2], 1024)
            epi, ext = rope_epi(tn)
            q, = _proj(h, [(d_wq, j)], epi, [(tn, MXU)], rows, tn, ext, nsub=ns, name="d_q")
            kvw = d_wk.shape[2]
            epi, ext = rope_epi(kvw)
            kv_outs = [(kvw, MXU), (kvw, F32, "ctx")]
            k, kc = _proj(h, [(d_wk, j)], _with_ctx_copy(epi, rows), kv_outs, rows, kvw, ext,
                          name="d_k")
            v, vc = _proj(h, [(d_wv, j)], _with_ctx_copy(_epi_store, rows), kv_outs, rows, kvw,
                          name="d_v")
            o = (_attn_d(q, k, v, None, None, g3(d_sink), j, rows),
                 _attn_d(q, k, v, flat4(cache_d_k), flat4(cache_d_v), g3(d_sink), j, rows))
            outs["d_k"].append(kc.reshape(b, s, D_KV_HEADS, D_DH))
            outs["d_v"].append(vc.reshape(b, s, D_KV_HEADS, D_DH))

        x, h = _resid(o, wo_b[m], j, x, mod, i, 2, n_mix_post, rows, nxt=(i, 3, n_ffn_pre),
                      name="mix_out")
        tf = _largest_tile(ffn_wg.shape[2], 512)
        a, = _proj(h, [(ffn_wg, i), (ffn_wu, i)], _epi_swiglu, [(tf, MXU)], rows, tf,
                   tm=_largest_tile(rows.mt, 2048), nsub=2, name="ffn_gu")
        nxt = (i + 1, 0, n_mix_pre) if i + 1 < depth else None
        x, h = _resid(a, wd_b, i, x, mod, i, 5, n_ffn_post, rows, nxt=nxt, name="ffn_down")

    stack = lambda lst: jnp.stack(lst, axis=1)
    return (x.reshape(b, s, d), h.reshape(db, ds, d),
            stack(outs["a_k"]), stack(outs["a_v"]), stack(outs["b_f"]), stack(outs["b_b"]),
            stack(outs["c_ckv"]), stack(outs["c_kpe"]), stack(outs["d_k"]), stack(outs["d_v"]))
```

```python
import functools
import math

import numpy as np
import jax
import jax.numpy as jnp
from jax import lax
from jax.experimental import pallas as pl
from jax.experimental.pallas import tpu as pltpu

EPS = 1e-6
ROPE_BASE = 10000.0
GRID_W = 64
NEG = -1e30
N_MIXERS = 4
A_HEADS, A_DH = 16, 64
B_HEADS, B_DK, B_CHUNK = 16, 128, 32
C_HEADS, C_NOPE, C_ROPE, C_VD = 16, 128, 64, 128
D_HEADS, D_KV_HEADS, D_DH, D_WINDOW, Q_BLOCK = 32, 4, 64, 128, 128

LANES = 128
SUBLANES = 8
VMEM_CAP_MIB = 56
MIN_SLAB_ROWS = 256

MXU = jnp.bfloat16
F32 = jnp.float32


def _cparams(sem, vmem_mib):
    return pltpu.CompilerParams(dimension_semantics=sem,
                                vmem_limit_bytes=min(vmem_mib, VMEM_CAP_MIB) << 20)


def _dot(a, b):
    return jnp.dot(a.astype(MXU), b.astype(MXU), preferred_element_type=F32)


def _dot_nt(a, b):
    return lax.dot_general(a.astype(MXU), b.astype(MXU), (((1,), (1,)), ((), ())),
                           preferred_element_type=F32)


def _dot_tn(a, b):
    return lax.dot_general(a.astype(MXU), b.astype(MXU), (((0,), (0,)), ((), ())),
                           preferred_element_type=F32)


def _sigmoid(x):
    return 1.0 / (1.0 + jnp.exp(-x))


def _silu(x):
    return x * _sigmoid(x)


def _rms(x, g):
    ms = jnp.mean(x * x, axis=-1, keepdims=True)
    return x * lax.rsqrt(ms + EPS) * g


def _largest_tile(n, pref):
    t = min(n, pref)
    while n % t:
        t -= SUBLANES
    assert t > 0 and n % t == 0
    return t


class _Rows:
    def __init__(self, b, s, db, ds, tm):
        self.b, self.s, self.db, self.ds = b, s, db, ds
        self.n_ctx, self.n_lat = b * s, db * ds
        self.mt = self.n_ctx + self.n_lat
        self.tm = tm
        assert self.n_ctx % tm == 0 and ds % tm == 0 and self.n_ctx % ds == 0
        self.nct = self.n_ctx // tm
        self.ntiles = self.mt // tm
        self.crows = -(-(db + 1) // SUBLANES) * SUBLANES

    def cond_row(self, i):
        lat = ((i - self.nct) * self.tm) // self.ds
        return jnp.where(i < self.nct, self.db, lat)

    def pos_block(self, i):
        lat = ((i - self.nct) * self.tm % self.ds) // self.tm
        return jnp.where(i < self.nct, self.ds // self.tm, lat)


def _rope_tables(n_tok, n_identity):
    nf = 16
    inv = (ROPE_BASE ** (-np.arange(nf, dtype=np.float32) / nf)).astype(np.float32)
    t = np.arange(n_tok)
    row, col = (t // GRID_W).astype(np.float32), (t % GRID_W).astype(np.float32)
    lane = np.arange(64)
    pos = np.where(lane[None, :] < 32, row[:, None], col[:, None]).astype(np.float32)
    ang = (pos * inv[lane % nf][None, :]).astype(np.float32)
    cos, sin = np.cos(ang).astype(np.float32), np.sin(ang).astype(np.float32)
    first = (lane % 32) < 16
    sa = np.where(first[None, :], -sin, 0.0).astype(np.float32)
    sb = np.where(first[None, :], 0.0, sin).astype(np.float32)
    def table(a, fill):
        a = np.concatenate([a, np.full((n_identity, 64), fill, np.float32)], axis=0)
        return jnp.asarray(np.concatenate([a, a], axis=1))

    return table(cos, 1.0), table(sa, 0.0), table(sb, 0.0)


def _rope128(y, cos, sa, sb):
    return (y * cos + pltpu.roll(y, LANES - 16, 1) * sa + pltpu.roll(y, 16, 1) * sb)


def _ada_kernel(c_ref, w_ref, b_ref, o_ref):
    o_ref[...] = _dot(_silu(c_ref[...]), w_ref[...]) + b_ref[...]


def _ada(cond, ada_w, ada_b):
    depth, d, n = ada_w.shape
    cr = cond.shape[0]
    tn = _largest_tile(n, 1024)
    return pl.pallas_call(
        _ada_kernel,
        out_shape=jax.ShapeDtypeStruct((depth, cr, n), F32),
        grid=(depth, n // tn),
        in_specs=[pl.BlockSpec((cr, d), lambda l, j: (0, 0)),
                  pl.BlockSpec((None, d, tn), lambda l, j: (l, 0, j)),
                  pl.BlockSpec((None, 1, tn), lambda l, j: (l, 0, j))],
        out_specs=pl.BlockSpec((None, cr, tn), lambda l, j: (l, 0, j)),
        compiler_params=_cparams(("parallel", "parallel"), 40),
        name="ada_mod",
    )(cond, ada_w, ada_b.reshape(depth, 1, n))


def _prep_kernel(xp_ref, xs_ref, g_ref, mod_ref, h_ref, *, rows, d):
    i = pl.program_id(0)
    x = jnp.where(i < rows.nct, xp_ref[...], xs_ref[...])
    cr = rows.cond_row(i)
    shift = mod_ref[pl.ds(cr, 1), pl.ds(0, d)]
    scale = mod_ref[pl.ds(cr, 1), pl.ds(d, d)]
    h_ref[...] = (_rms(x, g_ref[...]) * (1.0 + scale) + shift).astype(h_ref.dtype)


def _prep(xp, xs, g, mod, layer, rows):
    d = xp.shape[1]
    rows = _Rows(rows.b, rows.s, rows.db, rows.ds, _largest_tile(rows.tm, 512))
    tm, nct = rows.tm, rows.nct
    return pl.pallas_call(
        functools.partial(_prep_kernel, rows=rows, d=d),
        out_shape=jax.ShapeDtypeStruct((rows.mt, d), MXU),
        grid=(rows.ntiles,),
        in_specs=[pl.BlockSpec((tm, d), lambda i: (jnp.minimum(i, nct - 1), 0)),
                  pl.BlockSpec((tm, d), lambda i: (jnp.maximum(i - nct, 0), 0)),
                  pl.BlockSpec((None, 1, d), lambda i: (layer, 0, 0)),
                  pl.BlockSpec((None, rows.crows, mod.shape[2]), lambda i: (layer, 0, 0))],
        out_specs=pl.BlockSpec((tm, d), lambda i: (i, 0)),
        compiler_params=_cparams(("parallel",), 32 * tm * d // (1 << 20) + 8),
        name="prep",
    )(xp, xs, g, mod)


def _proj_kernel(*refs, n_w, n_extra, n_out, nsub, epi, fin):
    h_ref = refs[0]
    w_refs = refs[1:1 + n_w]
    extra = refs[1 + n_w:1 + n_w + n_extra]
    outs = refs[1 + n_w + n_extra:1 + n_w + n_extra + n_out]
    wb_refs = refs[1 + n_w + n_extra + n_out:]
    m = pl.program_id(1)

    @pl.when(m == 0)
    def _():
        for w_ref, wb_ref in zip(w_refs, wb_refs):
            wb_ref[...] = w_ref[...].astype(wb_ref.dtype)

    rs = h_ref.shape[0] // nsub
    slabs = [slice(s * rs, (s + 1) * rs) for s in range(nsub)]
    ys_all = [[jnp.dot(h_ref[r, :], wb_ref[...], preferred_element_type=F32) for wb_ref in wb_refs]
              for r in slabs]
    for r, ys in zip(slabs, ys_all):
        epi(ys, m, extra, outs, r)
    if fin is not None:
        fin(ys_all, slabs, m, extra, outs)


def _proj(h, ws, epi, outs, rows, tn, extra=(), tm=None, nsub=1, name="proj"):
    mt, k = h.shape
    n = ws[0][0].shape[2]
    assert n % tn == 0
    tm = rows.tm if tm is None else tm
    assert mt % tm == 0
    nj = n // tn
    in_specs = [pl.BlockSpec((tm, k), lambda j, m: (m, 0))]
    in_specs += [pl.BlockSpec((None, k, tn), lambda j, m, li=li: (li, 0, j)) for _, li in ws]
    in_specs += [s for _, s in extra]
    nct = rows.nct
    ctx_only = [len(o) > 2 and o[2] == "ctx" for o in outs]
    outs = [o[:2] for o in outs]
    out_shape = tuple(jax.ShapeDtypeStruct((rows.n_ctx if co else mt, c * nj), dt)
                      for (c, dt), co in zip(outs, ctx_only))
    out_specs = tuple(pl.BlockSpec((tm, c), (lambda j, m: (jnp.minimum(m, nct - 1), j)) if co
                                   else (lambda j, m: (m, j)))
                      for (c, _), co in zip(outs, ctx_only))
    assert not any(ctx_only) or tm == rows.tm
    out_bytes = sum(tm * c * jnp.dtype(dt).itemsize for c, dt in outs)
    vmem = (2 * tm * k * 2 + len(ws) * (2 * k * tn * 4 + k * tn * 2) + 2 * out_bytes
            + len(ws) * tm * tn * 8) // (1 << 20) + 6
    epi, fin = epi if isinstance(epi, tuple) else (epi, None)
    res = pl.pallas_call(
        functools.partial(_proj_kernel, n_w=len(ws), n_extra=len(extra), n_out=len(outs),
                          nsub=nsub, epi=epi, fin=fin),
        out_shape=out_shape,
        grid=(nj, mt // tm),
        in_specs=in_specs,
        out_specs=out_specs,
        scratch_shapes=[pltpu.VMEM((k, tn), MXU) for _ in ws],
        compiler_params=_cparams(("parallel", "arbitrary"), vmem),
        name=name,
    )(h, *[w for w, _ in ws], *[a for a, _ in extra])
    return res


def _epi_store(ys, m, extra, outs, r):
    outs[0][r, :] = ys[0].astype(outs[0].dtype)


def _epi_silu(ys, m, extra, outs, r):
    outs[0][r, :] = _silu(ys[0]).astype(outs[0].dtype)


def _epi_swiglu(ys, m, extra, outs, r):
    outs[0][r, :] = (_silu(ys[0]) * ys[1]).astype(outs[0].dtype)


def _make_epi_rope(tn):
    def epi(ys, m, extra, outs, r):
        cos_ref, sa_ref, sb_ref = extra
        y, o_ref = ys[0], outs[0]
        cos, sa, sb = cos_ref[r, :], sa_ref[r, :], sb_ref[r, :]
        for c in range(tn // LANES):
            sl = slice(c * LANES, (c + 1) * LANES)
            o_ref[r, sl] = _rope128(y[:, sl], cos, sa, sb).astype(o_ref.dtype)
    return epi


def _with_ctx_copy(epi, rows):
    def fin(ys_all, slabs, m, extra, outs):
        @pl.when(m < rows.nct)
        def _():
            for r, ys in zip(slabs, ys_all):
                outs[-1][r, :] = ys[0]
    return epi, fin


def _make_epi_qkv(rows, tn):
    rope = _make_epi_rope(tn)

    def epi(ys, m, extra, outs, r):
        rope(ys[0:1], m, extra, outs[0:1], r)
        rope(ys[1:2], m, extra, outs[1:2], r)
        outs[2][r, :] = ys[2].astype(outs[2].dtype)

    def fin(ys_all, slabs, m, extra, outs):
        @pl.when(m < rows.nct)
        def _():
            for r, ys in zip(slabs, ys_all):
                outs[3][r, :] = ys[1]
                outs[4][r, :] = ys[2]
    return epi, fin


def _rope_extra(tabs, rows):
    tm = rows.tm
    spec = pl.BlockSpec((tm, LANES), lambda j, m: (rows.pos_block(m), 0))
    return [(t, spec) for t in tabs]


def _make_epi_forget(layer_idx, depth):
    def epi(ys, m, extra, outs, r):
        bl = extra[0][...]
        e = jnp.exp(bl - jnp.max(bl, axis=0, keepdims=True))
        p = e / jnp.sum(e, axis=0, keepdims=True)
        lb = jnp.sum(p[0:layer_idx + 1], axis=0, keepdims=True) - p[0:1]
        outs[0][r, :] = lb + (1.0 - lb) * _sigmoid(ys[0])
    return epi


def _make_epi_rmsnorm():
    def epi(ys, m, extra, outs, r):
        outs[0][r, :] = _rms(ys[0], extra[0][...]).astype(outs[0].dtype)
    return epi


def _make_epi_ckv(rows):
    def epi(ys, m, extra, outs, r):
        g_ref, cos_ref, sa_ref, sb_ref = extra
        ckv_all, kpe_all, ckv_ctx, kpe_ctx = outs
        y = ys[0]
        ckv = _rms(y[:, :C_KVLORA], g_ref[...])
        ckv_all[r, :] = ckv.astype(ckv_all.dtype)
        kpe = y[:, C_KVLORA:C_KVLORA + C_ROPE]

        @pl.when(m < rows.nct)
        def _():
            kpe_all[r, :] = kpe.astype(kpe_all.dtype)
            ckv_ctx[r, :] = ckv
            kpe_ctx[r, :] = kpe

        @pl.when(m >= rows.nct)
        def _():
            k2 = jnp.concatenate([kpe, kpe], axis=1)
            kr = _rope128(k2, cos_ref[r, :], sa_ref[r, :], sb_ref[r, :])
            kpe_all[r, :] = kr[:, :C_ROPE].astype(kpe_all.dtype)
    return epi


C_KVLORA = 256


def _cast_kernel(w_ref, o_ref):
    o_ref[...] = w_ref[...].astype(o_ref.dtype)


def _cast_mxu(w):
    l, k, n = w.shape
    tr = _largest_tile(k, 512)
    return pl.pallas_call(
        _cast_kernel,
        out_shape=jax.ShapeDtypeStruct(w.shape, MXU),
        grid=(l, k // tr),
        in_specs=[pl.BlockSpec((None, tr, n), lambda i, r: (i, r, 0))],
        out_specs=pl.BlockSpec((None, tr, n), lambda i, r: (i, r, 0)),
        compiler_params=_cparams(("parallel", "parallel"), 12 * tr * n // (1 << 20) + 8),
        name="cast_w",
    )(w)


def _resid_kernel(*refs, rows, d, nsub, gate_chunk, next_chunk, split_a, split_x):
    refs = list(refs)
    a_refs = [refs.pop(0) for _ in range(2 if split_a else 1)]
    w_ref = refs.pop(0)
    x_refs = [refs.pop(0) for _ in range(2 if split_x else 1)]
    modc_ref, gpost_ref = refs.pop(0), refs.pop(0)
    if next_chunk is not None:
        modn_ref, gpre_ref = refs.pop(0), refs.pop(0)
        xo_ref, ho_ref = refs
    else:
        xo_ref, xl_ref = refs
    i = pl.program_id(0)

    def rows_of(parts, r):
        if len(parts) == 1:
            return parts[0][r, :]
        return jnp.where(i < rows.nct, parts[0][r, :], parts[1][r, :])

    cr = rows.cond_row(i)
    gate = modc_ref[pl.ds(cr, 1), pl.ds(gate_chunk * d, d)] * gpost_ref[...]
    if next_chunk is not None:
        shift = modn_ref[pl.ds(cr, 1), pl.ds(next_chunk * d, d)]
        scale = (1.0 + modn_ref[pl.ds(cr, 1), pl.ds((next_chunk + 1) * d, d)]) * gpre_ref[...]
    w = w_ref[...]
    rs = a_refs[0].shape[0] // nsub
    slabs = [slice(s * rs, (s + 1) * rs) for s in range(nsub)]
    ys = [jnp.dot(rows_of(a_refs, r), w, preferred_element_type=F32) for r in slabs]
    for r, y in zip(slabs, ys):
        yn = y * lax.rsqrt(jnp.mean(y * y, axis=-1, keepdims=True) + EPS)
        xn = rows_of(x_refs, r) + yn * gate
        if next_chunk is not None:
            xo_ref[r, :] = xn
            hn = xn * lax.rsqrt(jnp.mean(xn * xn, axis=-1, keepdims=True) + EPS)
            ho_ref[r, :] = (hn * scale + shift).astype(ho_ref.dtype)
        else:
            @pl.when(i < rows.nct)
            def _(xn=xn, r=r):
                xo_ref[r, :] = xn

            @pl.when(i >= rows.nct)
            def _(xn=xn, r=r):
                xl_ref[r, :] = xn


def _resid(a, w, li, x, mod, layer, gate_chunk, gpost, rows, nxt=None, name="resid"):
    split_a = isinstance(a, (tuple, list))
    a_parts = list(a) if split_a else [a]
    mt, kdim = rows.mt, a_parts[0].shape[1]
    d = w.shape[2]
    n_x = 2 if isinstance(x, (tuple, list)) else 1
    row_bytes = len(a_parts) * 2 * kdim * 2 + d * (8 * n_x + 8 + 4 + 12)
    budget = (VMEM_CAP_MIB - 6 << 20) - kdim * d * 2
    tm = _largest_tile(rows.tm, max(SUBLANES, 1 << int(math.log2(budget // row_bytes))))
    nsub = max(1, tm // MIN_SLAB_ROWS)
    sub = _Rows(rows.b, rows.s, rows.db, rows.ds, tm)
    nct = sub.nct
    mspec = lambda l: pl.BlockSpec((None, rows.crows, mod.shape[2]), lambda i: (l, 0, 0))
    gspec = lambda l: pl.BlockSpec((None, 1, d), lambda i: (l, 0, 0))
    def row_specs(width, split):
        if split:
            return [pl.BlockSpec((tm, width), lambda i: (jnp.minimum(i, nct - 1), 0)),
                    pl.BlockSpec((tm, width), lambda i: (jnp.maximum(i - nct, 0), 0))]
        return [pl.BlockSpec((tm, width), lambda i: (i, 0))]

    split_x = isinstance(x, (tuple, list))
    x_parts = list(x) if split_x else [x]
    in_specs = row_specs(kdim, split_a) + [
        pl.BlockSpec((None, kdim, d), lambda i: (li, 0, 0), pipeline_mode=pl.Buffered(1))
    ] + row_specs(d, split_x) + [mspec(layer), gspec(layer)]
    args = a_parts + [w] + x_parts + [mod, gpost]
    if nxt is not None:
        nl, nchunk, gpre = nxt
        in_specs += [mspec(nl), gspec(nl)]
        args += [mod, gpre]
        out_shape = [jax.ShapeDtypeStruct((mt, d), F32), jax.ShapeDtypeStruct((mt, d), MXU)]
        out_specs = [pl.BlockSpec((tm, d), lambda i: (i, 0))] * 2
    else:
        nchunk = None
        out_shape = [jax.ShapeDtypeStruct((rows.n_ctx, d), F32),
                     jax.ShapeDtypeStruct((rows.n_lat, d), F32)]
        out_specs = [pl.BlockSpec((tm, d), lambda i: (jnp.minimum(i, nct - 1), 0)),
                     pl.BlockSpec((tm, d), lambda i: (jnp.maximum(i - nct, 0), 0))]
    vmem = (kdim * d * 2 + tm * row_bytes) // (1 << 20) + 8
    res = pl.pallas_call(
        functools.partial(_resid_kernel, rows=sub, d=d, nsub=nsub, gate_chunk=gate_chunk,
                          next_chunk=nchunk, split_a=split_a, split_x=split_x),
        out_shape=tuple(out_shape),
        grid=(mt // tm,),
        in_specs=in_specs,
        out_specs=tuple(out_specs),
        compiler_params=_cparams(("arbitrary",), vmem),
        name=name,
    )(*args)
    return res


def _with_ones(v):
    return jnp.concatenate([v, jnp.ones((v.shape[0], LANES), v.dtype)], axis=1)


def _attend(q, keys, vaugs, scale=None, masks=None):
    parts = [_dot_nt(q, kk) for kk in keys]
    if masks is not None:
        parts = [s if mk is None else jnp.where(mk, s, NEG) for s, mk in zip(parts, masks)]
    if scale is not None:
        parts = [s * scale for s in parts]
    m = functools.reduce(jnp.maximum, [jnp.max(s, axis=-1, keepdims=True) for s in parts])
    acc = functools.reduce(lambda a, b: a + b,
                           [_dot(jnp.exp(s - m), va) for s, va in zip(parts, vaugs)])
    return acc, m


def _attn_a_kernel(*refs, hb, qc, lam_init, has_cache):
    q_ref, k_ref, v_ref = refs[:3]
    pos = 3
    if has_cache:
        kc_ref, vc_ref = refs[pos:pos + 2]
        pos += 2
    lam_ref, sub_ref, o_ref = refs[pos:pos + 3]
    sq = q_ref.shape[0]
    dv = 2 * A_DH
    scale = A_DH ** -0.5
    lp = lam_ref[...]
    lam = (jnp.exp(jnp.sum(lp[0:1] * lp[1:2], axis=1, keepdims=True))
           - jnp.exp(jnp.sum(lp[2:3] * lp[3:4], axis=1, keepdims=True)) + lam_init)
    lo = lax.broadcasted_iota(jnp.int32, (1, dv), 1) < A_DH
    sub = sub_ref[...] * (1.0 - lam_init)
    for h in range(hb):
        cols = slice(h * dv, (h + 1) * dv)
        keys = [k_ref[:, cols].astype(MXU)]
        vals = [_with_ones(v_ref[:, cols].astype(MXU))]
        if has_cache:
            keys.append(kc_ref[:, cols].astype(MXU))
            vals.append(_with_ones(vc_ref[:, cols].astype(MXU)))

        def body(c, carry, cols=cols, keys=keys, vals=vals):
            r = pl.ds(pl.multiple_of(c * qc, qc), qc)
            q = q_ref[r, cols] * scale
            zero = jnp.zeros_like(q)
            a1, _ = _attend(jnp.where(lo, q, zero), keys, vals)
            a2, _ = _attend(jnp.where(lo, zero, q), keys, vals)
            o = a1[:, :dv] / a1[:, dv:] - lam * (a2[:, :dv] / a2[:, dv:])
            ms = jnp.mean(o * o, axis=-1, keepdims=True)
            o_ref[r, cols] = (o * lax.rsqrt(ms + EPS) * sub).astype(o_ref.dtype)
            return carry

        lax.fori_loop(0, sq // qc, body, 0, unroll=True)


def _attn_a(q, k, v, cache_k, cache_v, lam_p, subln, layer_idx, j, rows):
    d = q.shape[1]
    lam_init = 0.8 - 0.6 * math.exp(-0.3 * layer_idx)
    has_cache = cache_k is not None
    if has_cache:
        t, nb, roff = rows.ds, rows.db, rows.n_ctx // rows.ds
        hb = 4
    else:
        t, nb, roff = rows.s, rows.b, 0
        hb = A_HEADS
    cw = hb * 2 * A_DH
    qc = _largest_tile(t, 256)
    tok = lambda b, g: (b + roff, g)
    in_specs = [pl.BlockSpec((t, cw), tok)] * 3
    args = [q, k, v]
    if has_cache:
        p = cache_k.shape[2]
        cspec = pl.BlockSpec((None, None, p, cw), lambda b, g: (b, j, 0, g))
        in_specs += [cspec, cspec]
        args += [cache_k, cache_v]
    in_specs += [pl.BlockSpec((None, 4, A_DH), lambda b, g: (j, 0, 0)),
                 pl.BlockSpec((None, 1, 2 * A_DH), lambda b, g: (j, 0, 0))]
    args += [lam_p, subln]
    return pl.pallas_call(
        functools.partial(_attn_a_kernel, hb=hb, qc=qc, lam_init=lam_init, has_cache=has_cache),
        out_shape=jax.ShapeDtypeStruct((nb * t, d), MXU),
        grid=(nb, d // cw),
        in_specs=in_specs,
        out_specs=pl.BlockSpec((t, cw), lambda b, g: (b, g)),
        compiler_params=_cparams(("parallel", "parallel"), 40),
        name="attn_a_lat" if has_cache else "attn_a_ctx",
    )(*args)


def _gla_kernel(*refs, t, nhb, has_s0, has_sout):
    q_ref, v_ref, f0_ref, f1_ref, g_ref, gn_ref = refs[:6]
    pos = 6
    if has_s0:
        s0f_ref, s0b_ref = refs[pos:pos + 2]
        pos += 2
    o_ref = refs[pos]
    pos += 1
    if has_sout:
        sf_ref, sb_ref = refs[pos:pos + 2]
        pos += 2
    qdf, kif, qdb, kib, vb, bcf, bsb, of, ob, uf, ub, stf, stb = refs[pos:]
    c = B_CHUNK
    n = t // c
    heads = [slice(hi * LANES, (hi + 1) * LANES) for hi in range(nhb)]
    r32 = lax.broadcasted_iota(jnp.int32, (t, 1), 0) % c

    def prefix(x):
        s = 1
        while s < c:
            x = x + jnp.where(r32 >= s, pltpu.roll(x, s, 0), 0.0)
            s *= 2
        return x

    def suffix(x):
        s = 1
        while s < c:
            x = x + jnp.where(r32 < c - s, pltpu.roll(x, t - s, 0), 0.0)
            s *= 2
        return x

    q = q_ref[...]
    vb[...] = v_ref[...].astype(vb.dtype)
    f = f0_ref[...]
    bc = prefix(jnp.log(f))
    bcf[...] = bc
    qdf[...] = (q * jnp.exp(bc)).astype(qdf.dtype)
    kif[...] = ((1.0 - f) * jnp.exp(-bc)).astype(kif.dtype)
    f = f1_ref[...]
    bs = suffix(jnp.log(f))
    bsb[...] = bs
    qdb[...] = (q * jnp.exp(bs)).astype(qdb.dtype)
    kib[...] = ((1.0 - f) * jnp.exp(-bs)).astype(kib.dtype)

    blk = _largest_tile(t, 256)
    ti = lax.broadcasted_iota(jnp.int32, (blk, blk), 0)
    si = lax.broadcasted_iota(jnp.int32, (blk, blk), 1)
    same = (ti // c) == (si // c)
    lower, upper = same & (ti >= si), same & (ti <= si)
    for bi in range(t // blk):
        rb = slice(bi * blk, (bi + 1) * blk)
        for hc in heads:
            vv = vb[rb, hc]
            of[rb, hc] = _dot(jnp.where(lower, _dot_nt(qdf[rb, hc], kif[rb, hc]), 0.0), vv)
            ob[rb, hc] = _dot(jnp.where(upper, _dot_nt(qdb[rb, hc], kib[rb, hc]), 0.0), vv)

    def incr(i, carry):
        r = pl.ds(pl.multiple_of(i * c, c), c)
        vv = vb[r, :]
        bc_c, bs_c = bcf[r, :], bsb[r, :]
        kef = (1.0 - f0_ref[r, :]) * jnp.exp(bc_c[c - 1:c, :] - bc_c)
        keb = (1.0 - f1_ref[r, :]) * jnp.exp(bs_c[0:1, :] - bs_c)
        for hi, hc in enumerate(heads):
            uf[hi, i] = _dot_tn(vv[:, hc], kef[:, hc])
            ub[hi, i] = _dot_tn(vv[:, hc], keb[:, hc])
        return carry

    lax.fori_loop(0, n, incr, 0, unroll=min(8, n))

    for hi in range(nhb):
        if has_s0:
            stf[hi] = s0f_ref[hi].T
            stb[hi] = s0b_ref[hi].T
        else:
            stf[hi] = jnp.zeros((B_DK, B_DK), F32)
            stb[hi] = jnp.zeros((B_DK, B_DK), F32)

    def step(i, carry):
        rf = pl.ds(pl.multiple_of(i * c, c), c)
        ib = n - 1 - i
        rb = pl.ds(pl.multiple_of(ib * c, c), c)
        decf = jnp.exp(bcf[pl.ds(i * c + c - 1, 1), :])
        decb = jnp.exp(bsb[pl.ds(ib * c, 1), :])
        for hi, hc in enumerate(heads):
            sf = stf[hi]
            of[rf, hc] += _dot_nt(qdf[rf, hc], sf)
            stf[hi] = sf * decf[:, hc] + uf[hi, i]
            sb = stb[hi]
            ob[rb, hc] += _dot_nt(qdb[rb, hc], sb)
            stb[hi] = sb * decb[:, hc] + ub[hi, ib]
        return carry

    lax.fori_loop(0, n, step, 0, unroll=min(8, n))
    gn = gn_ref[...]
    for hi, hc in enumerate(heads):
        if has_sout:
            sf_ref[hi] = stf[hi].T
            sb_ref[hi] = stb[hi].T
        o = of[:, hc] + ob[:, hc]
        o_ref[:, hc] = (_rms(o, gn) * _silu(g_ref[:, hc])).astype(o_ref.dtype)


def _gla(q, v, f0, f1, g, gnorm, s0f, s0b, j, rows):
    d = q.shape[1]
    has_s0 = s0f is not None
    if has_s0:
        t, nb, roff, nhb = rows.ds, rows.db, rows.n_ctx // rows.ds, 2
    else:
        t, nb, roff, nhb = rows.s, rows.b, 0, 8
    dv = d // B_HEADS
    assert dv == LANES and B_DK == LANES and B_HEADS % nhb == 0
    w = nhb * LANES
    tok = pl.BlockSpec((t, w), lambda b, h: (b + roff, h))
    in_specs = [tok] * 5 + [pl.BlockSpec((None, 1, dv), lambda b, h: (j, 0, 0))]
    args = [q, v, f0, f1, g, gnorm]
    if has_s0:
        sspec = pl.BlockSpec((None, None, nhb, B_DK, dv), lambda b, h: (b, j, h, 0, 0))
        in_specs += [sspec, sspec]
        args += [s0f, s0b]
    out_shape = [jax.ShapeDtypeStruct((nb * t, d), MXU)]
    out_specs = [pl.BlockSpec((t, w), lambda b, h: (b, h))]
    if not has_s0:
        ospec = pl.BlockSpec((None, nhb, B_DK, dv), lambda b, h: (b, h, 0, 0))
        out_shape += [jax.ShapeDtypeStruct((nb, B_HEADS, B_DK, dv), F32)] * 2
        out_specs += [ospec, ospec]
    scratch = ([pltpu.VMEM((t, w), MXU) for _ in range(5)]
               + [pltpu.VMEM((t, w), F32) for _ in range(4)]
               + [pltpu.VMEM((nhb, t // B_CHUNK, dv, B_DK), F32) for _ in range(2)]
               + [pltpu.VMEM((nhb, dv, B_DK), F32) for _ in range(2)])
    return pl.pallas_call(
        functools.partial(_gla_kernel, t=t, nhb=nhb, has_s0=has_s0, has_sout=not has_s0),
        out_shape=tuple(out_shape),
        grid=(nb, B_HEADS // nhb),
        in_specs=in_specs,
        out_specs=tuple(out_specs),
        scratch_shapes=scratch,
        compiler_params=_cparams(("parallel", "parallel"), 40),
        name="gla_lat" if has_s0 else "gla_ctx",
    )(*args)


def _attn_c_kernel(*refs, qc, nh, has_cache):
    qn_ref, qp_ref, ckv_ref, kpe_ref = refs[:4]
    pos = 4
    if has_cache:
        ckvc_ref, kpec_ref = refs[pos:pos + 2]
        pos += 2
    wuk_ref, wuv_ref, o_ref = refs[pos:pos + 3]
    scr = refs[pos + 3:]
    sq = qn_ref.shape[0]
    scale = (C_NOPE + C_ROPE) ** -0.5
    wuk = wuk_ref[...].astype(MXU)
    wuv = wuv_ref[...].astype(MXU)

    kw = C_NOPE + 2 * C_ROPE
    vw = C_VD + LANES

    def expand(ckv_r, kpe_r, kcat_r, vcat_r):
        ck = ckv_r[...].astype(MXU)
        kn = jnp.dot(ck, wuk, preferred_element_type=F32).astype(kcat_r.dtype)
        vn = jnp.dot(ck, wuv, preferred_element_type=F32).astype(vcat_r.dtype)
        kp = kpe_r[...].astype(kcat_r.dtype)
        ones = jnp.ones((ck.shape[0], LANES), vcat_r.dtype)
        for hh in range(nh):
            kcat_r[:, hh * kw:(hh + 1) * kw] = jnp.concatenate(
                [kn[:, hh * C_NOPE:(hh + 1) * C_NOPE], kp, kp], axis=1)
            vcat_r[:, hh * vw:(hh + 1) * vw] = jnp.concatenate(
                [vn[:, hh * C_VD:(hh + 1) * C_VD], ones], axis=1)

    groups = [scr[0:2]]
    expand(ckv_ref, kpe_ref, *scr[0:2])
    if has_cache:
        groups.append(scr[2:4])
        expand(ckvc_ref, kpec_ref, *scr[2:4])
    lo = lax.broadcasted_iota(jnp.int32, (1, LANES), 1) < C_ROPE

    def body(c, carry):
        r = pl.ds(pl.multiple_of(c * qc, qc), qc)
        for hh in range(nh):
            cols = slice(hh * C_NOPE, (hh + 1) * C_NOPE)
            qp = qp_ref[r, (hh // 2) * LANES:(hh // 2 + 1) * LANES]
            zero = jnp.zeros_like(qp)
            qph = jnp.where(lo, qp, zero) if hh % 2 == 0 else jnp.where(lo, zero, qp)
            q = jnp.concatenate([qn_ref[r, cols], qph], axis=1)
            acc, _ = _attend(q, [kc[:, hh * kw:(hh + 1) * kw] for kc, _ in groups],
                             [vc[:, hh * vw:(hh + 1) * vw] for _, vc in groups], scale=scale)
            o_ref[r, cols] = (acc[:, :C_VD] / acc[:, C_VD:]).astype(o_ref.dtype)
        return carry

    lax.fori_loop(0, sq // qc, body, 0, unroll=True)


def _attn_c(qn, qp, ckv, kpe, cache_ckv, cache_kpe, wuk, wuv, j, rows):
    has_cache = cache_ckv is not None
    if has_cache:
        t, nb, roff = rows.ds, rows.db, rows.n_ctx // rows.ds
    else:
        t, nb, roff = rows.s, rows.b, 0
    nh = 2 if has_cache else 8
    hw = nh * C_NOPE
    qc = _largest_tile(t, 256)
    in_specs = [pl.BlockSpec((t, hw), lambda b, g: (b + roff, g)),
                pl.BlockSpec((t, nh * C_ROPE), lambda b, g: (b + roff, g)),
                pl.BlockSpec((t, C_KVLORA), lambda b, g: (b + roff, 0)),
                pl.BlockSpec((t, C_ROPE), lambda b, g: (b + roff, 0))]
    args = [qn, qp, ckv, kpe]
    kcw, vcw = nh * (C_NOPE + 2 * C_ROPE), nh * (C_VD + LANES)
    scratch = [pltpu.VMEM((t, kcw), MXU), pltpu.VMEM((t, vcw), MXU)]
    if has_cache:
        p = cache_ckv.shape[2]
        in_specs += [pl.BlockSpec((None, None, p, C_KVLORA), lambda b, g: (b, j, 0, 0)),
                     pl.BlockSpec((None, None, p, C_ROPE), lambda b, g: (b, j, 0, 0))]
        args += [cache_ckv, cache_kpe]
        scratch += [pltpu.VMEM((p, kcw), MXU), pltpu.VMEM((p, vcw), MXU)]
    in_specs += [pl.BlockSpec((None, C_KVLORA, hw), lambda b, g: (j, 0, g))] * 2
    args += [wuk, wuv]
    return pl.pallas_call(
        functools.partial(_attn_c_kernel, qc=qc, nh=nh, has_cache=has_cache),
        out_shape=jax.ShapeDtypeStruct((nb * t, C_HEADS * C_VD), MXU),
        grid=(nb, C_HEADS // nh),
        in_specs=in_specs,
        out_specs=pl.BlockSpec((t, hw), lambda b, g: (b, g)),
        scratch_shapes=scratch,
        compiler_params=_cparams(("parallel", "parallel"), 32),
        name="attn_c_lat" if has_cache else "attn_c_ctx",
    )(*args)


def _attn_d_kernel(*refs, has_cache):
    q_ref, k_ref, v_ref = refs[:3]
    pos = 3
    if has_cache:
        kc_ref, vc_ref = refs[pos:pos + 2]
        pos += 2
    sink_ref, o_ref = refs[pos:pos + 2]
    k2, v2 = refs[pos + 2:pos + 4]
    if has_cache:
        k2c, v2c, bias = refs[pos + 4:pos + 7]
    t = q_ref.shape[0]
    qb = Q_BLOCK
    scale = D_DH ** -0.5
    r = D_HEADS // D_KV_HEADS
    win = min(t, qb + 2 * D_WINDOW)
    lo = lax.broadcasted_iota(jnp.int32, (1, LANES), 1) < D_DH
    rowblk = lax.broadcasted_iota(jnp.int32, (r * qb, 1), 0) // qb
    if has_cache:
        qi = lax.broadcasted_iota(jnp.int32, (r * qb, 1), 0) % qb
        kj = lax.broadcasted_iota(jnp.int32, (1, win), 1)
        for p in range(bias.shape[0]):
            bias[p] = jnp.where(jnp.abs(qi - kj + p * qb) <= D_WINDOW, 0.0, NEG)

    def stage(k_r, v_r, k2_r, v2_r, g):
        gs = slice(g * D_DH, (g + 1) * D_DH)
        kg, vg = k_r[:, gs].astype(k2_r.dtype), v_r[:, gs].astype(v2_r.dtype)
        k2_r[...] = jnp.concatenate([kg, kg], axis=1)
        v2_r[...] = _with_ones(jnp.concatenate([vg, vg], axis=1))

    for g in range(D_KV_HEADS):
        stage(k_ref, v_ref, k2, v2, g)
        if has_cache:
            stage(kc_ref, vc_ref, k2c, v2c, g)
        sink = jnp.zeros((r * qb, 1), F32)
        for e in range(r):
            h = g * r + e
            sink = jnp.where(rowblk == e, sink_ref[0:1, h:h + 1], sink)

        def body(c, carry, g=g, sink=sink):
            rq = pl.ds(pl.multiple_of(c * qb, qb), qb)
            tiles = []
            for e in range(r):
                cols = slice((g * r + e - e % 2) * D_DH, (g * r + e - e % 2 + 2) * D_DH)
                qp = q_ref[rq, cols] * scale
                zero = jnp.zeros_like(qp)
                tiles.append(jnp.where(lo, qp, zero) if e % 2 == 0 else jnp.where(lo, zero, qp))
            qs = jnp.concatenate(tiles, axis=0)
            if has_cache:
                ws = pl.multiple_of(jnp.clip(c * qb - D_WINDOW, 0, t - win), qb)
                rk = pl.ds(ws, win)
                parts = [_dot_nt(qs, k2c[...]), _dot_nt(qs, k2[rk, :]) + bias[(c * qb - ws) // qb]]
                vs = [v2c[...], v2[rk, :]]
            else:
                parts = [_dot_nt(qs, k2[...])]
                vs = [v2[...]]
            m = functools.reduce(jnp.maximum, [jnp.max(x, axis=-1, keepdims=True) for x in parts])
            m = jnp.maximum(m, sink)
            acc = functools.reduce(lambda a, b: a + b,
                                   [_dot(jnp.exp(x - m), vx) for x, vx in zip(parts, vs)])
            o = acc[:, :LANES] / (acc[:, LANES:] + jnp.exp(sink - m))
            for pr in range(r // 2):
                cols = slice((g * r + 2 * pr) * D_DH, (g * r + 2 * pr + 2) * D_DH)
                pair = jnp.where(lo, o[2 * pr * qb:(2 * pr + 1) * qb, :],
                                 o[(2 * pr + 1) * qb:(2 * pr + 2) * qb, :])
                o_ref[rq, cols] = pair.astype(o_ref.dtype)
            return carry

        lax.fori_loop(0, t // qb, body, 0, unroll=4)


def _attn_d(q, k, v, cache_k, cache_v, sink, j, rows):
    d = q.shape[1]
    kvw = D_KV_HEADS * D_DH
    has_cache = cache_k is not None
    if has_cache:
        t, nb, roff = rows.ds, rows.db, rows.n_ctx // rows.ds
    else:
        t, nb, roff = rows.s, rows.b, 0
    assert t % Q_BLOCK == 0
    in_specs = [pl.BlockSpec((t, d), lambda b: (b + roff, 0)),
                pl.BlockSpec((t, kvw), lambda b: (b + roff, 0)),
                pl.BlockSpec((t, kvw), lambda b: (b + roff, 0))]
    args = [q, k, v]
    scratch = [pltpu.VMEM((t, LANES), MXU), pltpu.VMEM((t, 2 * LANES), MXU)]
    if has_cache:
        p = cache_k.shape[2]
        cspec = pl.BlockSpec((None, None, p, kvw), lambda b: (b, j, 0, 0))
        in_specs += [cspec, cspec]
        args += [cache_k, cache_v]
        win = min(t, Q_BLOCK + 2 * D_WINDOW)
        n_pat = 3 if t > win else t // Q_BLOCK
        scratch += [pltpu.VMEM((p, LANES), MXU), pltpu.VMEM((p, 2 * LANES), MXU),
                    pltpu.VMEM((n_pat, D_HEADS // D_KV_HEADS * Q_BLOCK, win), F32)]
    in_specs.append(pl.BlockSpec((None, 1, D_HEADS), lambda b: (j, 0, 0)))
    args.append(sink)
    return pl.pallas_call(
        functools.partial(_attn_d_kernel, has_cache=has_cache),
        out_shape=jax.ShapeDtypeStruct((nb * t, d), MXU),
        grid=(nb,),
        in_specs=in_specs,
        out_specs=pl.BlockSpec((t, d), lambda b: (b, 0)),
        scratch_shapes=scratch,
        compiler_params=_cparams(("parallel",), 44),
        name="attn_d_lat" if has_cache else "attn_d_ctx",
    )(*args)


def kernel(x_prompt, x_sample, cache_a_k, cache_a_v, state_b_fwd, state_b_bwd, cache_c_ckv,
           cache_c_kpe, cache_d_k, cache_d_v, c, c_ctx, ada_w, ada_b, norm_mix_pre,
           norm_mix_post, norm_ffn_pre, norm_ffn_post, a_wq, a_wk, a_wv, a_wo, a_lambda,
           a_subln, b_wq, b_wi, b_wf, b_lower, b_wg, b_gnorm, b_wo, c_wdq, c_qnorm, c_wuq,
           c_wdkv, c_kvnorm, c_wuk, c_wuv, c_wo, d_wq, d_wk, d_wv, d_sink, d_wo, ffn_wg,
           ffn_wu, ffn_wd):
    b, s, d = x_prompt.shape
    db, ds, _ = x_sample.shape
    depth = ada_w.shape[0]
    rows = _Rows(b, s, db, ds, _largest_tile(math.gcd(b * s, ds), 1024))
    n_ctx = rows.n_ctx

    cond = jnp.zeros((rows.crows, d), F32).at[:db].set(c).at[db].set(c_ctx)
    mod = _ada(cond, ada_w, ada_b)
    g3 = lambda a: a.reshape(a.shape[0], 1, a.shape[1])
    n_mix_pre, n_mix_post = g3(norm_mix_pre), g3(norm_mix_post)
    n_ffn_pre, n_ffn_post = g3(norm_ffn_pre), g3(norm_ffn_post)
    tabs = _rope_tables(ds, rows.tm)
    rope_epi = lambda tn: (_make_epi_rope(tn), _rope_extra(tabs, rows))
    ns = max(1, rows.tm // MIN_SLAB_ROWS)

    x = (x_prompt.reshape(n_ctx, d), x_sample.reshape(rows.n_lat, d))
    h = _prep(x[0], x[1], n_mix_pre, mod, 0, rows)

    flat4 = lambda a: a.reshape(a.shape[:3] + (-1,))
    wo_b = {0: _cast_mxu(a_wo), 1: _cast_mxu(b_wo), 2: _cast_mxu(c_wo), 3: _cast_mxu(d_wo)}
    wd_b = _cast_mxu(ffn_wd)
    wf = b_wf.reshape((-1,) + b_wf.shape[2:])
    outs = {k_: [] for k_ in ("a_k", "a_v", "b_f", "b_b", "c_ckv", "c_kpe", "d_k", "d_v")}
    for i in range(depth):
        m, j = i % N_MIXERS, i // N_MIXERS
        if m == 0:
            tn = _largest_tile(a_wq.shape[2], 256)
            q, k, v, kc, vc = _proj(
                h, [(a_wq, j), (a_wk, j), (a_wv, j)], _make_epi_qkv(rows, tn),
                [(tn, MXU), (tn, MXU), (tn, MXU), (tn, F32, "ctx"), (tn, F32, "ctx")],
                rows, tn, _rope_extra(tabs, rows), nsub=ns, name="a_qkv")
            o = (_attn_a(q, k, v, None, None, a_lambda, g3(a_subln), i, j, rows),
                 _attn_a(q, k, v, flat4(cache_a_k), flat4(cache_a_v), a_lambda, g3(a_subln),
                         i, j, rows))
            outs["a_k"].append(kc.reshape(b, s, A_HEADS, 2 * A_DH))
            outs["a_v"].append(vc.reshape(b, s, A_HEADS, 2 * A_DH))
        elif m == 1:
            tn = _largest_tile(b_wq.shape[2], 1024)
            q, = _proj(h, [(b_wq, j)], _epi_silu, [(tn, F32)], rows, tn, nsub=ns, name="b_q")
            vi, = _proj(h, [(b_wi, j)], _epi_store, [(tn, MXU)], rows, tn, nsub=ns, name="b_i")
            g, = _proj(h, [(b_wg, j)], _epi_store, [(tn, F32)], rows, tn, nsub=ns, name="b_g")
            fs = []
            for dr in range(2):
                bl_spec = pl.BlockSpec((None, depth, tn), lambda jn, mm, dr=dr: (dr, 0, jn))
                f, = _proj(h, [(wf, 2 * j + dr)], _make_epi_forget(i, depth), [(tn, F32)], rows,
                           tn, [(jnp.swapaxes(b_lower, 0, 1), bl_spec)], nsub=ns,
                           name="b_f%d" % dr)
                fs.append(f)
            oc, sf, sb = _gla(q, vi, fs[0], fs[1], g, g3(b_gnorm), None, None, j, rows)
            ol, = _gla(q, vi, fs[0], fs[1], g, g3(b_gnorm), state_b_fwd, state_b_bwd, j, rows)
            o = (oc, ol)
            outs["b_f"].append(sf)
            outs["b_b"].append(sb)
        elif m == 2:
            nq = c_wdq.shape[2]
            cq, = _proj(h, [(c_wdq, j)], _make_epi_rmsnorm(), [(nq, MXU)], rows, nq,
                        [(g3(c_qnorm), pl.BlockSpec((None, 1, nq), lambda jn, mm: (j, 0, 0)))],
                        name="c_dq")
            nkv = c_wdkv.shape[2]
            ext = [(g3(c_kvnorm), pl.BlockSpec((None, 1, C_KVLORA), lambda jn, mm: (j, 0, 0)))]
            ext += _rope_extra(tabs, rows)
            ckv, kpe, ckv_c, kpe_c = _proj(
                h, [(c_wdkv, j)], _make_epi_ckv(rows),
                [(C_KVLORA, MXU), (C_ROPE, MXU), (C_KVLORA, F32, "ctx"), (C_ROPE, F32, "ctx")],
                rows, nkv, ext, name="c_dkv")
            wuq = c_wuq.reshape(c_wuq.shape[0], nq, C_HEADS, C_NOPE + C_ROPE)
            wuq_n = wuq[:, :, :, :C_NOPE].reshape(-1, nq, C_HEADS * C_NOPE)
            wuq_p = wuq[:, :, :, C_NOPE:].reshape(-1, nq, C_HEADS * C_ROPE)
            tn = _largest_tile(wuq_n.shape[2], 1024)
            qn, = _proj(cq, [(wuq_n, j)], _epi_store, [(tn, MXU)], rows, tn, name="c_qn")
            tnp = wuq_p.shape[2]
            epi, ext = rope_epi(tnp)
            qp, = _proj(cq, [(wuq_p, j)], epi, [(tnp, MXU)], rows, tnp, ext, name="c_qp")
            o = (_attn_c(qn, qp, ckv, kpe, None, None, c_wuk, c_wuv, j, rows),
                 _attn_c(qn, qp, ckv, kpe, cache_c_ckv, cache_c_kpe, c_wuk, c_wuv, j, rows))
            outs["c_ckv"].append(ckv_c.reshape(b, s, C_KVLORA))
            outs["c_kpe"].append(kpe_c.reshape(b, s, C_ROPE))
        else:
            tn = _largest_tile(d_wq.shape[2], 1024)
            epi, ext = rope_epi(tn)
            q, = _proj(h, [(d_wq, j)], epi, [(tn, MXU)], rows, tn, ext, nsub=ns, name="d_q")
            kvw = d_wk.shape[2]
            epi, ext = rope_epi(kvw)
            kv_outs = [(kvw, MXU), (kvw, F32, "ctx")]
            k, kc = _proj(h, [(d_wk, j)], _with_ctx_copy(epi, rows), kv_outs, rows, kvw, ext,
                          name="d_k")
            v, vc = _proj(h, [(d_wv, j)], _with_ctx_copy(_epi_store, rows), kv_outs, rows, kvw,
                          name="d_v")
            o = (_attn_d(q, k, v, None, None, g3(d_sink), j, rows),
                 _attn_d(q, k, v, flat4(cache_d_k), flat4(cache_d_v), g3(d_sink), j, rows))
            outs["d_k"].append(kc.reshape(b, s, D_KV_HEADS, D_DH))
            outs["d_v"].append(vc.reshape(b, s, D_KV_HEADS, D_DH))

        x, h = _resid(o, wo_b[m], j, x, mod, i, 2, n_mix_post, rows, nxt=(i, 3, n_ffn_pre),
                      name="mix_out")
        tf = _largest_tile(ffn_wg.shape[2], 512)
        a, = _proj(h, [(ffn_wg, i), (ffn_wu, i)], _epi_swiglu, [(tf, MXU)], rows, tf,
                   tm=_largest_tile(rows.mt, 2048), nsub=2, name="ffn_gu")
        nxt = (i + 1, 0, n_mix_pre) if i + 1 < depth else None
        x, h = _resid(a, wd_b, i, x, mod, i, 5, n_ffn_post, rows, nxt=nxt, name="ffn_down")

    stack = lambda lst: jnp.stack(lst, axis=1)
    return (x.reshape(b, s, d), h.reshape(db, ds, d),
            stack(outs["a_k"]), stack(outs["a_v"]), stack(outs["b_f"]), stack(outs["b_b"]),
            stack(outs["c_ckv"]), stack(outs["c_kpe"]), stack(outs["d_k"]), stack(outs["d_v"]))
```
